```python
import jax, jax.numpy as jnp
from jax import lax
import numpy as np

D_MODEL = 1024
BATCH = 8
SEQ = 8192
DEPTH = 2

GRID_W = 64
CTX_LEN = 256
N_MOD = 9
D_FF = 2816
NORM_EPS = 1e-6

MLA_HEADS = 8
MLA_NOPE = 64
MLA_ROPE = 32
MLA_V = 64
MLA_Q_RANK = 384
MLA_KV_RANK = 128
MLA_IN = MLA_Q_RANK + MLA_KV_RANK + MLA_ROPE
QK_DIM = MLA_NOPE + MLA_ROPE
ATTN_SCALE = QK_DIM ** -0.5
ROPE_THETA = 10000.0
Q_BLOCK = 128

CONV_CH = 512
CONV_K = 31

IN_WIDTH = MLA_IN + 2 * CONV_CH
MIX_WIDTH = MLA_HEADS * MLA_V + CONV_CH

RWKV_HEAD = 64
RWKV_HEADS = D_MODEL // RWKV_HEAD
DECAY_LORA = 64
AAA_LORA = 64
GATE_LORA = 128
GN_EPS = RWKV_HEAD * 1e-5

N_EVEN = (DEPTH + 1) // 2
N_ODD = DEPTH // 2

kernel_name = 'hybrid_mla_conformer_rwkv7_dit'


def rms_norm(x, gain=None):
    xf = x.astype(jnp.float32)
    y = xf * lax.rsqrt(jnp.mean(xf * xf, axis=-1, keepdims=True) + NORM_EPS)
    if gain is not None:
        y = y * gain.astype(jnp.float32)
    return y.astype(x.dtype)


def layer_norm(x, gain, bias):
    xf = x.astype(jnp.float32)
    xc = xf - jnp.mean(xf, axis=-1, keepdims=True)
    y = xc * lax.rsqrt(jnp.mean(xc * xc, axis=-1, keepdims=True) + NORM_EPS)
    return (y * gain.astype(jnp.float32) + bias.astype(jnp.float32)).astype(x.dtype)


def modulate(x, shift, scale):
    return rms_norm(x) * (1 + scale) + shift


def ada_mods(cond, w, b):
    m = jax.nn.silu(cond) @ w + b
    return jnp.split(m[:, None, :], N_MOD, axis=-1)


def half_ffn(x, shift, scale, gate, w1, w3, w2):
    h = modulate(x, shift, scale)
    return x + 0.5 * gate * ((jax.nn.silu(h @ w1) * (h @ w3)) @ w2)


def axial_rope_tables(row, col, dtype):
    n_freq = MLA_ROPE // 4
    inv = ROPE_THETA ** (-jnp.arange(n_freq, dtype=jnp.float32) / n_freq)
    ang_r = row.astype(jnp.float32)[:, None] * inv
    ang_c = col.astype(jnp.float32)[:, None] * inv
    return tuple(t.astype(dtype) for t in (jnp.cos(ang_r), jnp.sin(ang_r), jnp.cos(ang_c), jnp.sin(ang_c)))


def rotate_half(x, cos, sin):
    x1, x2 = jnp.split(x, 2, axis=-1)
    return jnp.concatenate([x1 * cos - x2 * sin, x2 * cos + x1 * sin], axis=-1)


def apply_axial_rope(x, tables):
    cos_r, sin_r, cos_c, sin_c = tables
    nope, rot = x[..., :MLA_NOPE], x[..., MLA_NOPE:]
    rot_r, rot_c = jnp.split(rot, 2, axis=-1)
    return jnp.concatenate([nope, rotate_half(rot_r, cos_r, sin_r), rotate_half(rot_c, cos_c, sin_c)], axis=-1)


def mla_q(z, q_norm, w_qb, qk_norm_q):
    B, L, _ = z.shape
    q = (rms_norm(z[..., :MLA_Q_RANK], q_norm) @ w_qb).reshape(B, L, MLA_HEADS, QK_DIM)
    return rms_norm(q, qk_norm_q).transpose(0, 2, 1, 3)


def mla_kv(z, kv_norm, w_kvb, qk_norm_k):
    B, L, _ = z.shape
    c_kv = z[..., MLA_Q_RANK:MLA_Q_RANK + MLA_KV_RANK]
    k_rope = z[..., MLA_Q_RANK + MLA_KV_RANK:MLA_IN]
    kv = (rms_norm(c_kv, kv_norm) @ w_kvb).reshape(B, L, MLA_HEADS, MLA_NOPE + MLA_V)
    k_nope, v = kv[..., :MLA_NOPE], kv[..., MLA_NOPE:]
    k = jnp.concatenate([k_nope, jnp.broadcast_to(k_rope[:, :, None, :], (B, L, MLA_HEADS, MLA_ROPE))], axis=-1)
    return rms_norm(k, qk_norm_k).transpose(0, 2, 1, 3), v.transpose(0, 2, 1, 3)


def attend(q, k, v):
    s = jnp.einsum('bhqe,bhke->bhqk', q, k, preferred_element_type=jnp.float32) * ATTN_SCALE
    p = jax.nn.softmax(s, axis=-1).astype(v.dtype)
    return jnp.einsum('bhqk,bhkd->bhqd', p, v)


def blocked_attend(q, k, v):
    B, H, L, E = q.shape
    qb = q.reshape(B, H, L // Q_BLOCK, Q_BLOCK, E).transpose(2, 0, 1, 3, 4)
    out = lax.map(lambda qi: attend(qi, k, v), qb)
    return out.transpose(1, 2, 0, 3, 4).reshape(B, H, L, v.shape[-1])


def merge_heads(a):
    B, H, L, E = a.shape
    return a.transpose(0, 2, 1, 3).reshape(B, L, H * E)


def conformer_conv(u, dw_w, dw_b, n_g, n_b):
    a, g = jnp.split(u, 2, axis=-1)
    h = a * jax.nn.sigmoid(g)
    h = lax.conv_general_dilated(h, dw_w, window_strides=(1,), padding=[(CONV_K // 2, CONV_K // 2)],
                                 dimension_numbers=('NWC', 'WIO', 'NWC'), feature_group_count=CONV_CH) + dw_b
    return jax.nn.silu(layer_norm(h, n_g, n_b))


def mla_conv_mixer(h_lat, h_ctx, rope, p, need_ctx):
    w_in, q_norm, w_qb, kv_norm, w_kvb, qn_q, qn_k, dw_w, dw_b, cn_g, cn_b, w_out = p
    z_lat = h_lat @ w_in
    z_ctx = h_ctx @ (w_in if need_ctx else w_in[:, :MLA_IN])
    q_l = apply_axial_rope(mla_q(z_lat, q_norm, w_qb, qn_q), rope)
    k_l, v_l = mla_kv(z_lat, kv_norm, w_kvb, qn_k)
    k_l = apply_axial_rope(k_l, rope)
    k_c, v_c = mla_kv(z_ctx, kv_norm, w_kvb, qn_k)
    att_l = blocked_attend(q_l, jnp.concatenate([k_c, k_l], axis=2), jnp.concatenate([v_c, v_l], axis=2))
    conv_l = conformer_conv(z_lat[..., MLA_IN:], dw_w, dw_b, cn_g, cn_b)
    out_l = jnp.concatenate([merge_heads(att_l), conv_l], axis=-1) @ w_out
    if not need_ctx:
        return out_l, None
    att_c = attend(mla_q(z_ctx, q_norm, w_qb, qn_q), k_c, v_c)
    conv_c = conformer_conv(z_ctx[..., MLA_IN:], dw_w, dw_b, cn_g, cn_b)
    out_c = jnp.concatenate([merge_heads(att_c), conv_c], axis=-1) @ w_out
    return out_l, out_c


def token_shift(x):
    xp = jnp.pad(x, ((0, 0), (1, 1), (0, 0)))
    return 0.5 * (xp[:, :-2] + xp[:, 2:]) - x


def rwkv_streams(h, x_mix, w_r, w_k, w_v, w0, w1, w2, a0, a1, a2, k_k, k_a):
    B, L, _ = h.shape
    heads = lambda t: t.reshape(B, L, RWKV_HEADS, RWKV_HEAD).astype(jnp.float32)
    xx = token_shift(h)
    xr, xw, xk, xv, xa, xg = (h + xx * x_mix[j] for j in range(6))
    r = heads(xr @ w_r)
    k = xk @ w_k
    v = heads(xv @ w_v)
    kk = heads(k * k_k)
    kk = kk / jnp.maximum(jnp.sqrt(jnp.sum(kk * kk, axis=-1, keepdims=True)), 1e-12)
    per_dir = []
    for d in range(2):
        w_pre = (w0[d] + jnp.tanh(xw @ w1[d]) @ w2[d]).astype(jnp.float32)
        w_log = -jax.nn.softplus(-w_pre) - 0.5
        decay = heads(jnp.exp(-jnp.exp(w_log)))
        a = jax.nn.sigmoid(a0[d] + (xa @ a1[d]) @ a2[d])
        k_d = heads(k * (1 + (a - 1) * k_a))
        per_dir.append((decay, k_d, -kk, kk * heads(a)))
    return r, v, xg, per_dir


def wkv_scan(state0, r, decay, k, a_vec, b_vec, v, reverse):
    def step(S, inp):
        r_t, w_t, k_t, a_t, b_t, v_t = inp
        sa = jnp.einsum('bhvk,bhk->bhv', S, a_t)
        S = S * w_t[:, :, None, :] + sa[..., None] * b_t[:, :, None, :] + v_t[..., None] * k_t[:, :, None, :]
        return S, jnp.einsum('bhvk,bhk->bhv', S, r_t)
    xs = tuple(jnp.moveaxis(t, 1, 0) for t in (r, decay, k, a_vec, b_vec, v))
    S, ys = lax.scan(step, state0, xs, reverse=reverse)
    return S, jnp.moveaxis(ys, 0, 1)


def rwkv_readout(y, r, v, k_mean, xg, r_k, ln_g, ln_b, g1, g2, w_o, dtype):
    B, L = y.shape[:2]
    yc = y - jnp.mean(y, axis=-1, keepdims=True)
    yn = (yc * lax.rsqrt(jnp.mean(yc * yc, axis=-1, keepdims=True) + GN_EPS)).reshape(B, L, D_MODEL)
    yn = yn * ln_g.astype(jnp.float32) + ln_b.astype(jnp.float32)
    bonus = (jnp.sum(r * k_mean * r_k.astype(jnp.float32), axis=-1, keepdims=True) * v).reshape(B, L, D_MODEL)
    g = jax.nn.sigmoid(xg @ g1) @ g2
    return ((yn + bonus).astype(dtype) * g) @ w_o


def rwkv_mixer(h_lat, h_ctx, p, need_ctx):
    (x_mix, w_r, w_k, w_v, w0, w1, w2, a0, a1, a2, g1, g2, k_k, k_a, r_k, ln_g, ln_b, w_o) = p
    sp = (x_mix, w_r, w_k, w_v, w0, w1, w2, a0, a1, a2, k_k, k_a)
    r_l, v_l, xg_l, dirs_l = rwkv_streams(h_lat, *sp)
    r_c, v_c, xg_c, dirs_c = rwkv_streams(h_ctx, *sp)
    s0 = jnp.zeros((h_lat.shape[0], RWKV_HEADS, RWKV_HEAD, RWKV_HEAD), jnp.float32)
    y_l, y_c = [], []
    for d, rev in enumerate((False, True)):
        s_ctx, yc = wkv_scan(s0, r_c, *dirs_c[d], v_c, rev)
        _, yl = wkv_scan(s_ctx, r_l, *dirs_l[d], v_l, rev)
        y_l.append(yl)
        y_c.append(yc)
    ro = (r_k, ln_g, ln_b, g1, g2, w_o)
    out_l = rwkv_readout(y_l[0] + y_l[1], r_l, v_l, 0.5 * (dirs_l[0][1] + dirs_l[1][1]), xg_l, *ro, h_lat.dtype)
    if not need_ctx:
        return out_l, None
    out_c = rwkv_readout(y_c[0] + y_c[1], r_c, v_c, 0.5 * (dirs_c[0][1] + dirs_c[1][1]), xg_c, *ro, h_ctx.dtype)
    return out_l, out_c


def setup_inputs(seed: int = 0) -> dict:
    key = jax.random.key(seed)
    ks = iter(jax.random.split(key, 48))
    NE, NO, D, H, N = N_EVEN, N_ODD, D_MODEL, RWKV_HEADS, RWKV_HEAD

    def nrm(shape, scale):
        return scale * jax.random.normal(next(ks), shape, jnp.float32)

    def unif(shape, lo, hi):
        return jax.random.uniform(next(ks), shape, jnp.float32, lo, hi)

    def gain(shape):
        return 1.0 + nrm(shape, 0.02)

    return {
        'x': nrm((BATCH, SEQ, D), 1.0),
        'c': nrm((BATCH, D), 1.0),
        'ctx': nrm((BATCH, CTX_LEN, D), 1.0),
        'c_ctx': nrm((D,), 1.0),
        'ada_w': nrm((DEPTH, D, N_MOD * D), 0.5 * D ** -0.5),
        'ada_b': nrm((DEPTH, N_MOD * D), 0.02),
        'ffn_w1': nrm((DEPTH, 2, D, D_FF), D ** -0.5),
        'ffn_w3': nrm((DEPTH, 2, D, D_FF), D ** -0.5),
        'ffn_w2': nrm((DEPTH, 2, D_FF, D), D_FF ** -0.5),
        'mla_w_in': nrm((NE, D, IN_WIDTH), D ** -0.5),
        'mla_q_norm': gain((NE, MLA_Q_RANK)),
        'mla_w_qb': nrm((NE, MLA_Q_RANK, MLA_HEADS * QK_DIM), MLA_Q_RANK ** -0.5),
        'mla_kv_norm': gain((NE, MLA_KV_RANK)),
        'mla_w_kvb': nrm((NE, MLA_KV_RANK, MLA_HEADS * (MLA_NOPE + MLA_V)), MLA_KV_RANK ** -0.5),
        'mla_qk_norm_q': gain((NE, QK_DIM)),
        'mla_qk_norm_k': gain((NE, QK_DIM)),
        'conv_dw_w': nrm((NE, CONV_K, 1, CONV_CH), CONV_K ** -0.5),
        'conv_dw_b': nrm((NE, CONV_CH), 0.02),
        'conv_norm_g': gain((NE, CONV_CH)),
        'conv_norm_b': nrm((NE, CONV_CH), 0.02),
        'mix_w_out': nrm((NE, MIX_WIDTH, D), MIX_WIDTH ** -0.5),
        'rwkv_x_mix': unif((NO, 6, D), 0.0, 1.0),
        'rwkv_w_r': nrm((NO, D, D), D ** -0.5),
        'rwkv_w_k': nrm((NO, D, D), D ** -0.5),
        'rwkv_w_v': nrm((NO, D, D), D ** -0.5),
        'rwkv_w0': unif((NO, 2, D), -6.0, -1.0),
        'rwkv_w1': nrm((NO, 2, D, DECAY_LORA), D ** -0.5),
        'rwkv_w2': nrm((NO, 2, DECAY_LORA, D), 0.5 * DECAY_LORA ** -0.5),
        'rwkv_a0': nrm((NO, 2, D), 0.1),
        'rwkv_a1': nrm((NO, 2, D, AAA_LORA), D ** -0.5),
        'rwkv_a2': nrm((NO, 2, AAA_LORA, D), 0.5 * AAA_LORA ** -0.5),
        'rwkv_g1': nrm((NO, D, GATE_LORA), D ** -0.5),
        'rwkv_g2': nrm((NO, GATE_LORA, D), GATE_LORA ** -0.5),
        'rwkv_k_k': 0.85 + nrm((NO, D), 0.02),
        'rwkv_k_a': gain((NO, D)),
        'rwkv_r_k': nrm((NO, H, N), 0.1),
        'rwkv_ln_g': gain((NO, D)),
        'rwkv_ln_b': nrm((NO, D), 0.02),
        'rwkv_w_o': nrm((NO, D, D), D ** -0.5),
    }


def reference(x, c, ctx, c_ctx, ada_w, ada_b, ffn_w1, ffn_w3, ffn_w2,
              mla_w_in, mla_q_norm, mla_w_qb, mla_kv_norm, mla_w_kvb, mla_qk_norm_q, mla_qk_norm_k,
              conv_dw_w, conv_dw_b, conv_norm_g, conv_norm_b, mix_w_out,
              rwkv_x_mix, rwkv_w_r, rwkv_w_k, rwkv_w_v, rwkv_w0, rwkv_w1, rwkv_w2,
              rwkv_a0, rwkv_a1, rwkv_a2, rwkv_g1, rwkv_g2, rwkv_k_k, rwkv_k_a, rwkv_r_k,
              rwkv_ln_g, rwkv_ln_b, rwkv_w_o):
    rows = x.shape[1] // GRID_W
    row = jnp.repeat(jnp.arange(rows), GRID_W)
    col = jnp.tile(jnp.arange(GRID_W), rows)
    rope = axial_rope_tables(row, col, x.dtype)
    for i in range(DEPTH):
        last = i == DEPTH - 1
        j = i // 2
        m_l = ada_mods(c, ada_w[i], ada_b[i])
        m_c = ada_mods(c_ctx[None], ada_w[i], ada_b[i])
        f1 = (ffn_w1[i, 0], ffn_w3[i, 0], ffn_w2[i, 0])
        f2 = (ffn_w1[i, 1], ffn_w3[i, 1], ffn_w2[i, 1])
        x = half_ffn(x, *m_l[0:3], *f1)
        ctx = half_ffn(ctx, *m_c[0:3], *f1)
        h_l = modulate(x, m_l[3], m_l[4])
        h_c = modulate(ctx, m_c[3], m_c[4])
        if i % 2 == 0:
            p = (mla_w_in[j], mla_q_norm[j], mla_w_qb[j], mla_kv_norm[j], mla_w_kvb[j], mla_qk_norm_q[j],
                 mla_qk_norm_k[j], conv_dw_w[j], conv_dw_b[j], conv_norm_g[j], conv_norm_b[j], mix_w_out[j])
            out_l, out_c = mla_conv_mixer(h_l, h_c, rope, p, not last)
        else:
            p = (rwkv_x_mix[j], rwkv_w_r[j], rwkv_w_k[j], rwkv_w_v[j], rwkv_w0[j], rwkv_w1[j], rwkv_w2[j],
                 rwkv_a0[j], rwkv_a1[j], rwkv_a2[j], rwkv_g1[j], rwkv_g2[j], rwkv_k_k[j], rwkv_k_a[j],
                 rwkv_r_k[j], rwkv_ln_g[j], rwkv_ln_b[j], rwkv_w_o[j])
            out_l, out_c = rwkv_mixer(h_l, h_c, p, not last)
        x = x + m_l[5] * out_l
        x = half_ffn(x, *m_l[6:9], *f2)
        if not last:
            ctx = ctx + m_c[5] * out_c
            ctx = half_ffn(ctx, *m_c[6:9], *f2)
    return x
```

```python
import functools
import math

import jax
import jax.numpy as jnp
from jax import lax
from jax.experimental import pallas as pl
from jax.experimental.pallas import tpu as pltpu

F32 = jnp.float32
BF16 = jnp.bfloat16

LANES = 128
VMEM_LIMIT = 56 * 1024 * 1024

NORM_EPS = 1e-6
N_MOD = 9
GRID_W = 64
ROPE_THETA = 10000.0

MLA_HEADS = 8
MLA_NOPE = 64
MLA_ROPE = 32
MLA_V = 64
MLA_Q_RANK = 384
MLA_KV_RANK = 128
QK_DIM = MLA_NOPE + MLA_ROPE
ATTN_SCALE = QK_DIM ** -0.5
CONV_CH = 512
CONV_K = 31
CONV_HALO = 16

RWKV_HEAD = 64
SCAN_CHUNK = 64
DECAY_SCALE = math.exp(-0.5)


def _dot(a, b):
    return jnp.dot(a.astype(BF16), b.astype(BF16), preferred_element_type=F32)


def _dot_nt(a, b):
    return lax.dot_general(a.astype(BF16), b.astype(BF16), (((1,), (1,)), ((), ())),
                           preferred_element_type=F32)


def _dot_tn(a, b):
    return lax.dot_general(a.astype(BF16), b.astype(BF16), (((0,), (0,)), ((), ())),
                           preferred_element_type=F32)


def _dot_sel(x, sel):
    hi = x.astype(BF16)
    lo = (x - hi.astype(F32)).astype(BF16)
    return (jnp.dot(hi, sel, preferred_element_type=F32)
            + jnp.dot(lo, sel, preferred_element_type=F32))


def _rms(x):
    return x * lax.rsqrt(jnp.mean(x * x, axis=-1, keepdims=True) + NORM_EPS)


def _sigmoid(x):
    return 1.0 / (1.0 + jnp.exp(-x))


def _modulate(x, m_ref, first):
    return _rms(x) * (1.0 + m_ref[first + 1:first + 2, :]) + m_ref[first:first + 1, :]


def _const_spec(shape):
    zeros = (0,) * len(shape)
    return pl.BlockSpec(shape, lambda *_: zeros, pipeline_mode=pl.Buffered(1))


def _params(*sem):
    return pltpu.CompilerParams(dimension_semantics=sem, vmem_limit_bytes=VMEM_LIMIT)


def _mod_spec(mods):
    d = mods.shape[-1]
    if mods.shape[0] == 1:
        return pl.BlockSpec((None, N_MOD, d), lambda b, i: (0, 0, 0))
    return pl.BlockSpec((None, N_MOD, d), lambda b, i: (b, 0, 0))


def _tile(length, pref):
    t = min(pref, length)
    assert length % t == 0, (length, t)
    return t


def _ada_kernel(c_ref, w_ref, b_ref, o_ref):
    c = c_ref[...]
    s = c * _sigmoid(c)
    w = w_ref[...]
    s_hi = s.astype(BF16)
    s_lo = (s - s_hi.astype(F32)).astype(BF16)
    w_hi = w.astype(BF16)
    w_lo = (w - w_hi.astype(F32)).astype(BF16)
    acc = jnp.dot(s_hi, w_hi, preferred_element_type=F32)
    acc += jnp.dot(s_hi, w_lo, preferred_element_type=F32)
    acc += jnp.dot(s_lo, w_hi, preferred_element_type=F32)
    o_ref[...] = acc + b_ref[...]


def _ada_mods(cond, w, b):
    r, d = cond.shape
    n = w.shape[1]
    out = pl.pallas_call(
        _ada_kernel,
        grid=(n // d,),
        in_specs=[pl.BlockSpec((r, d), lambda j: (0, 0)),
                  pl.BlockSpec((d, d), lambda j: (0, j)),
                  pl.BlockSpec((1, d), lambda j: (0, j))],
        out_specs=pl.BlockSpec((r, d), lambda j: (0, j)),
        out_shape=jax.ShapeDtypeStruct((r, n), F32),
        compiler_params=_params("parallel"),
        name="ada_mods",
    )(cond, w, b.reshape(1, n))
    return out.reshape(r, n // d, d)


def _ffn_kernel(x_ref, m_ref, w1_ref, w3_ref, w2_ref, o_ref, *, first, f_chunk):
    x = x_ref[...]
    h = _modulate(x, m_ref, first).astype(BF16)
    acc = None
    for c in range(w1_ref.shape[1] // f_chunk):
        cols = slice(c * f_chunk, (c + 1) * f_chunk)
        a = jnp.dot(h, w1_ref[:, cols], preferred_element_type=F32)
        b = jnp.dot(h, w3_ref[:, cols], preferred_element_type=F32)
        g = (a * _sigmoid(a) * b).astype(BF16)
        part = jnp.dot(g, w2_ref[cols, :], preferred_element_type=F32)
        acc = part if acc is None else acc + part
    o_ref[...] = x + (0.5 * m_ref[first + 2:first + 3, :]) * acc


def _half_ffn(x, mods, first, w1, w3, w2):
    bsz, length, d = x.shape
    f = w1.shape[1]
    tm = _tile(length, 512)
    f_chunk = f // 2 if (f // 2) % LANES == 0 else f
    return pl.pallas_call(
        functools.partial(_ffn_kernel, first=first, f_chunk=f_chunk),
        grid=(bsz, length // tm),
        in_specs=[pl.BlockSpec((None, tm, d), lambda b, i: (b, i, 0)),
                  _mod_spec(mods),
                  _const_spec((d, f)), _const_spec((d, f)), _const_spec((f, d))],
        out_specs=pl.BlockSpec((None, tm, d), lambda b, i: (b, i, 0)),
        out_shape=jax.ShapeDtypeStruct(x.shape, F32),
        compiler_params=_params("parallel", "parallel"),
        name="half_ffn",
    )(x, mods, w1, w3, w2)


def _head_rms(t, width, grp_ref, grp_t_ref):
    ss = _dot_sel(t * t, grp_ref[...])
    r = lax.rsqrt(ss * (1.0 / width) + NORM_EPS)
    return t * _dot_sel(r, grp_t_ref[...])


def _rope(t, cos, sin_lo, sin_hi):
    n = t.shape[1]
    rep = n // cos.shape[1]
    cos, sin_lo, sin_hi = (jnp.tile(u, (1, rep)) for u in (cos, sin_lo, sin_hi))
    return t * cos + pltpu.roll(t, n - 8, 1) * sin_lo + pltpu.roll(t, 8, 1) * sin_hi


def _mla_in_kernel(x_ref, m_ref, w_in_ref, qn_ref, wq_ref, kvn_ref, wk_ref, wkr_ref, wv_ref,
                   gq_ref, gk_ref, cos_ref, sl_ref, sh_ref, grp_ref, grp_t_ref,
                   q_ref, k_ref, v_ref, hg_ref):
    h = _modulate(x_ref[...], m_ref, 3)
    z = _dot(h, w_in_ref[...])
    o_kv = MLA_Q_RANK
    o_kr = o_kv + MLA_KV_RANK
    o_a = o_kr + LANES
    o_g = o_a + CONV_CH
    c_q = _rms(z[:, :o_kv]) * qn_ref[...]
    c_kv = _rms(z[:, o_kv:o_kr]) * kvn_ref[...]
    hg_ref[...] = z[:, o_a:o_g] * _sigmoid(z[:, o_g:o_g + CONV_CH])
    cos, s_lo, s_hi = cos_ref[...], sl_ref[...], sh_ref[...]
    q = _head_rms(_dot(c_q, wq_ref[...]), QK_DIM, grp_ref, grp_t_ref) * gq_ref[...]
    q_ref[...] = _rope(q, cos, s_lo, s_hi).astype(BF16)
    k = _dot(c_kv, wk_ref[...]) + _dot(z[:, o_kr:o_a], wkr_ref[...])
    k = _head_rms(k, QK_DIM, grp_ref, grp_t_ref) * gk_ref[...]
    k_ref[...] = _rope(k, cos, s_lo, s_hi).astype(BF16)
    v_ref[...] = _dot(c_kv, wv_ref[...]).astype(BF16)


def _mla_in(x, mods, p, rope_tabs):
    bsz, length, d = x.shape
    tm = _tile(length, 512)
    hs = MLA_HEADS * LANES
    cos, s_lo, s_hi = rope_tabs
    tab_spec = pl.BlockSpec((tm, LANES), lambda b, i: (i, 0))
    tok = lambda w: pl.BlockSpec((None, tm, w), lambda b, i: (b, i, 0))
    consts = [p["w_in"], p["q_norm"], p["w_q"], p["kv_norm"], p["w_k"], p["w_kr"], p["w_v"],
              p["gain_q"], p["gain_k"]]
    return pl.pallas_call(
        _mla_in_kernel,
        grid=(bsz, length // tm),
        in_specs=([tok(d), _mod_spec(mods)] + [_const_spec(c.shape) for c in consts]
                  + [tab_spec, tab_spec, tab_spec, _const_spec(p["grp"].shape), _const_spec(p["grp_t"].shape)]),
        out_specs=[tok(hs), tok(hs), tok(MLA_HEADS * MLA_V), tok(CONV_CH)],
        out_shape=[jax.ShapeDtypeStruct((bsz, length, hs), BF16),
                   jax.ShapeDtypeStruct((bsz, length, hs), BF16),
                   jax.ShapeDtypeStruct((bsz, length, MLA_HEADS * MLA_V), BF16),
                   jax.ShapeDtypeStruct((bsz, length, CONV_CH), F32)],
        compiler_params=_params("parallel", "parallel"),
        name="mla_in",
    )(x, mods, *consts, cos, s_lo, s_hi, p["grp"], p["grp_t"])


def _attn_kernel(q_ref, k_ref, v_ref, o_ref, *, tk):
    tq = q_ref.shape[0]
    n_chunks = k_ref.shape[0] // tk
    lane = lax.broadcasted_iota(jnp.int32, (tq, LANES), 1)
    outs = []
    for j in range(2):
        q = q_ref[:, j * LANES:(j + 1) * LANES]

        def body(c, carry, q=q, j=j):
            m, l, acc = carry
            rows = pl.ds(pl.multiple_of(c * tk, tk), tk)
            s = _dot_nt(q, k_ref[rows, j * LANES:(j + 1) * LANES])
            m_new = jnp.maximum(m, jnp.max(s, axis=-1, keepdims=True))
            alpha = jnp.exp(m - m_new)
            p = jnp.exp(s - m_new)
            l = alpha * l + jnp.sum(p, axis=-1, keepdims=True)
            acc = alpha * acc + jnp.dot(p.astype(BF16), v_ref[rows, :], preferred_element_type=F32)
            return m_new, l, acc

        init = (jnp.full((tq, 1), -jnp.inf, F32), jnp.zeros((tq, 1), F32), jnp.zeros((tq, LANES), F32))
        _, l, acc = lax.fori_loop(0, n_chunks, body, init)
        outs.append(acc / l)
    o_ref[...] = jnp.where(lane < MLA_V, outs[0], outs[1]).astype(BF16)


def _attention(q, k, v):
    bsz, lq, _ = q.shape
    lk = k.shape[1]
    tq = _tile(lq, 256)
    tk = next(t for t in (768, 512, 256, lk) if lk % t == 0)
    return pl.pallas_call(
        functools.partial(_attn_kernel, tk=tk),
        grid=(bsz, MLA_HEADS // 2, lq // tq),
        in_specs=[pl.BlockSpec((None, tq, 2 * LANES), lambda b, h, i: (b, i, h)),
                  pl.BlockSpec((None, lk, 2 * LANES), lambda b, h, i: (b, 0, h)),
                  pl.BlockSpec((None, lk, 2 * MLA_V), lambda b, h, i: (b, 0, h))],
        out_specs=pl.BlockSpec((None, tq, 2 * MLA_V), lambda b, h, i: (b, i, h)),
        out_shape=jax.ShapeDtypeStruct((bsz, lq, MLA_HEADS * MLA_V), BF16),
        compiler_params=_params("parallel", "parallel", "parallel"),
        name="attention",
    )(q, k, v)


def _conv_kernel(h_ref, hp_ref, hn_ref, w_ref, b_ref, g_ref, be_ref, o_ref, ext_ref, *, rows):
    i = pl.program_id(1)
    tm = h_ref.shape[0]
    ext_ref[0:CONV_HALO, :] = jnp.where(i > 0, hp_ref[...], 0.0)
    ext_ref[CONV_HALO:CONV_HALO + tm, :] = h_ref[...]
    ext_ref[CONV_HALO + tm:, :] = jnp.where(i < pl.num_programs(1) - 1, hn_ref[...], 0.0)
    first = CONV_HALO - CONV_K // 2
    for r0 in range(0, tm, rows):
        acc = jnp.zeros((rows, CONV_CH), F32)
        for j in range(CONV_K):
            acc += ext_ref[first + r0 + j:first + r0 + j + rows, :] * w_ref[j:j + 1, :]
        acc += b_ref[...]
        xc = acc - jnp.mean(acc, axis=-1, keepdims=True)
        y = xc * lax.rsqrt(jnp.mean(xc * xc, axis=-1, keepdims=True) + NORM_EPS)
        y = y * g_ref[...] + be_ref[...]
        o_ref[r0:r0 + rows, :] = (y * _sigmoid(y)).astype(BF16)


def _conformer_conv(hg, p):
    bsz, length, ch = hg.shape
    tm = _tile(length, 256)
    nh = tm // CONV_HALO
    last = length // CONV_HALO - 1
    return pl.pallas_call(
        functools.partial(_conv_kernel, rows=32),
        grid=(bsz, length // tm),
        in_specs=[pl.BlockSpec((None, tm, ch), lambda b, i: (b, i, 0)),
                  pl.BlockSpec((None, CONV_HALO, ch), lambda b, i: (b, jnp.maximum(i * nh - 1, 0), 0)),
                  pl.BlockSpec((None, CONV_HALO, ch), lambda b, i: (b, jnp.minimum((i + 1) * nh, last), 0)),
                  _const_spec((CONV_K, ch)), _const_spec((1, ch)), _const_spec((1, ch)), _const_spec((1, ch))],
        out_specs=pl.BlockSpec((None, tm, ch), lambda b, i: (b, i, 0)),
        out_shape=jax.ShapeDtypeStruct((bsz, length, ch), BF16),
        scratch_shapes=[pltpu.VMEM((tm + 2 * CONV_HALO, ch), F32)],
        compiler_params=_params("parallel", "parallel"),
        name="conformer_conv",
    )(hg, hg, hg, p["dw_w"], p["dw_b"], p["cn_g"], p["cn_b"])


def _mix_out_kernel(x_ref, m_ref, a_ref, c_ref, wa_ref, wc_ref, o_ref):
    out = (jnp.dot(a_ref[...], wa_ref[...], preferred_element_type=F32)
           + jnp.dot(c_ref[...], wc_ref[...], preferred_element_type=F32))
    o_ref[...] = x_ref[...] + m_ref[5:6, :] * out


def _mix_out(x, mods, att, conv, p):
    bsz, length, d = x.shape
    tm = _tile(length, 512)
    tok = lambda w: pl.BlockSpec((None, tm, w), lambda b, i: (b, i, 0))
    return pl.pallas_call(
        _mix_out_kernel,
        grid=(bsz, length // tm),
        in_specs=[tok(d), _mod_spec(mods), tok(att.shape[-1]), tok(conv.shape[-1]),
                  _const_spec(p["w_out_att"].shape), _const_spec(p["w_out_conv"].shape)],
        out_specs=tok(d),
        out_shape=jax.ShapeDtypeStruct(x.shape, F32),
        compiler_params=_params("parallel", "parallel"),
        name="mix_out",
    )(x, mods, att, conv, p["w_out_att"], p["w_out_conv"])


def _rwkv_in_kernel(x_ref, xp_ref, xn_ref, m_ref, xmix_ref, wr_ref, wk_ref, wv_ref, w0_ref, w1_ref, w2_ref,
                    a0_ref, a1_ref, a2_ref, g1_ref, g2_ref, kk_ref, ka_ref, rk_ref, grp_ref, grp_t_ref,
                    r_out, v_out, kkn_out, g_out, bonus_out, lw_out, kd_out, b_out):
    i = pl.program_id(1)
    tm = x_ref.shape[0]
    h = _modulate(x_ref[...], m_ref, 3)
    h_before = jnp.where(i > 0, _modulate(xp_ref[7:8, :], m_ref, 3), 0.0)
    h_after = jnp.where(i < pl.num_programs(1) - 1, _modulate(xn_ref[0:1, :], m_ref, 3), 0.0)
    row = lax.broadcasted_iota(jnp.int32, (tm, 1), 0)
    h_prev = jnp.where(row == 0, h_before, pltpu.roll(h, 1, 0))
    h_next = jnp.where(row == tm - 1, h_after, pltpu.roll(h, tm - 1, 0))
    xx = 0.5 * (h_prev + h_next) - h
    xr, xw, xk, xv, xa, xg = (h + xx * xmix_ref[j:j + 1, :] for j in range(6))
    r = _dot(xr, wr_ref[...])
    k = _dot(xk, wk_ref[...])
    v = _dot(xv, wv_ref[...])
    kk = k * kk_ref[...]
    norm = jnp.maximum(jnp.sqrt(_dot_sel(kk * kk, grp_ref[...])), 1e-12)
    kk = kk * _dot_sel(1.0 / norm, grp_t_ref[...])
    g_out[...] = _dot(_sigmoid(_dot(xg, g1_ref[...])), g2_ref[...])
    k_sum = None
    for d in range(2):
        w_pre = w0_ref[d:d + 1, :] + _dot(jnp.tanh(_dot(xw, w1_ref[d])), w2_ref[d])
        lw_out[d] = -DECAY_SCALE * _sigmoid(w_pre)
        a = _sigmoid(a0_ref[d:d + 1, :] + _dot(_dot(xa, a1_ref[d]), a2_ref[d]))
        k_d = k * (1.0 + (a - 1.0) * ka_ref[...])
        kd_out[d] = k_d
        b_out[d] = kk * a
        k_sum = k_d if k_sum is None else k_sum + k_d
    r_out[...] = r
    v_out[...] = v
    kkn_out[...] = kk
    head_dot = _dot_sel(r * (0.5 * k_sum) * rk_ref[...], grp_ref[...])
    bonus_out[...] = _dot_sel(head_dot, grp_t_ref[...]) * v


def _rwkv_in(x, mods, p):
    bsz, length, d = x.shape
    tm = _tile(length, 256)
    nb = tm // 8
    last = length // 8 - 1
    tok = pl.BlockSpec((None, tm, d), lambda b, i: (b, i, 0))
    tok2 = pl.BlockSpec((None, 2, tm, d), lambda b, i: (b, 0, i, 0))
    consts = [p[n] for n in ("x_mix", "w_r", "w_k", "w_v", "w0", "w1", "w2", "a0", "a1", "a2", "g1", "g2",
                             "k_k", "k_a", "r_k", "grp", "grp_t")]
    one = jax.ShapeDtypeStruct((bsz, length, d), F32)
    two = jax.ShapeDtypeStruct((bsz, 2, length, d), F32)
    return pl.pallas_call(
        _rwkv_in_kernel,
        grid=(bsz, length // tm),
        in_specs=([tok,
                   pl.BlockSpec((None, 8, d), lambda b, i: (b, jnp.maximum(i * nb - 1, 0), 0)),
                   pl.BlockSpec((None, 8, d), lambda b, i: (b, jnp.minimum((i + 1) * nb, last), 0)),
                   _mod_spec(mods)] + [_const_spec(c.shape) for c in consts]),
        out_specs=[tok, tok, tok, tok, tok, tok2, tok2, tok2],
        out_shape=[one, one, one, one, one, two, two, two],
        compiler_params=_params("parallel", "parallel"),
        name="rwkv_in",
    )(x, x, x, mods, *consts)


def _scan_kernel(r_ref, v_ref, kk_ref, lw_ref, kd_ref, b_ref, s0_ref, y_ref, st_ref, state_ref):
    d = pl.program_id(1)
    c = pl.program_id(2)
    cs = lw_ref.shape[0]
    n_pairs = state_ref.shape[0]

    @pl.when(c == 0)
    def _():
        state_ref[...] = s0_ref[...]

    rev = d == 1
    sign = 1 - 2 * d
    ti = lax.broadcasted_iota(jnp.int32, (cs, cs), 0)
    si = lax.broadcasted_iota(jnp.int32, (cs, cs), 1)
    incl = jnp.where((ti - si) * sign >= 0, 1.0, 0.0).astype(BF16)
    lw = lw_ref[...]
    l1 = lw.astype(BF16)
    rem = lw - l1.astype(F32)
    l2 = rem.astype(BF16)
    l3 = (rem - l2.astype(F32)).astype(BF16)
    cum = (jnp.dot(incl, l1, preferred_element_type=F32) + jnp.dot(incl, l2, preferred_element_type=F32)
           + jnp.dot(incl, l3, preferred_element_type=F32))
    total = jnp.where(rev, cum[0:1, :], cum[cs - 1:cs, :])
    p_inv = jnp.exp(-cum)
    to_end = jnp.exp(total - cum)
    kd = kd_ref[...]
    bv = b_ref[...]
    r_t = r_ref[...] * jnp.exp(cum)
    a_t = -kk_ref[...] * jnp.exp(cum - lw)
    k_t = kd * p_inv
    b_t = bv * p_inv
    k_e = kd * to_end
    b_e = bv * to_end
    v = v_ref[...]
    p_tot = jnp.exp(total)

    assert cs & (cs - 1) == 0
    t2 = lax.broadcasted_iota(jnp.int32, (2 * cs, 2 * cs), 0) & (cs - 1)
    s2 = lax.broadcasted_iota(jnp.int32, (2 * cs, 2 * cs), 1) & (cs - 1)
    strict2 = (t2 - s2) * sign > 0
    incl2 = (t2 - s2) * sign >= 0
    even = lax.broadcasted_iota(jnp.int32, (cs, LANES), 1) < RWKV_HEAD
    vi = lax.broadcasted_iota(jnp.int32, (LANES, LANES), 0) < RWKV_HEAD
    ki = lax.broadcasted_iota(jnp.int32, (LANES, LANES), 1) < RWKV_HEAD
    same_head = vi == ki

    def stack(t):
        return jnp.concatenate([jnp.where(even, t, 0.0), jnp.where(even, 0.0, t)], axis=0)

    def twice(t):
        return jnp.concatenate([t, t], axis=0)

    for p in range(n_pairs):
        sl = slice(p * LANES, (p + 1) * LANES)
        state = state_ref[p]
        a_s, r_s, b_s, k_s = stack(a_t[:, sl]), stack(r_t[:, sl]), stack(b_t[:, sl]), stack(k_t[:, sl])
        cross = _dot_nt(jnp.concatenate([a_s, r_s], axis=0), jnp.concatenate([b_s, k_s], axis=0))
        m_ab = jnp.where(strict2, cross[:2 * cs, :2 * cs], 0.0)
        m_ak = jnp.where(strict2, cross[:2 * cs, 2 * cs:], 0.0)
        m_rb = jnp.where(incl2, cross[2 * cs:, :2 * cs], 0.0)
        m_rk = jnp.where(incl2, cross[2 * cs:, 2 * cs:], 0.0)
        v2 = twice(v[:, sl])
        u = twice(_dot_nt(a_t[:, sl], state)) + _dot(m_ak, v2)
        m = m_ab
        n_sq = max(1, (cs - 1).bit_length())
        for it in range(n_sq):
            u = u + _dot(m, u)
            if it + 1 < n_sq:
                m = _dot(m, m)
        uv = jnp.concatenate([u, v2], axis=0)
        y2 = twice(_dot_nt(r_t[:, sl], state)) + _dot(jnp.concatenate([m_rb, m_rk], axis=1), uv)
        y_ref[:, sl] = jnp.where(even, y2[:cs], y2[cs:])
        upd = _dot_tn(uv, jnp.concatenate([stack(b_e[:, sl]), stack(k_e[:, sl])], axis=0))
        state_ref[p] = state * p_tot[:, sl] + jnp.where(same_head, upd, 0.0)

    @pl.when(c == pl.num_programs(2) - 1)
    def _():
        st_ref[...] = state_ref[...]


def _wkv_scan(r, v, kk, lw, kd, bv, s0):
    bsz, length, d = r.shape
    cs = SCAN_CHUNK
    nc = length // cs
    n_pairs = d // LANES
    chunk = lambda d_, c: c + d_ * (nc - 1 - 2 * c)
    shared = pl.BlockSpec((None, cs, d), lambda b, d_, c: (b, chunk(d_, c), 0))
    per_dir = pl.BlockSpec((None, None, cs, d), lambda b, d_, c: (b, d_, chunk(d_, c), 0))
    st_spec = pl.BlockSpec((None, None, n_pairs, LANES, LANES), lambda b, d_, c: (b, d_, 0, 0, 0))
    return pl.pallas_call(
        _scan_kernel,
        grid=(bsz, 2, nc),
        in_specs=[shared, shared, shared, per_dir, per_dir, per_dir, st_spec],
        out_specs=[per_dir, st_spec],
        out_shape=[jax.ShapeDtypeStruct((bsz, 2, length, d), F32),
                   jax.ShapeDtypeStruct(s0.shape, F32)],
        scratch_shapes=[pltpu.VMEM((n_pairs, LANES, LANES), F32)],
        compiler_params=_params("parallel", "parallel", "arbitrary"),
        name="wkv_scan",
    )(r, v, kk, lw, kd, bv, s0)


def _rwkv_out_kernel(x_ref, m_ref, y_ref, bonus_ref, g_ref, lng_ref, lnb_ref, wo_ref, grp_ref, grp_t_ref, o_ref,
                     *, gn_eps):
    y = y_ref[0] + y_ref[1]
    inv_n = 1.0 / RWKV_HEAD
    mean = _dot_sel(_dot_sel(y, grp_ref[...]) * inv_n, grp_t_ref[...])
    yc = y - mean
    rstd = lax.rsqrt(_dot_sel(yc * yc, grp_ref[...]) * inv_n + gn_eps)
    yn = yc * _dot_sel(rstd, grp_t_ref[...]) * lng_ref[...] + lnb_ref[...]
    out = _dot((yn + bonus_ref[...]) * g_ref[...], wo_ref[...])
    o_ref[...] = x_ref[...] + m_ref[5:6, :] * out


def _rwkv_out(x, mods, y, bonus, g, p):
    bsz, length, d = x.shape
    tm = _tile(length, 256)
    tok = pl.BlockSpec((None, tm, d), lambda b, i: (b, i, 0))
    consts = [p["ln_g"], p["ln_b"], p["w_o"], p["grp"], p["grp_t"]]
    return pl.pallas_call(
        functools.partial(_rwkv_out_kernel, gn_eps=RWKV_HEAD * 1e-5),
        grid=(bsz, length // tm),
        in_specs=([tok, _mod_spec(mods), pl.BlockSpec((None, 2, tm, d), lambda b, i: (b, 0, i, 0)), tok, tok]
                  + [_const_spec(c.shape) for c in consts]),
        out_specs=tok,
        out_shape=jax.ShapeDtypeStruct(x.shape, F32),
        compiler_params=_params("parallel", "parallel"),
        name="rwkv_out",
    )(x, mods, y, bonus, g, *consts)


def _group_selectors(width, group):
    sel = (jnp.arange(width)[:, None] // group == jnp.arange(LANES)[None, :]).astype(BF16)
    return sel, sel.T


def _head_slots(w, per_head, offset=0):
    k = w.shape[0]
    w = w.reshape(k, MLA_HEADS, per_head)
    w = jnp.pad(w, ((0, 0), (0, 0), (offset, LANES - per_head - offset)))
    return w.reshape(k, MLA_HEADS * LANES)


def _mla_params(w_in, q_norm, w_qb, kv_norm, w_kvb, qn_q, qn_k, dw_w, dw_b, cn_g, cn_b, w_out):
    d = w_in.shape[0]
    o_kr = MLA_Q_RANK + MLA_KV_RANK
    mla_in = o_kr + MLA_ROPE
    w_in_p = jnp.concatenate([w_in[:, :mla_in], jnp.zeros((d, LANES - MLA_ROPE), F32), w_in[:, mla_in:]], axis=1)
    kvb = w_kvb.reshape(MLA_KV_RANK, MLA_HEADS, MLA_NOPE + MLA_V)
    w_k = _head_slots(kvb[:, :, :MLA_NOPE].reshape(MLA_KV_RANK, -1), MLA_NOPE)
    w_v = kvb[:, :, MLA_NOPE:].reshape(MLA_KV_RANK, MLA_HEADS * MLA_V)
    place = jnp.pad(jnp.eye(MLA_ROPE, dtype=F32), ((0, LANES - MLA_ROPE), (MLA_NOPE, LANES - QK_DIM)))
    w_kr = jnp.tile(place, (1, MLA_HEADS))
    pad_gain = lambda g: jnp.tile(jnp.pad(g, (0, LANES - QK_DIM)), MLA_HEADS)[None, :]
    grp, grp_t = _group_selectors(MLA_HEADS * LANES, LANES)
    mix = MLA_HEADS * MLA_V
    return dict(
        w_in=w_in_p.astype(BF16), q_norm=q_norm[None, :], w_q=_head_slots(w_qb, QK_DIM).astype(BF16),
        kv_norm=kv_norm[None, :], w_k=w_k.astype(BF16), w_kr=w_kr.astype(BF16), w_v=w_v.astype(BF16),
        gain_q=pad_gain(qn_q) * ATTN_SCALE, gain_k=pad_gain(qn_k), grp=grp, grp_t=grp_t,
        dw_w=dw_w[:, 0, :], dw_b=dw_b[None, :], cn_g=cn_g[None, :], cn_b=cn_b[None, :],
        w_out_att=w_out[:mix].astype(BF16), w_out_conv=w_out[mix:].astype(BF16))


def _rope_tables(length, with_rope):
    n_freq = MLA_ROPE // 4
    if not with_rope:
        return jnp.ones((length, LANES), F32), jnp.zeros((length, LANES), F32), jnp.zeros((length, LANES), F32)
    t = jnp.arange(length)
    inv = ROPE_THETA ** (-jnp.arange(n_freq, dtype=F32) / n_freq)
    ang_r = (t // GRID_W).astype(F32)[:, None] * inv
    ang_c = (t % GRID_W).astype(F32)[:, None] * inv
    zeros = jnp.zeros((length, n_freq), F32)

    def slot(nope_val, r_first, r_second, c_first, c_second):
        nope = jnp.full((length, MLA_NOPE), nope_val, F32)
        pad = jnp.full((length, LANES - QK_DIM), nope_val, F32)
        return jnp.concatenate([nope, r_first, r_second, c_first, c_second, pad], axis=1)

    cos_r, sin_r, cos_c, sin_c = jnp.cos(ang_r), jnp.sin(ang_r), jnp.cos(ang_c), jnp.sin(ang_c)
    return (slot(1.0, cos_r, cos_r, cos_c, cos_c),
            slot(0.0, -sin_r, zeros, -sin_c, zeros),
            slot(0.0, zeros, sin_r, zeros, sin_c))


def _rwkv_params(x_mix, w_r, w_k, w_v, w0, w1, w2, a0, a1, a2, g1, g2, k_k, k_a, r_k, ln_g, ln_b, w_o):
    d = w_r.shape[0]
    grp, grp_t = _group_selectors(d, RWKV_HEAD)
    bf = lambda w: w.astype(BF16)
    return dict(x_mix=x_mix, w_r=bf(w_r), w_k=bf(w_k), w_v=bf(w_v), w0=w0, w1=bf(w1), w2=bf(w2), a0=a0,
                a1=bf(a1), a2=bf(a2), g1=bf(g1), g2=bf(g2), k_k=k_k[None, :], k_a=k_a[None, :],
                r_k=r_k.reshape(1, d), ln_g=ln_g[None, :], ln_b=ln_b[None, :], w_o=bf(w_o), grp=grp, grp_t=grp_t)


def _mla_conv_layer(x, ctx, m_l, m_c, p):
    length, lc = x.shape[1], ctx.shape[1]
    q_l, k_l, v_l, hg_l = _mla_in(x, m_l, p, _rope_tables(length, True))
    q_c, k_c, v_c, hg_c = _mla_in(ctx, m_c, p, _rope_tables(lc, False))
    att_l = _attention(q_l, jnp.concatenate([k_c, k_l], axis=1), jnp.concatenate([v_c, v_l], axis=1))
    att_c = _attention(q_c, k_c, v_c)
    x = _mix_out(x, m_l, att_l, _conformer_conv(hg_l, p), p)
    ctx = _mix_out(ctx, m_c, att_c, _conformer_conv(hg_c, p), p)
    return x, ctx


def _rwkv_layer(x, ctx, m_l, m_c, p):
    bsz, _, d = x.shape
    r_c, v_c, kk_c, _, _, lw_c, kd_c, b_c = _rwkv_in(ctx, m_c, p)
    r_l, v_l, kk_l, g_l, bonus_l, lw_l, kd_l, b_l = _rwkv_in(x, m_l, p)
    s0 = jnp.zeros((bsz, 2, d // LANES, LANES, LANES), F32)
    _, s_ctx = _wkv_scan(r_c, v_c, kk_c, lw_c, kd_c, b_c, s0)
    y_l, _ = _wkv_scan(r_l, v_l, kk_l, lw_l, kd_l, b_l, s_ctx)
    return _rwkv_out(x, m_l, y_l, bonus_l, g_l, p)


def kernel(x, c, ctx, c_ctx, ada_w, ada_b, ffn_w1, ffn_w3, ffn_w2, mla_w_in, mla_q_norm, mla_w_qb, mla_kv_norm, mla_w_kvb, mla_qk_norm_q, mla_qk_norm_k, conv_dw_w, conv_dw_b, conv_norm_g, conv_norm_b, mix_w_out, rwkv_x_mix, rwkv_w_r, rwkv_w_k, rwkv_w_v, rwkv_w0, rwkv_w1, rwkv_w2, rwkv_a0, rwkv_a1, rwkv_a2, rwkv_g1, rwkv_g2, rwkv_k_k, rwkv_k_a, rwkv_r_k, rwkv_ln_g, rwkv_ln_b, rwkv_w_o):
    bsz = x.shape[0]
    depth = ada_w.shape[0]
    assert depth == 2, "layer 0 is the MLA/conv layer, layer 1 the (last) RWKV layer"
    cond = jnp.concatenate([c, c_ctx[None, :], jnp.zeros((16 - bsz - 1, c.shape[1]), F32)], axis=0)
    w1, w3, w2 = ffn_w1.astype(BF16), ffn_w3.astype(BF16), ffn_w2.astype(BF16)
    mla_p = _mla_params(mla_w_in[0], mla_q_norm[0], mla_w_qb[0], mla_kv_norm[0], mla_w_kvb[0], mla_qk_norm_q[0],
                        mla_qk_norm_k[0], conv_dw_w[0], conv_dw_b[0], conv_norm_g[0], conv_norm_b[0], mix_w_out[0])
    rwkv_p = _rwkv_params(rwkv_x_mix[0], rwkv_w_r[0], rwkv_w_k[0], rwkv_w_v[0], rwkv_w0[0], rwkv_w1[0],
                          rwkv_w2[0], rwkv_a0[0], rwkv_a1[0], rwkv_a2[0], rwkv_g1[0], rwkv_g2[0], rwkv_k_k[0],
                          rwkv_k_a[0], rwkv_r_k[0], rwkv_ln_g[0], rwkv_ln_b[0], rwkv_w_o[0])
    for i in range(depth):
        mods = _ada_mods(cond, ada_w[i], ada_b[i])
        m_l, m_c = mods[:bsz], mods[bsz:bsz + 1]
        x = _half_ffn(x, m_l, 0, w1[i, 0], w3[i, 0], w2[i, 0])
        ctx = _half_ffn(ctx, m_c, 0, w1[i, 0], w3[i, 0], w2[i, 0])
        if i == 0:
            x, ctx = _mla_conv_layer(x, ctx, m_l, m_c, mla_p)
            ctx = _half_ffn(ctx, m_c, 6, w1[i, 1], w3[i, 1], w2[i, 1])
        else:
            x = _rwkv_layer(x, ctx, m_l, m_c, rwkv_p)
        x = _half_ffn(x, m_l, 6, w1[i, 1], w3[i, 1], w2[i, 1])
    return x
```

```python
import functools
import math

import jax
import jax.numpy as jnp
from jax import lax
from jax.experimental import pallas as pl
from jax.experimental.pallas import tpu as pltpu

F32 = jnp.float32
BF16 = jnp.bfloat16

LANES = 128
VMEM_LIMIT = 56 * 1024 * 1024

NORM_EPS = 1e-6
N_MOD = 9
GRID_W = 64
ROPE_THETA = 10000.0

MLA_HEADS = 8
MLA_NOPE = 64
MLA_ROPE = 32
MLA_V = 64
MLA_Q_RANK = 384
MLA_KV_RANK = 128
QK_DIM = MLA_NOPE + MLA_ROPE
ATTN_SCALE = QK_DIM ** -0.5
LOG2_E = math.log2(math.e)
ATTN_TQ = 256
ATTN_TK = 768
ATTN_HEADS = 2
CONV_CH = 512
CONV_K = 31
CONV_HALO = 16

RWKV_HEAD = 64
SCAN_CHUNK = 64
DECAY_SCALE = math.exp(-0.5)


def _dot(a, b):
    return jnp.dot(a.astype(BF16), b.astype(BF16), preferred_element_type=F32)


def _dot_nt(a, b):
    return lax.dot_general(a.astype(BF16), b.astype(BF16), (((1,), (1,)), ((), ())),
                           preferred_element_type=F32)


def _dot_tn(a, b):
    return lax.dot_general(a.astype(BF16), b.astype(BF16), (((0,), (0,)), ((), ())),
                           preferred_element_type=F32)


def _dot_sel(x, sel):
    hi = x.astype(BF16)
    lo = (x - hi.astype(F32)).astype(BF16)
    return (jnp.dot(hi, sel, preferred_element_type=F32)
            + jnp.dot(lo, sel, preferred_element_type=F32))


def _rms(x):
    return x * lax.rsqrt(jnp.mean(x * x, axis=-1, keepdims=True) + NORM_EPS)


def _sigmoid(x):
    return 1.0 / (1.0 + jnp.exp(-x))


def _modulate(x, m_ref, first):
    return _rms(x) * (1.0 + m_ref[first + 1:first + 2, :]) + m_ref[first:first + 1, :]


def _const_spec(shape):
    zeros = (0,) * len(shape)
    return pl.BlockSpec(shape, lambda *_: zeros, pipeline_mode=pl.Buffered(1))


def _params(*sem):
    return pltpu.CompilerParams(dimension_semantics=sem, vmem_limit_bytes=VMEM_LIMIT)


def _mod_spec(mods):
    d = mods.shape[-1]
    if mods.shape[0] == 1:
        return pl.BlockSpec((None, N_MOD, d), lambda b, i: (0, 0, 0))
    return pl.BlockSpec((None, N_MOD, d), lambda b, i: (b, 0, 0))


def _tile(length, pref):
    t = min(pref, length)
    assert length % t == 0, (length, t)
    return t


def _ada_kernel(c_ref, w_ref, b_ref, o_ref):
    c = c_ref[...]
    s = c * _sigmoid(c)
    w = w_ref[...]
    s_hi = s.astype(BF16)
    s_lo = (s - s_hi.astype(F32)).astype(BF16)
    w_hi = w.astype(BF16)
    w_lo = (w - w_hi.astype(F32)).astype(BF16)
    acc = jnp.dot(s_hi, w_hi, preferred_element_type=F32)
    acc += jnp.dot(s_hi, w_lo, preferred_element_type=F32)
    acc += jnp.dot(s_lo, w_hi, preferred_element_type=F32)
    o_ref[...] = acc + b_ref[...]


def _ada_mods(cond, w, b):
    r, d = cond.shape
    n = w.shape[1]
    out = pl.pallas_call(
        _ada_kernel,
        grid=(n // d,),
        in_specs=[pl.BlockSpec((r, d), lambda j: (0, 0)),
                  pl.BlockSpec((d, d), lambda j: (0, j)),
                  pl.BlockSpec((1, d), lambda j: (0, j))],
        out_specs=pl.BlockSpec((r, d), lambda j: (0, j)),
        out_shape=jax.ShapeDtypeStruct((r, n), F32),
        compiler_params=_params("parallel"),
        name="ada_mods",
    )(cond, w, b.reshape(1, n))
    return out.reshape(r, n // d, d)


def _ffn_kernel(x_ref, m_ref, w1_ref, w3_ref, w2_ref, o_ref, *, first, f_chunk):
    x = x_ref[...]
    h = _modulate(x, m_ref, first).astype(BF16)
    acc = None
    for c in range(w1_ref.shape[1] // f_chunk):
        cols = slice(c * f_chunk, (c + 1) * f_chunk)
        a = jnp.dot(h, w1_ref[:, cols], preferred_element_type=F32)
        b = jnp.dot(h, w3_ref[:, cols], preferred_element_type=F32)
        g = (a * _sigmoid(a) * b).astype(BF16)
        part = jnp.dot(g, w2_ref[cols, :], preferred_element_type=F32)
        acc = part if acc is None else acc + part
    o_ref[...] = x + (0.5 * m_ref[first + 2:first + 3, :]) * acc


def _half_ffn(x, mods, first, w1, w3, w2):
    bsz, length, d = x.shape
    f = w1.shape[1]
    tm = _tile(length, 512)
    f_chunk = f // 2 if (f // 2) % LANES == 0 else f
    return pl.pallas_call(
        functools.partial(_ffn_kernel, first=first, f_chunk=f_chunk),
        grid=(bsz, length // tm),
        in_specs=[pl.BlockSpec((None, tm, d), lambda b, i: (b, i, 0)),
                  _mod_spec(mods),
                  _const_spec((d, f)), _const_spec((d, f)), _const_spec((f, d))],
        out_specs=pl.BlockSpec((None, tm, d), lambda b, i: (b, i, 0)),
        out_shape=jax.ShapeDtypeStruct(x.shape, F32),
        compiler_params=_params("parallel", "parallel"),
        name="half_ffn",
    )(x, mods, w1, w3, w2)


def _head_rms(t, width, grp_ref, grp_t_ref):
    ss = _dot_sel(t * t, grp_ref[...])
    r = lax.rsqrt(ss * (1.0 / width) + NORM_EPS)
    return t * _dot_sel(r, grp_t_ref[...])


def _rope(t, cos, sin_lo, sin_hi):
    n = t.shape[1]
    rep = n // cos.shape[1]
    cos, sin_lo, sin_hi = (jnp.tile(u, (1, rep)) for u in (cos, sin_lo, sin_hi))
    return t * cos + pltpu.roll(t, n - 8, 1) * sin_lo + pltpu.roll(t, 8, 1) * sin_hi


def _mla_in_kernel(x_ref, m_ref, w_in_ref, qn_ref, wq_ref, kvn_ref, wk_ref, wkr_ref, wv_ref,
                   gq_ref, gk_ref, vone_ref, cos_ref, sl_ref, sh_ref, grp_ref, grp_t_ref,
                   q_ref, k_ref, v_ref, hg_ref):
    h = _modulate(x_ref[...], m_ref, 3)
    z = _dot(h, w_in_ref[...])
    o_kv = MLA_Q_RANK
    o_kr = o_kv + MLA_KV_RANK
    o_a = o_kr + LANES
    o_g = o_a + CONV_CH
    c_q = _rms(z[:, :o_kv]) * qn_ref[...]
    c_kv = _rms(z[:, o_kv:o_kr]) * kvn_ref[...]
    hg_ref[...] = z[:, o_a:o_g] * _sigmoid(z[:, o_g:o_g + CONV_CH])
    cos, s_lo, s_hi = cos_ref[...], sl_ref[...], sh_ref[...]
    q = _head_rms(_dot(c_q, wq_ref[...]), QK_DIM, grp_ref, grp_t_ref) * gq_ref[...]
    q_ref[...] = _rope(q, cos, s_lo, s_hi).astype(BF16)
    k = _dot(c_kv, wk_ref[...]) + _dot(z[:, o_kr:o_a], wkr_ref[...])
    k = _head_rms(k, QK_DIM, grp_ref, grp_t_ref) * gk_ref[...]
    k_ref[...] = _rope(k, cos, s_lo, s_hi).astype(BF16)
    v_ref[...] = (_dot(c_kv, wv_ref[...]) + vone_ref[...]).astype(BF16)


def _mla_in(x, mods, p, rope_tabs):
    bsz, length, d = x.shape
    tm = _tile(length, 512)
    hs = MLA_HEADS * LANES
    cos, s_lo, s_hi = rope_tabs
    tab_spec = pl.BlockSpec((tm, LANES), lambda b, i: (i, 0))
    tok = lambda w: pl.BlockSpec((None, tm, w), lambda b, i: (b, i, 0))
    consts = [p["w_in"], p["q_norm"], p["w_q"], p["kv_norm"], p["w_k"], p["w_kr"], p["w_v"],
              p["gain_q"], p["gain_k"], p["v_one"]]
    return pl.pallas_call(
        _mla_in_kernel,
        grid=(bsz, length // tm),
        in_specs=([tok(d), _mod_spec(mods)] + [_const_spec(c.shape) for c in consts]
                  + [tab_spec, tab_spec, tab_spec, _const_spec(p["grp"].shape), _const_spec(p["grp_t"].shape)]),
        out_specs=[tok(hs), tok(hs), tok(hs), tok(CONV_CH)],
        out_shape=[jax.ShapeDtypeStruct((bsz, length, hs), BF16),
                   jax.ShapeDtypeStruct((bsz, length, hs), BF16),
                   jax.ShapeDtypeStruct((bsz, length, hs), BF16),
                   jax.ShapeDtypeStruct((bsz, length, CONV_CH), F32)],
        compiler_params=_params("parallel", "parallel"),
        name="mla_in",
    )(x, mods, *consts, cos, s_lo, s_hi, p["grp"], p["grp_t"])


def _attn_kernel(q_ref, k_ref, v_ref, o_ref, *, tk):
    tq = q_ref.shape[0]
    n_chunks = k_ref.shape[0] // tk
    heads = [slice(j * LANES, (j + 1) * LANES) for j in range(q_ref.shape[1] // LANES)]
    qs = [q_ref[:, cols] for cols in heads]

    def rows(c):
        return pl.ds(pl.multiple_of(c * tk, tk), tk)

    def scores(c):
        return tuple(_dot_nt(q, k_ref[rows(c), cols]) for q, cols in zip(qs, heads))

    def consume(c, s_all, carry):
        out = []
        for s, cols, (m, acc) in zip(s_all, heads, carry):
            m_new = jnp.maximum(m, jnp.max(s, axis=-1, keepdims=True))
            p = jnp.exp2(s - m_new).astype(BF16)
            acc = jnp.exp2(m - m_new) * acc + jnp.dot(p, v_ref[rows(c), cols], preferred_element_type=F32)
            out.append((m_new, acc))
        return tuple(out)

    def body(c, carry):
        s_all, state = carry
        return scores(c + 1), consume(c, s_all, state)

    init = tuple((jnp.full((tq, 1), -jnp.inf, F32), jnp.zeros((tq, LANES), F32)) for _ in heads)
    s_last, state = lax.fori_loop(0, n_chunks - 1, body, (scores(0), init))
    for cols, (_, acc) in zip(heads, consume(n_chunks - 1, s_last, state)):
        o_ref[:, cols] = (acc * (1.0 / acc[:, MLA_V:MLA_V + 1])).astype(BF16)


def _attention(q, k, v):
    bsz, lq, hs = q.shape
    lk = k.shape[1]
    tq = _tile(lq, ATTN_TQ)
    tk = next(t for t in (ATTN_TK, 512, 256, lk) if lk % t == 0)
    blk = lambda rows, imap: pl.BlockSpec((None, rows, ATTN_HEADS * LANES), imap)
    return pl.pallas_call(
        functools.partial(_attn_kernel, tk=tk),
        grid=(bsz, hs // (ATTN_HEADS * LANES), lq // tq),
        in_specs=[blk(tq, lambda b, h, i: (b, i, h)), blk(lk, lambda b, h, i: (b, 0, h)),
                  blk(lk, lambda b, h, i: (b, 0, h))],
        out_specs=blk(tq, lambda b, h, i: (b, i, h)),
        out_shape=jax.ShapeDtypeStruct((bsz, lq, hs), BF16),
        compiler_params=_params("parallel", "parallel", "parallel"),
        name="attention",
    )(q, k, v)


def _conv_kernel(h_ref, hp_ref, hn_ref, w_ref, b_ref, g_ref, be_ref, o_ref, ext_ref, *, rows):
    i = pl.program_id(1)
    tm = h_ref.shape[0]
    ext_ref[0:CONV_HALO, :] = jnp.where(i > 0, hp_ref[...], 0.0)
    ext_ref[CONV_HALO:CONV_HALO + tm, :] = h_ref[...]
    ext_ref[CONV_HALO + tm:, :] = jnp.where(i < pl.num_programs(1) - 1, hn_ref[...], 0.0)
    first = CONV_HALO - CONV_K // 2
    for r0 in range(0, tm, rows):
        acc = jnp.zeros((rows, CONV_CH), F32)
        for j in range(CONV_K):
            acc += ext_ref[first + r0 + j:first + r0 + j + rows, :] * w_ref[j:j + 1, :]
        acc += b_ref[...]
        xc = acc - jnp.mean(acc, axis=-1, keepdims=True)
        y = xc * lax.rsqrt(jnp.mean(xc * xc, axis=-1, keepdims=True) + NORM_EPS)
        y = y * g_ref[...] + be_ref[...]
        o_ref[r0:r0 + rows, :] = (y * _sigmoid(y)).astype(BF16)


def _conformer_conv(hg, p):
    bsz, length, ch = hg.shape
    tm = _tile(length, 256)
    nh = tm // CONV_HALO
    last = length // CONV_HALO - 1
    return pl.pallas_call(
        functools.partial(_conv_kernel, rows=32),
        grid=(bsz, length // tm),
        in_specs=[pl.BlockSpec((None, tm, ch), lambda b, i: (b, i, 0)),
                  pl.BlockSpec((None, CONV_HALO, ch), lambda b, i: (b, jnp.maximum(i * nh - 1, 0), 0)),
                  pl.BlockSpec((None, CONV_HALO, ch), lambda b, i: (b, jnp.minimum((i + 1) * nh, last), 0)),
                  _const_spec((CONV_K, ch)), _const_spec((1, ch)), _const_spec((1, ch)), _const_spec((1, ch))],
        out_specs=pl.BlockSpec((None, tm, ch), lambda b, i: (b, i, 0)),
        out_shape=jax.ShapeDtypeStruct((bsz, length, ch), BF16),
        scratch_shapes=[pltpu.VMEM((tm + 2 * CONV_HALO, ch), F32)],
        compiler_params=_params("parallel", "parallel"),
        name="conformer_conv",
    )(hg, hg, hg, p["dw_w"], p["dw_b"], p["cn_g"], p["cn_b"])


def _mix_out_kernel(x_ref, m_ref, a_ref, c_ref, wa_ref, wc_ref, o_ref):
    out = (jnp.dot(a_ref[...], wa_ref[...], preferred_element_type=F32)
           + jnp.dot(c_ref[...], wc_ref[...], preferred_element_type=F32))
    o_ref[...] = x_ref[...] + m_ref[5:6, :] * out


def _mix_out(x, mods, att, conv, p):
    bsz, length, d = x.shape
    tm = _tile(length, 512)
    tok = lambda w: pl.BlockSpec((None, tm, w), lambda b, i: (b, i, 0))
    return pl.pallas_call(
        _mix_out_kernel,
        grid=(bsz, length // tm),
        in_specs=[tok(d), _mod_spec(mods), tok(att.shape[-1]), tok(conv.shape[-1]),
                  _const_spec(p["w_out_att"].shape), _const_spec(p["w_out_conv"].shape)],
        out_specs=tok(d),
        out_shape=jax.ShapeDtypeStruct(x.shape, F32),
        compiler_params=_params("parallel", "parallel"),
        name="mix_out",
    )(x, mods, att, conv, p["w_out_att"], p["w_out_conv"])


def _rwkv_in_kernel(x_ref, xp_ref, xn_ref, m_ref, xmix_ref, wr_ref, wk_ref, wv_ref, w0_ref, w1_ref, w2_ref,
                    a0_ref, a1_ref, a2_ref, g1_ref, g2_ref, kk_ref, ka_ref, rk_ref, grp_ref, grp_t_ref,
                    r_out, v_out, kkn_out, g_out, bonus_out, lw_out, kd_out, b_out):
    i = pl.program_id(1)
    tm = x_ref.shape[0]
    h = _modulate(x_ref[...], m_ref, 3)
    h_before = jnp.where(i > 0, _modulate(xp_ref[7:8, :], m_ref, 3), 0.0)
    h_after = jnp.where(i < pl.num_programs(1) - 1, _modulate(xn_ref[0:1, :], m_ref, 3), 0.0)
    row = lax.broadcasted_iota(jnp.int32, (tm, 1), 0)
    h_prev = jnp.where(row == 0, h_before, pltpu.roll(h, 1, 0))
    h_next = jnp.where(row == tm - 1, h_after, pltpu.roll(h, tm - 1, 0))
    xx = 0.5 * (h_prev + h_next) - h
    xr, xw, xk, xv, xa, xg = (h + xx * xmix_ref[j:j + 1, :] for j in range(6))
    r = _dot(xr, wr_ref[...])
    k = _dot(xk, wk_ref[...])
    v = _dot(xv, wv_ref[...])
    kk = k * kk_ref[...]
    norm = jnp.maximum(jnp.sqrt(_dot_sel(kk * kk, grp_ref[...])), 1e-12)
    kk = kk * _dot_sel(1.0 / norm, grp_t_ref[...])
    g_out[...] = _dot(_sigmoid(_dot(xg, g1_ref[...])), g2_ref[...]).astype(BF16)
    k_sum = None
    for d in range(2):
        w_pre = w0_ref[d:d + 1, :] + _dot(jnp.tanh(_dot(xw, w1_ref[d])), w2_ref[d])
        lw_out[d] = -DECAY_SCALE * _sigmoid(w_pre)
        a = _sigmoid(a0_ref[d:d + 1, :] + _dot(_dot(xa, a1_ref[d]), a2_ref[d]))
        k_d = k * (1.0 + (a - 1.0) * ka_ref[...])
        kd_out[d] = k_d.astype(BF16)
        b_out[d] = (kk * a).astype(BF16)
        k_sum = k_d if k_sum is None else k_sum + k_d
    r_out[...] = r.astype(BF16)
    v_out[...] = v.astype(BF16)
    kkn_out[...] = kk.astype(BF16)
    head_dot = _dot_sel(r * (0.5 * k_sum) * rk_ref[...], grp_ref[...])
    bonus_out[...] = (_dot_sel(head_dot, grp_t_ref[...]) * v).astype(BF16)


def _rwkv_in(x, mods, p):
    bsz, length, d = x.shape
    tm = _tile(length, 256)
    nb = tm // 8
    last = length // 8 - 1
    tok = pl.BlockSpec((None, tm, d), lambda b, i: (b, i, 0))
    tok2 = pl.BlockSpec((None, 2, tm, d), lambda b, i: (b, 0, i, 0))
    consts = [p[n] for n in ("x_mix", "w_r", "w_k", "w_v", "w0", "w1", "w2", "a0", "a1", "a2", "g1", "g2",
                             "k_k", "k_a", "r_k", "grp", "grp_t")]
    one = jax.ShapeDtypeStruct((bsz, length, d), BF16)
    two = jax.ShapeDtypeStruct((bsz, 2, length, d), BF16)
    log_decay = jax.ShapeDtypeStruct((bsz, 2, length, d), F32)
    return pl.pallas_call(
        _rwkv_in_kernel,
        grid=(bsz, length // tm),
        in_specs=([tok,
                   pl.BlockSpec((None, 8, d), lambda b, i: (b, jnp.maximum(i * nb - 1, 0), 0)),
                   pl.BlockSpec((None, 8, d), lambda b, i: (b, jnp.minimum((i + 1) * nb, last), 0)),
                   _mod_spec(mods)] + [_const_spec(c.shape) for c in consts]),
        out_specs=[tok, tok, tok, tok, tok, tok2, tok2, tok2],
        out_shape=[one, one, one, one, one, log_decay, two, two],
        compiler_params=_params("parallel", "parallel"),
        name="rwkv_in",
    )(x, x, x, mods, *consts)


def _scan_kernel(r_ref, v_ref, kk_ref, lw_ref, kd_ref, b_ref, s0_ref, y_ref, st_ref, state_ref):
    d = pl.program_id(1)
    c = pl.program_id(2)
    cs = lw_ref.shape[0]
    n_pairs = state_ref.shape[0]

    @pl.when(c == 0)
    def _():
        state_ref[...] = s0_ref[...]

    rev = d == 1
    sign = 1 - 2 * d
    ti = lax.broadcasted_iota(jnp.int32, (cs, cs), 0)
    si = lax.broadcasted_iota(jnp.int32, (cs, cs), 1)
    incl = jnp.where((ti - si) * sign >= 0, 1.0, 0.0).astype(BF16)
    lw = lw_ref[...]
    l1 = lw.astype(BF16)
    rem = lw - l1.astype(F32)
    l2 = rem.astype(BF16)
    l3 = (rem - l2.astype(F32)).astype(BF16)
    cum = (jnp.dot(incl, l1, preferred_element_type=F32) + jnp.dot(incl, l2, preferred_element_type=F32)
           + jnp.dot(incl, l3, preferred_element_type=F32))
    total = jnp.where(rev, cum[0:1, :], cum[cs - 1:cs, :])
    p_inv = jnp.exp(-cum)
    to_end = jnp.exp(total - cum)
    kd = kd_ref[...]
    bv = b_ref[...]
    r_t = r_ref[...] * jnp.exp(cum)
    a_t = -kk_ref[...] * jnp.exp(cum - lw)
    k_t = kd * p_inv
    b_t = bv * p_inv
    k_e = kd * to_end
    b_e = bv * to_end
    v = v_ref[...]
    p_tot = jnp.exp(total)

    assert cs & (cs - 1) == 0
    t2 = lax.broadcasted_iota(jnp.int32, (2 * cs, 2 * cs), 0) & (cs - 1)
    s2 = lax.broadcasted_iota(jnp.int32, (2 * cs, 2 * cs), 1) & (cs - 1)
    strict2 = (t2 - s2) * sign > 0
    incl2 = (t2 - s2) * sign >= 0
    even = lax.broadcasted_iota(jnp.int32, (cs, LANES), 1) < RWKV_HEAD
    vi = lax.broadcasted_iota(jnp.int32, (LANES, LANES), 0) < RWKV_HEAD
    ki = lax.broadcasted_iota(jnp.int32, (LANES, LANES), 1) < RWKV_HEAD
    same_head = vi == ki

    def stack(t):
        return jnp.concatenate([jnp.where(even, t, 0.0), jnp.where(even, 0.0, t)], axis=0)

    def twice(t):
        return jnp.concatenate([t, t], axis=0)

    bf = lambda t: t.astype(BF16)
    mm = lambda a, b: jnp.dot(a, b, preferred_element_type=F32)
    pairs = range(n_pairs)
    sls = [slice(p * LANES, (p + 1) * LANES) for p in pairs]
    eye = jnp.where(lax.broadcasted_iota(jnp.int32, (2 * cs, 2 * cs), 0)
                    == lax.broadcasted_iota(jnp.int32, (2 * cs, 2 * cs), 1), 1.0, 0.0)

    cross = [_dot_nt(jnp.concatenate([stack(a_t[:, sl]), stack(r_t[:, sl])], axis=0),
                     jnp.concatenate([stack(b_t[:, sl]), stack(k_t[:, sl])], axis=0)) for sl in sls]
    m_ab = [jnp.where(strict2, x[:2 * cs, :2 * cs], 0.0) for x in cross]
    m_ak = [bf(jnp.where(strict2, x[:2 * cs, 2 * cs:], 0.0)) for x in cross]
    m_rb = [bf(jnp.where(incl2, x[2 * cs:, :2 * cs], 0.0)) for x in cross]
    m_rk = [bf(jnp.where(incl2, x[2 * cs:, 2 * cs:], 0.0)) for x in cross]
    v2 = [bf(twice(v[:, sl])) for sl in sls]
    x_v = [mm(m_ak[p], v2[p]) for p in pairs]
    y_v = [mm(m_rk[p], v2[p]) for p in pairs]
    upd_v = [_dot_tn(v2[p], stack(k_e[:, sls[p]])) for p in pairs]
    pw = [bf(m) for m in m_ab]
    t_inv = [eye + m for m in m_ab]
    for _ in range(max(1, (cs - 1).bit_length()) - 1):
        pw = [bf(mm(w, w)) for w in pw]
        t_inv = [t_inv[p] + mm(pw[p], bf(t_inv[p])) for p in pairs]
    t_inv = [bf(t) for t in t_inv]

    states = [state_ref[p] for p in pairs]
    ar = [_dot_nt(jnp.concatenate([a_t[:, sls[p]], r_t[:, sls[p]]], axis=0), states[p]) for p in pairs]
    u = [bf(mm(t_inv[p], bf(twice(ar[p][:cs]) + x_v[p]))) for p in pairs]
    for p in pairs:
        y2 = twice(ar[p][cs:]) + y_v[p] + mm(m_rb[p], u[p])
        y_ref[:, sls[p]] = jnp.where(even, y2[:cs], y2[cs:])
        upd = upd_v[p] + _dot_tn(u[p], stack(b_e[:, sls[p]]))
        state_ref[p] = states[p] * p_tot[:, sls[p]] + jnp.where(same_head, upd, 0.0)

    @pl.when(c == pl.num_programs(2) - 1)
    def _():
        st_ref[...] = state_ref[...]


def _wkv_scan(r, v, kk, lw, kd, bv, s0):
    bsz, length, d = r.shape
    cs = SCAN_CHUNK
    nc = length // cs
    n_pairs = d // LANES
    chunk = lambda d_, c: c + d_ * (nc - 1 - 2 * c)
    shared = pl.BlockSpec((None, cs, d), lambda b, d_, c: (b, chunk(d_, c), 0))
    per_dir = pl.BlockSpec((None, None, cs, d), lambda b, d_, c: (b, d_, chunk(d_, c), 0))
    st_spec = pl.BlockSpec((None, None, n_pairs, LANES, LANES), lambda b, d_, c: (b, d_, 0, 0, 0))
    return pl.pallas_call(
        _scan_kernel,
        grid=(bsz, 2, nc),
        in_specs=[shared, shared, shared, per_dir, per_dir, per_dir, st_spec],
        out_specs=[per_dir, st_spec],
        out_shape=[jax.ShapeDtypeStruct((bsz, 2, length, d), F32),
                   jax.ShapeDtypeStruct(s0.shape, F32)],
        scratch_shapes=[pltpu.VMEM((n_pairs, LANES, LANES), F32)],
        compiler_params=_params("parallel", "parallel", "arbitrary"),
        name="wkv_scan",
    )(r, v, kk, lw, kd, bv, s0)


def _rwkv_out_kernel(x_ref, m_ref, y_ref, bonus_ref, g_ref, lng_ref, lnb_ref, wo_ref, grp_ref, grp_t_ref, o_ref,
                     *, gn_eps):
    y = y_ref[0] + y_ref[1]
    inv_n = 1.0 / RWKV_HEAD
    mean = _dot_sel(_dot_sel(y, grp_ref[...]) * inv_n, grp_t_ref[...])
    yc = y - mean
    rstd = lax.rsqrt(_dot_sel(yc * yc, grp_ref[...]) * inv_n + gn_eps)
    yn = yc * _dot_sel(rstd, grp_t_ref[...]) * lng_ref[...] + lnb_ref[...]
    out = _dot((yn + bonus_ref[...]) * g_ref[...], wo_ref[...])
    o_ref[...] = x_ref[...] + m_ref[5:6, :] * out


def _rwkv_out(x, mods, y, bonus, g, p):
    bsz, length, d = x.shape
    tm = _tile(length, 256)
    tok = pl.BlockSpec((None, tm, d), lambda b, i: (b, i, 0))
    consts = [p["ln_g"], p["ln_b"], p["w_o"], p["grp"], p["grp_t"]]
    return pl.pallas_call(
        functools.partial(_rwkv_out_kernel, gn_eps=RWKV_HEAD * 1e-5),
        grid=(bsz, length // tm),
        in_specs=([tok, _mod_spec(mods), pl.BlockSpec((None, 2, tm, d), lambda b, i: (b, 0, i, 0)), tok, tok]
                  + [_const_spec(c.shape) for c in consts]),
        out_specs=tok,
        out_shape=jax.ShapeDtypeStruct(x.shape, F32),
        compiler_params=_params("parallel", "parallel"),
        name="rwkv_out",
    )(x, mods, y, bonus, g, *consts)


def _group_selectors(width, group):
    sel = (jnp.arange(width)[:, None] // group == jnp.arange(LANES)[None, :]).astype(BF16)
    return sel, sel.T


def _head_slots(w, per_head, offset=0):
    k = w.shape[0]
    w = w.reshape(k, MLA_HEADS, per_head)
    w = jnp.pad(w, ((0, 0), (0, 0), (offset, LANES - per_head - offset)))
    return w.reshape(k, MLA_HEADS * LANES)


def _mla_params(w_in, q_norm, w_qb, kv_norm, w_kvb, qn_q, qn_k, dw_w, dw_b, cn_g, cn_b, w_out):
    d = w_in.shape[0]
    o_kr = MLA_Q_RANK + MLA_KV_RANK
    mla_in = o_kr + MLA_ROPE
    w_in_p = jnp.concatenate([w_in[:, :mla_in], jnp.zeros((d, LANES - MLA_ROPE), F32), w_in[:, mla_in:]], axis=1)
    kvb = w_kvb.reshape(MLA_KV_RANK, MLA_HEADS, MLA_NOPE + MLA_V)
    w_k = _head_slots(kvb[:, :, :MLA_NOPE].reshape(MLA_KV_RANK, -1), MLA_NOPE)
    w_v = _head_slots(kvb[:, :, MLA_NOPE:].reshape(MLA_KV_RANK, -1), MLA_V)
    v_one = jnp.tile(jnp.arange(LANES) == MLA_V, MLA_HEADS).astype(F32)[None, :]
    w_out_att = jnp.pad(w_out[:MLA_HEADS * MLA_V].reshape(MLA_HEADS, MLA_V, d), ((0, 0), (0, LANES - MLA_V), (0, 0)))
    place = jnp.pad(jnp.eye(MLA_ROPE, dtype=F32), ((0, LANES - MLA_ROPE), (MLA_NOPE, LANES - QK_DIM)))
    w_kr = jnp.tile(place, (1, MLA_HEADS))
    pad_gain = lambda g: jnp.tile(jnp.pad(g, (0, LANES - QK_DIM)), MLA_HEADS)[None, :]
    grp, grp_t = _group_selectors(MLA_HEADS * LANES, LANES)
    mix = MLA_HEADS * MLA_V
    return dict(
        w_in=w_in_p.astype(BF16), q_norm=q_norm[None, :], w_q=_head_slots(w_qb, QK_DIM).astype(BF16),
        kv_norm=kv_norm[None, :], w_k=w_k.astype(BF16), w_kr=w_kr.astype(BF16), w_v=w_v.astype(BF16),
        gain_q=pad_gain(qn_q) * (ATTN_SCALE * LOG2_E), gain_k=pad_gain(qn_k), v_one=v_one, grp=grp, grp_t=grp_t,
        dw_w=dw_w[:, 0, :], dw_b=dw_b[None, :], cn_g=cn_g[None, :], cn_b=cn_b[None, :],
        w_out_att=w_out_att.reshape(MLA_HEADS * LANES, d).astype(BF16), w_out_conv=w_out[mix:].astype(BF16))


def _rope_tables(length, with_rope):
    n_freq = MLA_ROPE // 4
    if not with_rope:
        return jnp.ones((length, LANES), F32), jnp.zeros((length, LANES), F32), jnp.zeros((length, LANES), F32)
    t = jnp.arange(length)
    inv = ROPE_THETA ** (-jnp.arange(n_freq, dtype=F32) / n_freq)
    ang_r = (t // GRID_W).astype(F32)[:, None] * inv
    ang_c = (t % GRID_W).astype(F32)[:, None] * inv
    zeros = jnp.zeros((length, n_freq), F32)

    def slot(nope_val, r_first, r_second, c_first, c_second):
        nope = jnp.full((length, MLA_NOPE), nope_val, F32)
        pad = jnp.full((length, LANES - QK_DIM), nope_val, F32)
        return jnp.concatenate([nope, r_first, r_second, c_first, c_second, pad], axis=1)

    cos_r, sin_r, cos_c, sin_c = jnp.cos(ang_r), jnp.sin(ang_r), jnp.cos(ang_c), jnp.sin(ang_c)
    return (slot(1.0, cos_r, cos_r, cos_c, cos_c),
            slot(0.0, -sin_r, zeros, -sin_c, zeros),
            slot(0.0, zeros, sin_r, zeros, sin_c))


def _rwkv_params(x_mix, w_r, w_k, w_v, w0, w1, w2, a0, a1, a2, g1, g2, k_k, k_a, r_k, ln_g, ln_b, w_o):
    d = w_r.shape[0]
    grp, grp_t = _group_selectors(d, RWKV_HEAD)
    bf = lambda w: w.astype(BF16)
    return dict(x_mix=x_mix, w_r=bf(w_r), w_k=bf(w_k), w_v=bf(w_v), w0=w0, w1=bf(w1), w2=bf(w2), a0=a0,
                a1=bf(a1), a2=bf(a2), g1=bf(g1), g2=bf(g2), k_k=k_k[None, :], k_a=k_a[None, :],
                r_k=r_k.reshape(1, d), ln_g=ln_g[None, :], ln_b=ln_b[None, :], w_o=bf(w_o), grp=grp, grp_t=grp_t)


def _mla_conv_layer(x, ctx, m_l, m_c, p):
    length, lc = x.shape[1], ctx.shape[1]
    q_l, k_l, v_l, hg_l = _mla_in(x, m_l, p, _rope_tables(length, True))
    q_c, k_c, v_c, hg_c = _mla_in(ctx, m_c, p, _rope_tables(lc, False))
    att_l = _attention(q_l, jnp.concatenate([k_c, k_l], axis=1), jnp.concatenate([v_c, v_l], axis=1))
    att_c = _attention(q_c, k_c, v_c)
    x = _mix_out(x, m_l, att_l, _conformer_conv(hg_l, p), p)
    ctx = _mix_out(ctx, m_c, att_c, _conformer_conv(hg_c, p), p)
    return x, ctx


def _rwkv_layer(x, ctx, m_l, m_c, p):
    bsz, _, d = x.shape
    r_c, v_c, kk_c, _, _, lw_c, kd_c, b_c = _rwkv_in(ctx, m_c, p)
    r_l, v_l, kk_l, g_l, bonus_l, lw_l, kd_l, b_l = _rwkv_in(x, m_l, p)
    s0 = jnp.zeros((bsz, 2, d // LANES, LANES, LANES), F32)
    _, s_ctx = _wkv_scan(r_c, v_c, kk_c, lw_c, kd_c, b_c, s0)
    y_l, _ = _wkv_scan(r_l, v_l, kk_l, lw_l, kd_l, b_l, s_ctx)
    return _rwkv_out(x, m_l, y_l, bonus_l, g_l, p)


def kernel(x, c, ctx, c_ctx, ada_w, ada_b, ffn_w1, ffn_w3, ffn_w2, mla_w_in, mla_q_norm, mla_w_qb, mla_kv_norm, mla_w_kvb, mla_qk_norm_q, mla_qk_norm_k, conv_dw_w, conv_dw_b, conv_norm_g, conv_norm_b, mix_w_out, rwkv_x_mix, rwkv_w_r, rwkv_w_k, rwkv_w_v, rwkv_w0, rwkv_w1, rwkv_w2, rwkv_a0, rwkv_a1, rwkv_a2, rwkv_g1, rwkv_g2, rwkv_k_k, rwkv_k_a, rwkv_r_k, rwkv_ln_g, rwkv_ln_b, rwkv_w_o):
    bsz = x.shape[0]
    depth = ada_w.shape[0]
    assert depth == 2, "layer 0 is the MLA/conv layer, layer 1 the (last) RWKV layer"
    cond = jnp.concatenate([c, c_ctx[None, :], jnp.zeros((16 - bsz - 1, c.shape[1]), F32)], axis=0)
    w1, w3, w2 = ffn_w1.astype(BF16), ffn_w3.astype(BF16), ffn_w2.astype(BF16)
    mla_p = _mla_params(mla_w_in[0], mla_q_norm[0], mla_w_qb[0], mla_kv_norm[0], mla_w_kvb[0], mla_qk_norm_q[0],
                        mla_qk_norm_k[0], conv_dw_w[0], conv_dw_b[0], conv_norm_g[0], conv_norm_b[0], mix_w_out[0])
    rwkv_p = _rwkv_params(rwkv_x_mix[0], rwkv_w_r[0], rwkv_w_k[0], rwkv_w_v[0], rwkv_w0[0], rwkv_w1[0],
                          rwkv_w2[0], rwkv_a0[0], rwkv_a1[0], rwkv_a2[0], rwkv_g1[0], rwkv_g2[0], rwkv_k_k[0],
                          rwkv_k_a[0], rwkv_r_k[0], rwkv_ln_g[0], rwkv_ln_b[0], rwkv_w_o[0])
    for i in range(depth):
        mods = _ada_mods(cond, ada_w[i], ada_b[i])
        m_l, m_c = mods[:bsz], mods[bsz:bsz + 1]
        x = _half_ffn(x, m_l, 0, w1[i, 0], w3[i, 0], w2[i, 0])
        ctx = _half_ffn(ctx, m_c, 0, w1[i, 0], w3[i, 0], w2[i, 0])
        if i == 0:
            x, ctx = _mla_conv_layer(x, ctx, m_l, m_c, mla_p)
            ctx = _half_ffn(ctx, m_c, 6, w1[i, 1], w3[i, 1], w2[i, 1])
        else:
            x = _rwkv_layer(x, ctx, m_l, m_c, rwkv_p)
        x = _half_ffn(x, m_l, 6, w1[i, 1], w3[i, 1], w2[i, 1])
    return x
```

```python
import functools
import math

import jax
import jax.numpy as jnp
from jax import lax
from jax.experimental import pallas as pl
from jax.experimental.pallas import tpu as pltpu

F32 = jnp.float32
BF16 = jnp.bfloat16

LANES = 128
VMEM_LIMIT = 56 * 1024 * 1024

NORM_EPS = 1e-6
N_MOD = 9
GRID_W = 64
ROPE_THETA = 10000.0

MLA_HEADS = 8
MLA_NOPE = 64
MLA_ROPE = 32
MLA_V = 64
MLA_Q_RANK = 384
MLA_KV_RANK = 128
QK_DIM = MLA_NOPE + MLA_ROPE
ATTN_SCALE = QK_DIM ** -0.5
LOG2_E = math.log2(math.e)
ATTN_TQ = 512
ATTN_TK = 768
ATTN_HEADS = 2
SHIFT_LANE = QK_DIM
BOUND_SLACK = 1.02
ATTN_BOUND_LIMIT = 40.0
CONV_CH = 512
CONV_K = 31
CONV_HALO = 16

RWKV_HEAD = 64
SCAN_CHUNK = 64
DECAY_SCALE = math.exp(-0.5)


def _dot(a, b):
    return jnp.dot(a.astype(BF16), b.astype(BF16), preferred_element_type=F32)


def _dot_nt(a, b):
    return lax.dot_general(a.astype(BF16), b.astype(BF16), (((1,), (1,)), ((), ())),
                           preferred_element_type=F32)


def _dot_tn(a, b):
    return lax.dot_general(a.astype(BF16), b.astype(BF16), (((0,), (0,)), ((), ())),
                           preferred_element_type=F32)


def _dot_sel(x, sel):
    hi = x.astype(BF16)
    lo = (x - hi.astype(F32)).astype(BF16)
    return (jnp.dot(hi, sel, preferred_element_type=F32)
            + jnp.dot(lo, sel, preferred_element_type=F32))


def _rms(x):
    return x * lax.rsqrt(jnp.mean(x * x, axis=-1, keepdims=True) + NORM_EPS)


def _sigmoid(x):
    return 1.0 / (1.0 + jnp.exp(-x))


def _modulate(x, m_ref, first):
    return _rms(x) * (1.0 + m_ref[first + 1:first + 2, :]) + m_ref[first:first + 1, :]


def _const_spec(shape):
    zeros = (0,) * len(shape)
    return pl.BlockSpec(shape, lambda *_: zeros, pipeline_mode=pl.Buffered(1))


def _params(*sem):
    return pltpu.CompilerParams(dimension_semantics=sem, vmem_limit_bytes=VMEM_LIMIT)


def _mod_spec(mods):
    d = mods.shape[-1]
    if mods.shape[0] == 1:
        return pl.BlockSpec((None, N_MOD, d), lambda b, i: (0, 0, 0))
    return pl.BlockSpec((None, N_MOD, d), lambda b, i: (b, 0, 0))


def _tile(length, pref):
    t = min(pref, length)
    assert length % t == 0, (length, t)
    return t


def _ada_kernel(c_ref, w_ref, b_ref, o_ref):
    c = c_ref[...]
    s = c * _sigmoid(c)
    w = w_ref[...]
    s_hi = s.astype(BF16)
    s_lo = (s - s_hi.astype(F32)).astype(BF16)
    w_hi = w.astype(BF16)
    w_lo = (w - w_hi.astype(F32)).astype(BF16)
    acc = jnp.dot(s_hi, w_hi, preferred_element_type=F32)
    acc += jnp.dot(s_hi, w_lo, preferred_element_type=F32)
    acc += jnp.dot(s_lo, w_hi, preferred_element_type=F32)
    o_ref[...] = acc + b_ref[...]


def _ada_mods(cond, w, b):
    r, d = cond.shape
    n = w.shape[1]
    out = pl.pallas_call(
        _ada_kernel,
        grid=(n // d,),
        in_specs=[pl.BlockSpec((r, d), lambda j: (0, 0)),
                  pl.BlockSpec((d, d), lambda j: (0, j)),
                  pl.BlockSpec((1, d), lambda j: (0, j))],
        out_specs=pl.BlockSpec((r, d), lambda j: (0, j)),
        out_shape=jax.ShapeDtypeStruct((r, n), F32),
        compiler_params=_params("parallel"),
        name="ada_mods",
    )(cond, w, b.reshape(1, n))
    return out.reshape(r, n // d, d)


def _ffn_kernel(x_ref, m_ref, w1_ref, w3_ref, w2_ref, o_ref, *, first, f_chunk):
    x = x_ref[...]
    h = _modulate(x, m_ref, first).astype(BF16)
    acc = None
    for c in range(w1_ref.shape[1] // f_chunk):
        cols = slice(c * f_chunk, (c + 1) * f_chunk)
        a = jnp.dot(h, w1_ref[:, cols], preferred_element_type=F32)
        b = jnp.dot(h, w3_ref[:, cols], preferred_element_type=F32)
        g = (a * _sigmoid(a) * b).astype(BF16)
        part = jnp.dot(g, w2_ref[cols, :], preferred_element_type=F32)
        acc = part if acc is None else acc + part
    o_ref[...] = x + (0.5 * m_ref[first + 2:first + 3, :]) * acc


def _half_ffn(x, mods, first, w1, w3, w2):
    bsz, length, d = x.shape
    f = w1.shape[1]
    tm = _tile(length, 512)
    f_chunk = f // 2 if (f // 2) % LANES == 0 else f
    return pl.pallas_call(
        functools.partial(_ffn_kernel, first=first, f_chunk=f_chunk),
        grid=(bsz, length // tm),
        in_specs=[pl.BlockSpec((None, tm, d), lambda b, i: (b, i, 0)),
                  _mod_spec(mods),
                  _const_spec((d, f)), _const_spec((d, f)), _const_spec((f, d))],
        out_specs=pl.BlockSpec((None, tm, d), lambda b, i: (b, i, 0)),
        out_shape=jax.ShapeDtypeStruct(x.shape, F32),
        compiler_params=_params("parallel", "parallel"),
        name="half_ffn",
    )(x, mods, w1, w3, w2)


def _head_rms(t, width, grp_ref, grp_t_ref):
    ss = _dot(t * t, grp_ref[...])
    r = lax.rsqrt(ss * (1.0 / width) + NORM_EPS)
    return t * _dot_sel(r, grp_t_ref[...])


def _rope(t, cos, sin_lo, sin_hi):
    n = t.shape[1]
    rep = n // cos.shape[1]
    cos, sin_lo, sin_hi = (jnp.tile(u, (1, rep)) for u in (cos, sin_lo, sin_hi))
    return t * cos + pltpu.roll(t, n - 8, 1) * sin_lo + pltpu.roll(t, 8, 1) * sin_hi


def _slot_sq_norm(t, grp_ref):
    t = t.astype(F32)
    return _dot(t * t, grp_ref[...])


def _mla_in_kernel(x_ref, m_ref, w_in_ref, qn_ref, wq_ref, kvn_ref, wk_ref, wkr_ref, wv_ref,
                   gq_ref, gk_ref, vone_ref, kone_ref, cos_ref, sl_ref, sh_ref, grp_ref, grp_t_ref,
                   q_ref, k_ref, v_ref, hg_ref, qsq_ref, ksq_ref):
    h = _modulate(x_ref[...], m_ref, 3)
    z = _dot(h, w_in_ref[...])
    o_kv = MLA_Q_RANK
    o_kr = o_kv + MLA_KV_RANK
    o_a = o_kr + LANES
    o_g = o_a + CONV_CH
    c_q = _rms(z[:, :o_kv]) * qn_ref[...]
    c_kv = _rms(z[:, o_kv:o_kr]) * kvn_ref[...]
    hg_ref[...] = z[:, o_a:o_g] * _sigmoid(z[:, o_g:o_g + CONV_CH])
    cos, s_lo, s_hi = cos_ref[...], sl_ref[...], sh_ref[...]
    q = _head_rms(_dot(c_q, wq_ref[...]), QK_DIM, grp_ref, grp_t_ref) * gq_ref[...]
    q = _rope(q, cos, s_lo, s_hi).astype(BF16)
    q_ref[...] = q
    qsq_ref[...] = _slot_sq_norm(q, grp_ref)
    k = _dot(c_kv, wk_ref[...]) + _dot(z[:, o_kr:o_a], wkr_ref[...])
    k = _head_rms(k, QK_DIM, grp_ref, grp_t_ref) * gk_ref[...]
    k = _rope(k, cos, s_lo, s_hi).astype(BF16)
    ksq_ref[...] = _slot_sq_norm(k, grp_ref)
    k_ref[...] = k + kone_ref[...]
    v_ref[...] = (_dot(c_kv, wv_ref[...]) + vone_ref[...]).astype(BF16)


def _mla_in(x, mods, p, rope_tabs):
    bsz, length, d = x.shape
    tm = _tile(length, 512)
    hs = MLA_HEADS * LANES
    cos, s_lo, s_hi = rope_tabs
    tab_spec = pl.BlockSpec((tm, LANES), lambda b, i: (i, 0))
    tok = lambda w: pl.BlockSpec((None, tm, w), lambda b, i: (b, i, 0))
    consts = [p["w_in"], p["q_norm"], p["w_q"], p["kv_norm"], p["w_k"], p["w_kr"], p["w_v"],
              p["gain_q"], p["gain_k"], p["v_one"], p["k_one"]]
    return pl.pallas_call(
        _mla_in_kernel,
        grid=(bsz, length // tm),
        in_specs=([tok(d), _mod_spec(mods)] + [_const_spec(c.shape) for c in consts]
                  + [tab_spec, tab_spec, tab_spec, _const_spec(p["grp"].shape), _const_spec(p["grp_t"].shape)]),
        out_specs=[tok(hs), tok(hs), tok(hs), tok(CONV_CH), tok(LANES), tok(LANES)],
        out_shape=[jax.ShapeDtypeStruct((bsz, length, hs), BF16),
                   jax.ShapeDtypeStruct((bsz, length, hs), BF16),
                   jax.ShapeDtypeStruct((bsz, length, hs), BF16),
                   jax.ShapeDtypeStruct((bsz, length, CONV_CH), F32),
                   jax.ShapeDtypeStruct((bsz, length, LANES), F32),
                   jax.ShapeDtypeStruct((bsz, length, LANES), F32)],
        compiler_params=_params("parallel", "parallel"),
        name="mla_in",
    )(x, mods, *consts, cos, s_lo, s_hi, p["grp"], p["grp_t"])


def _attn_kernel(q_ref, k_ref, v_ref, o_ref, *, tk):
    tq = q_ref.shape[0]
    n_chunks = k_ref.shape[0] // tk
    heads = [slice(j * LANES, (j + 1) * LANES) for j in range(q_ref.shape[1] // LANES)]
    qs = [q_ref[:, cols] for cols in heads]

    def rows(c):
        return pl.ds(pl.multiple_of(c * tk, tk), tk)

    def scores(c):
        return tuple(_dot_nt(q, k_ref[rows(c), cols]) for q, cols in zip(qs, heads))

    def consume(c, s_all, carry):
        out = []
        for s, cols, (m, acc) in zip(s_all, heads, carry):
            m_new = jnp.maximum(m, jnp.max(s, axis=-1, keepdims=True))
            p = jnp.exp2(s - m_new).astype(BF16)
            acc = jnp.exp2(m - m_new) * acc + jnp.dot(p, v_ref[rows(c), cols], preferred_element_type=F32)
            out.append((m_new, acc))
        return tuple(out)

    def body(c, carry):
        s_all, state = carry
        return scores(c + 1), consume(c, s_all, state)

    init = tuple((jnp.full((tq, 1), -jnp.inf, F32), jnp.zeros((tq, LANES), F32)) for _ in heads)
    s_last, state = lax.fori_loop(0, n_chunks - 1, body, (scores(0), init))
    for cols, (_, acc) in zip(heads, consume(n_chunks - 1, s_last, state)):
        o_ref[:, cols] = (acc * (1.0 / acc[:, MLA_V:MLA_V + 1])).astype(BF16)


def _attn_bounded_kernel(q_ref, k_ref, v_ref, kmax_ref, o_ref, *, tk):
    tq = q_ref.shape[0]
    n_chunks = k_ref.shape[0] // tk
    heads = [slice(j * LANES, (j + 1) * LANES) for j in range(q_ref.shape[1] // LANES)]
    lane = lax.broadcasted_iota(jnp.int32, (tq, LANES), 1)
    qs = []
    for cols in heads:
        q = q_ref[:, cols].astype(F32)
        bound = jnp.sqrt(jnp.sum(q * q, axis=-1, keepdims=True)) * kmax_ref[:, cols]
        qs.append(jnp.where(lane == SHIFT_LANE, -bound, q).astype(BF16))

    def body(c, accs):
        rows = pl.ds(pl.multiple_of(c * tk, tk), tk)
        out = []
        for q, cols, acc in zip(qs, heads, accs):
            p = jnp.exp2(_dot_nt(q, k_ref[rows, cols])).astype(BF16)
            out.append(acc + jnp.dot(p, v_ref[rows, cols], preferred_element_type=F32))
        return tuple(out)

    accs = lax.fori_loop(0, n_chunks, body, tuple(jnp.zeros((tq, LANES), F32) for _ in heads))
    for cols, acc in zip(heads, accs):
        o_ref[:, cols] = (acc * (1.0 / acc[:, MLA_V:MLA_V + 1])).astype(BF16)


def _col_max_kernel(x_ref, o_ref):
    o_ref[...] = jnp.max(x_ref[...], axis=0, keepdims=True)


def _col_max(x):
    bsz, length, w = x.shape
    return pl.pallas_call(
        _col_max_kernel,
        grid=(bsz,),
        in_specs=[pl.BlockSpec((None, length, w), lambda b: (b, 0, 0))],
        out_specs=pl.BlockSpec((None, 1, w), lambda b: (b, 0, 0)),
        out_shape=jax.ShapeDtypeStruct((bsz, 1, w), F32),
        compiler_params=_params("parallel"),
        name="col_max",
    )(x)


def _attention(q, k, v, sq_norms=None):
    bsz, lq, hs = q.shape
    lk = k.shape[1]
    tq = _tile(lq, ATTN_TQ)
    tk = next(t for t in (ATTN_TK, 512, 256, lk) if lk % t == 0)
    group = ATTN_HEADS * LANES
    blk = lambda rows, imap: pl.BlockSpec((None, rows, group), imap)
    specs = [blk(tq, lambda b, h, i: (b, i, h)), blk(lk, lambda b, h, i: (b, 0, h)), blk(lk, lambda b, h, i: (b, 0, h))]
    common = dict(grid=(bsz, hs // group, lq // tq), out_specs=blk(tq, lambda b, h, i: (b, i, h)),
                  out_shape=jax.ShapeDtypeStruct((bsz, lq, hs), BF16),
                  compiler_params=_params("parallel", "parallel", "parallel"))

    def online(q, k, v, _):
        return pl.pallas_call(functools.partial(_attn_kernel, tk=tk), in_specs=specs, name="attention", **common)(q, k, v)

    if sq_norms is None:
        return online(q, k, v, None)

    def bounded(q, k, v, kmax):
        return pl.pallas_call(functools.partial(_attn_bounded_kernel, tk=tk),
                              in_specs=specs + [blk(1, lambda b, h, i: (b, 0, h))],
                              name="attention_bounded", **common)(q, k, v, kmax)

    n_heads = hs // LANES
    q_max = jnp.sqrt(_col_max(sq_norms[0]))[..., :n_heads]
    k_max = jnp.sqrt(_col_max(sq_norms[1]))[..., :n_heads] * BOUND_SLACK
    k_max_slots = jnp.repeat(k_max, LANES, axis=-1)
    return lax.cond(jnp.max(q_max * k_max) <= ATTN_BOUND_LIMIT, bounded, online, q, k, v, k_max_slots)


def _conv_kernel(h_ref, hp_ref, hn_ref, w_ref, b_ref, g_ref, be_ref, o_ref, ext_ref, *, rows):
    i = pl.program_id(1)
    tm = h_ref.shape[0]
    ext_ref[0:CONV_HALO, :] = jnp.where(i > 0, hp_ref[...], 0.0)
    ext_ref[CONV_HALO:CONV_HALO + tm, :] = h_ref[...]
    ext_ref[CONV_HALO + tm:, :] = jnp.where(i < pl.num_programs(1) - 1, hn_ref[...], 0.0)
    first = CONV_HALO - CONV_K // 2
    for r0 in range(0, tm, rows):
        acc = jnp.zeros((rows, CONV_CH), F32)
        for j in range(CONV_K):
            acc += ext_ref[first + r0 + j:first + r0 + j + rows, :] * w_ref[j:j + 1, :]
        acc += b_ref[...]
        xc = acc - jnp.mean(acc, axis=-1, keepdims=True)
        y = xc * lax.rsqrt(jnp.mean(xc * xc, axis=-1, keepdims=True) + NORM_EPS)
        y = y * g_ref[...] + be_ref[...]
        o_ref[r0:r0 + rows, :] = (y * _sigmoid(y)).astype(BF16)


def _conformer_conv(hg, p):
    bsz, length, ch = hg.shape
    tm = _tile(length, 256)
    nh = tm // CONV_HALO
    last = length // CONV_HALO - 1
    return pl.pallas_call(
        functools.partial(_conv_kernel, rows=32),
        grid=(bsz, length // tm),
        in_specs=[pl.BlockSpec((None, tm, ch), lambda b, i: (b, i, 0)),
                  pl.BlockSpec((None, CONV_HALO, ch), lambda b, i: (b, jnp.maximum(i * nh - 1, 0), 0)),
                  pl.BlockSpec((None, CONV_HALO, ch), lambda b, i: (b, jnp.minimum((i + 1) * nh, last), 0)),
                  _const_spec((CONV_K, ch)), _const_spec((1, ch)), _const_spec((1, ch)), _const_spec((1, ch))],
        out_specs=pl.BlockSpec((None, tm, ch), lambda b, i: (b, i, 0)),
        out_shape=jax.ShapeDtypeStruct((bsz, length, ch), BF16),
        scratch_shapes=[pltpu.VMEM((tm + 2 * CONV_HALO, ch), F32)],
        compiler_params=_params("parallel", "parallel"),
        name="conformer_conv",
    )(hg, hg, hg, p["dw_w"], p["dw_b"], p["cn_g"], p["cn_b"])


def _mix_out_kernel(x_ref, m_ref, a_ref, c_ref, wa_ref, wc_ref, o_ref):
    out = (jnp.dot(a_ref[...], wa_ref[...], preferred_element_type=F32)
           + jnp.dot(c_ref[...], wc_ref[...], preferred_element_type=F32))
    o_ref[...] = x_ref[...] + m_ref[5:6, :] * out


def _mix_out(x, mods, att, conv, p):
    bsz, length, d = x.shape
    tm = _tile(length, 512)
    tok = lambda w: pl.BlockSpec((None, tm, w), lambda b, i: (b, i, 0))
    return pl.pallas_call(
        _mix_out_kernel,
        grid=(bsz, length // tm),
        in_specs=[tok(d), _mod_spec(mods), tok(att.shape[-1]), tok(conv.shape[-1]),
                  _const_spec(p["w_out_att"].shape), _const_spec(p["w_out_conv"].shape)],
        out_specs=tok(d),
        out_shape=jax.ShapeDtypeStruct(x.shape, F32),
        compiler_params=_params("parallel", "parallel"),
        name="mix_out",
    )(x, mods, att, conv, p["w_out_att"], p["w_out_conv"])


def _rwkv_in_kernel(x_ref, xp_ref, xn_ref, m_ref, xmix_ref, wr_ref, wk_ref, wv_ref, w0_ref, w1_ref, w2_ref,
                    a0_ref, a1_ref, a2_ref, g1_ref, g2_ref, kk_ref, ka_ref, rk_ref, grp_ref, grp_t_ref,
                    r_out, v_out, kkn_out, g_out, bonus_out, lw_out, kd_out, b_out):
    i = pl.program_id(1)
    tm = x_ref.shape[0]
    h = _modulate(x_ref[...], m_ref, 3)
    h_before = jnp.where(i > 0, _modulate(xp_ref[7:8, :], m_ref, 3), 0.0)
    h_after = jnp.where(i < pl.num_programs(1) - 1, _modulate(xn_ref[0:1, :], m_ref, 3), 0.0)
    row = lax.broadcasted_iota(jnp.int32, (tm, 1), 0)
    h_prev = jnp.where(row == 0, h_before, pltpu.roll(h, 1, 0))
    h_next = jnp.where(row == tm - 1, h_after, pltpu.roll(h, tm - 1, 0))
    xx = 0.5 * (h_prev + h_next) - h
    xr, xw, xk, xv, xa, xg = (h + xx * xmix_ref[j:j + 1, :] for j in range(6))
    r = _dot(xr, wr_ref[...])
    k = _dot(xk, wk_ref[...])
    v = _dot(xv, wv_ref[...])
    kk = k * kk_ref[...]
    norm = jnp.maximum(jnp.sqrt(_dot(kk * kk, grp_ref[...])), 1e-12)
    kk = kk * _dot_sel(1.0 / norm, grp_t_ref[...])
    g_out[...] = _dot(_sigmoid(_dot(xg, g1_ref[...])), g2_ref[...]).astype(BF16)
    k_sum = None
    for d in range(2):
        w_pre = w0_ref[d:d + 1, :] + _dot(jnp.tanh(_dot(xw, w1_ref[d])), w2_ref[d])
        lw_out[d] = -DECAY_SCALE * _sigmoid(w_pre)
        a = _sigmoid(a0_ref[d:d + 1, :] + _dot(_dot(xa, a1_ref[d]), a2_ref[d]))
        k_d = k * (1.0 + (a - 1.0) * ka_ref[...])
        kd_out[d] = k_d.astype(BF16)
        b_out[d] = (kk * a).astype(BF16)
        k_sum = k_d if k_sum is None else k_sum + k_d
    r_out[...] = r.astype(BF16)
    v_out[...] = v.astype(BF16)
    kkn_out[...] = kk.astype(BF16)
    head_dot = _dot(r * (0.5 * k_sum) * rk_ref[...], grp_ref[...])
    bonus_out[...] = (_dot(head_dot, grp_t_ref[...]) * v).astype(BF16)


def _rwkv_in(x, mods, p):
    bsz, length, d = x.shape
    tm = _tile(length, 256)
    nb = tm // 8
    last = length // 8 - 1
    tok = pl.BlockSpec((None, tm, d), lambda b, i: (b, i, 0))
    tok2 = pl.BlockSpec((None, 2, tm, d), lambda b, i: (b, 0, i, 0))
    consts = [p[n] for n in ("x_mix", "w_r", "w_k", "w_v", "w0", "w1", "w2", "a0", "a1", "a2", "g1", "g2",
                             "k_k", "k_a", "r_k", "grp", "grp_t")]
    one = jax.ShapeDtypeStruct((bsz, length, d), BF16)
    two = jax.ShapeDtypeStruct((bsz, 2, length, d), BF16)
    log_decay = jax.ShapeDtypeStruct((bsz, 2, length, d), F32)
    return pl.pallas_call(
        _rwkv_in_kernel,
        grid=(bsz, length // tm),
        in_specs=([tok,
                   pl.BlockSpec((None, 8, d), lambda b, i: (b, jnp.maximum(i * nb - 1, 0), 0)),
                   pl.BlockSpec((None, 8, d), lambda b, i: (b, jnp.minimum((i + 1) * nb, last), 0)),
                   _mod_spec(mods)] + [_const_spec(c.shape) for c in consts]),
        out_specs=[tok, tok, tok, tok, tok, tok2, tok2, tok2],
        out_shape=[one, one, one, one, one, log_decay, two, two],
        compiler_params=_params("parallel", "parallel"),
        name="rwkv_in",
    )(x, x, x, mods, *consts)


def _scan_kernel(r_ref, v_ref, kk_ref, lw_ref, kd_ref, b_ref, s0_ref, y_ref, st_ref, state_ref):
    d = pl.program_id(1)
    c = pl.program_id(2)
    cs = lw_ref.shape[0]
    n_pairs = state_ref.shape[0]

    @pl.when(c == 0)
    def _():
        state_ref[...] = s0_ref[...]

    rev = d == 1
    sign = 1 - 2 * d
    ti = lax.broadcasted_iota(jnp.int32, (cs, cs), 0)
    si = lax.broadcasted_iota(jnp.int32, (cs, cs), 1)
    incl = jnp.where((ti - si) * sign >= 0, 1.0, 0.0).astype(BF16)
    lw = lw_ref[...]
    l1 = lw.astype(BF16)
    rem = lw - l1.astype(F32)
    l2 = rem.astype(BF16)
    l3 = (rem - l2.astype(F32)).astype(BF16)
    cum = (jnp.dot(incl, l1, preferred_element_type=F32) + jnp.dot(incl, l2, preferred_element_type=F32)
           + jnp.dot(incl, l3, preferred_element_type=F32))
    total = jnp.where(rev, cum[0:1, :], cum[cs - 1:cs, :])
    p_inv = jnp.exp(-cum)
    to_end = jnp.exp(total - cum)
    kd = kd_ref[...]
    bv = b_ref[...]
    r_t = r_ref[...] * jnp.exp(cum)
    a_t = -kk_ref[...] * jnp.exp(cum - lw)
    k_t = kd * p_inv
    b_t = bv * p_inv
    k_e = kd * to_end
    b_e = bv * to_end
    v = v_ref[...]
    p_tot = jnp.exp(total)

    assert cs & (cs - 1) == 0
    t2 = lax.broadcasted_iota(jnp.int32, (2 * cs, 2 * cs), 0) & (cs - 1)
    s2 = lax.broadcasted_iota(jnp.int32, (2 * cs, 2 * cs), 1) & (cs - 1)
    strict2 = (t2 - s2) * sign > 0
    incl2 = (t2 - s2) * sign >= 0
    even = lax.broadcasted_iota(jnp.int32, (cs, LANES), 1) < RWKV_HEAD
    vi = lax.broadcasted_iota(jnp.int32, (LANES, LANES), 0) < RWKV_HEAD
    ki = lax.broadcasted_iota(jnp.int32, (LANES, LANES), 1) < RWKV_HEAD
    same_head = vi == ki

    def stack(t):
        return jnp.concatenate([jnp.where(even, t, 0.0), jnp.where(even, 0.0, t)], axis=0)

    def twice(t):
        return jnp.concatenate([t, t], axis=0)

    bf = lambda t: t.astype(BF16)
    mm = lambda a, b: jnp.dot(a, b, preferred_element_type=F32)
    pairs = range(n_pairs)
    sls = [slice(p * LANES, (p + 1) * LANES) for p in pairs]
    eye = jnp.where(lax.broadcasted_iota(jnp.int32, (2 * cs, 2 * cs), 0)
                    == lax.broadcasted_iota(jnp.int32, (2 * cs, 2 * cs), 1), 1.0, 0.0)

    cross = [_dot_nt(jnp.concatenate([stack(a_t[:, sl]), stack(r_t[:, sl])], axis=0),
                     jnp.concatenate([stack(b_t[:, sl]), stack(k_t[:, sl])], axis=0)) for sl in sls]
    m_ab = [jnp.where(strict2, x[:2 * cs, :2 * cs], 0.0) for x in cross]
    m_ak = [bf(jnp.where(strict2, x[:2 * cs, 2 * cs:], 0.0)) for x in cross]
    m_rb = [bf(jnp.where(incl2, x[2 * cs:, :2 * cs], 0.0)) for x in cross]
    m_rk = [bf(jnp.where(incl2, x[2 * cs:, 2 * cs:], 0.0)) for x in cross]
    v2 = [bf(twice(v[:, sl])) for sl in sls]
    x_v = [mm(m_ak[p], v2[p]) for p in pairs]
    y_v = [mm(m_rk[p], v2[p]) for p in pairs]
    upd_v = [_dot_tn(v2[p], stack(k_e[:, sls[p]])) for p in pairs]
    pw = [bf(m) for m in m_ab]
    t_inv = [eye + m for m in m_ab]
    for _ in range(max(1, (cs - 1).bit_length()) - 1):
        pw = [bf(mm(w, w)) for w in pw]
        t_inv = [t_inv[p] + mm(pw[p], bf(t_inv[p])) for p in pairs]
    t_inv = [bf(t) for t in t_inv]

    states = [state_ref[p] for p in pairs]
    ar = [_dot_nt(jnp.concatenate([a_t[:, sls[p]], r_t[:, sls[p]]], axis=0), states[p]) for p in pairs]
    u = [bf(mm(t_inv[p], bf(twice(ar[p][:cs]) + x_v[p]))) for p in pairs]
    for p in pairs:
        y2 = twice(ar[p][cs:]) + y_v[p] + mm(m_rb[p], u[p])
        y_ref[:, sls[p]] = jnp.where(even, y2[:cs], y2[cs:])
        upd = upd_v[p] + _dot_tn(u[p], stack(b_e[:, sls[p]]))
        state_ref[p] = states[p] * p_tot[:, sls[p]] + jnp.where(same_head, upd, 0.0)

    @pl.when(c == pl.num_programs(2) - 1)
    def _():
        st_ref[...] = state_ref[...]


def _wkv_scan(r, v, kk, lw, kd, bv, s0):
    bsz, length, d = r.shape
    cs = SCAN_CHUNK
    nc = length // cs
    n_pairs = d // LANES
    chunk = lambda d_, c: c + d_ * (nc - 1 - 2 * c)
    shared = pl.BlockSpec((None, cs, d), lambda b, d_, c: (b, chunk(d_, c), 0))
    per_dir = pl.BlockSpec((None, None, cs, d), lambda b, d_, c: (b, d_, chunk(d_, c), 0))
    st_spec = pl.BlockSpec((None, None, n_pairs, LANES, LANES), lambda b, d_, c: (b, d_, 0, 0, 0))
    return pl.pallas_call(
        _scan_kernel,
        grid=(bsz, 2, nc),
        in_specs=[shared, shared, shared, per_dir, per_dir, per_dir, st_spec],
        out_specs=[per_dir, st_spec],
        out_shape=[jax.ShapeDtypeStruct((bsz, 2, length, d), F32),
                   jax.ShapeDtypeStruct(s0.shape, F32)],
        scratch_shapes=[pltpu.VMEM((n_pairs, LANES, LANES), F32)],
        compiler_params=_params("parallel", "parallel", "arbitrary"),
        name="wkv_scan",
    )(r, v, kk, lw, kd, bv, s0)


def _rwkv_out_kernel(x_ref, m_ref, y_ref, bonus_ref, g_ref, lng_ref, lnb_ref, wo_ref, grp_ref, grp_t_ref, o_ref,
                     *, gn_eps):
    y = y_ref[0] + y_ref[1]
    inv_n = 1.0 / RWKV_HEAD
    mean = _dot_sel(_dot(y, grp_ref[...]) * inv_n, grp_t_ref[...])
    yc = y - mean
    rstd = lax.rsqrt(_dot(yc * yc, grp_ref[...]) * inv_n + gn_eps)
    yn = yc * _dot_sel(rstd, grp_t_ref[...]) * lng_ref[...] + lnb_ref[...]
    out = _dot((yn + bonus_ref[...]) * g_ref[...], wo_ref[...])
    o_ref[...] = x_ref[...] + m_ref[5:6, :] * out


def _rwkv_out(x, mods, y, bonus, g, p):
    bsz, length, d = x.shape
    tm = _tile(length, 256)
    tok = pl.BlockSpec((None, tm, d), lambda b, i: (b, i, 0))
    consts = [p["ln_g"], p["ln_b"], p["w_o"], p["grp"], p["grp_t"]]
    return pl.pallas_call(
        functools.partial(_rwkv_out_kernel, gn_eps=RWKV_HEAD * 1e-5),
        grid=(bsz, length // tm),
        in_specs=([tok, _mod_spec(mods), pl.BlockSpec((None, 2, tm, d), lambda b, i: (b, 0, i, 0)), tok, tok]
                  + [_const_spec(c.shape) for c in consts]),
        out_specs=tok,
        out_shape=jax.ShapeDtypeStruct(x.shape, F32),
        compiler_params=_params("parallel", "parallel"),
        name="rwkv_out",
    )(x, mods, y, bonus, g, *consts)


def _group_selectors(width, group):
    sel = (jnp.arange(width)[:, None] // group == jnp.arange(LANES)[None, :]).astype(BF16)
    return sel, sel.T


def _head_slots(w, per_head, offset=0):
    k = w.shape[0]
    w = w.reshape(k, MLA_HEADS, per_head)
    w = jnp.pad(w, ((0, 0), (0, 0), (offset, LANES - per_head - offset)))
    return w.reshape(k, MLA_HEADS * LANES)


def _mla_params(w_in, q_norm, w_qb, kv_norm, w_kvb, qn_q, qn_k, dw_w, dw_b, cn_g, cn_b, w_out):
    d = w_in.shape[0]
    o_kr = MLA_Q_RANK + MLA_KV_RANK
    mla_in = o_kr + MLA_ROPE
    w_in_p = jnp.concatenate([w_in[:, :mla_in], jnp.zeros((d, LANES - MLA_ROPE), F32), w_in[:, mla_in:]], axis=1)
    kvb = w_kvb.reshape(MLA_KV_RANK, MLA_HEADS, MLA_NOPE + MLA_V)
    w_k = _head_slots(kvb[:, :, :MLA_NOPE].reshape(MLA_KV_RANK, -1), MLA_NOPE)
    w_v = _head_slots(kvb[:, :, MLA_NOPE:].reshape(MLA_KV_RANK, -1), MLA_V)
    v_one = jnp.tile(jnp.arange(LANES) == MLA_V, MLA_HEADS).astype(F32)[None, :]
    k_one = jnp.tile(jnp.arange(LANES) == SHIFT_LANE, MLA_HEADS).astype(BF16)[None, :]
    w_out_att = jnp.pad(w_out[:MLA_HEADS * MLA_V].reshape(MLA_HEADS, MLA_V, d), ((0, 0), (0, LANES - MLA_V), (0, 0)))
    place = jnp.pad(jnp.eye(MLA_ROPE, dtype=F32), ((0, LANES - MLA_ROPE), (MLA_NOPE, LANES - QK_DIM)))
    w_kr = jnp.tile(place, (1, MLA_HEADS))
    pad_gain = lambda g: jnp.tile(jnp.pad(g, (0, LANES - QK_DIM)), MLA_HEADS)[None, :]
    grp, grp_t = _group_selectors(MLA_HEADS * LANES, LANES)
    mix = MLA_HEADS * MLA_V
    return dict(
        w_in=w_in_p.astype(BF16), q_norm=q_norm[None, :], w_q=_head_slots(w_qb, QK_DIM).astype(BF16),
        kv_norm=kv_norm[None, :], w_k=w_k.astype(BF16), w_kr=w_kr.astype(BF16), w_v=w_v.astype(BF16),
        gain_q=pad_gain(qn_q) * (ATTN_SCALE * LOG2_E), gain_k=pad_gain(qn_k), v_one=v_one, k_one=k_one, grp=grp, grp_t=grp_t,
        dw_w=dw_w[:, 0, :], dw_b=dw_b[None, :], cn_g=cn_g[None, :], cn_b=cn_b[None, :],
        w_out_att=w_out_att.reshape(MLA_HEADS * LANES, d).astype(BF16), w_out_conv=w_out[mix:].astype(BF16))


def _rope_tables(length, with_rope):
    n_freq = MLA_ROPE // 4
    if not with_rope:
        return jnp.ones((length, LANES), F32), jnp.zeros((length, LANES), F32), jnp.zeros((length, LANES), F32)
    t = jnp.arange(length)
    inv = ROPE_THETA ** (-jnp.arange(n_freq, dtype=F32) / n_freq)
    ang_r = (t // GRID_W).astype(F32)[:, None] * inv
    ang_c = (t % GRID_W).astype(F32)[:, None] * inv
    zeros = jnp.zeros((length, n_freq), F32)

    def slot(nope_val, r_first, r_second, c_first, c_second):
        nope = jnp.full((length, MLA_NOPE), nope_val, F32)
        pad = jnp.full((length, LANES - QK_DIM), nope_val, F32)
        return jnp.concatenate([nope, r_first, r_second, c_first, c_second, pad], axis=1)

    cos_r, sin_r, cos_c, sin_c = jnp.cos(ang_r), jnp.sin(ang_r), jnp.cos(ang_c), jnp.sin(ang_c)
    return (slot(1.0, cos_r, cos_r, cos_c, cos_c),
            slot(0.0, -sin_r, zeros, -sin_c, zeros),
            slot(0.0, zeros, sin_r, zeros, sin_c))


def _rwkv_params(x_mix, w_r, w_k, w_v, w0, w1, w2, a0, a1, a2, g1, g2, k_k, k_a, r_k, ln_g, ln_b, w_o):
    d = w_r.shape[0]
    grp, grp_t = _group_selectors(d, RWKV_HEAD)
    bf = lambda w: w.astype(BF16)
    return dict(x_mix=x_mix, w_r=bf(w_r), w_k=bf(w_k), w_v=bf(w_v), w0=w0, w1=bf(w1), w2=bf(w2), a0=a0,
                a1=bf(a1), a2=bf(a2), g1=bf(g1), g2=bf(g2), k_k=k_k[None, :], k_a=k_a[None, :],
                r_k=r_k.reshape(1, d), ln_g=ln_g[None, :], ln_b=ln_b[None, :], w_o=bf(w_o), grp=grp, grp_t=grp_t)


def _mla_conv_layer(x, ctx, m_l, m_c, p):
    length, lc = x.shape[1], ctx.shape[1]
    q_l, k_l, v_l, hg_l, qsq_l, ksq_l = _mla_in(x, m_l, p, _rope_tables(length, True))
    q_c, k_c, v_c, hg_c, _, ksq_c = _mla_in(ctx, m_c, p, _rope_tables(lc, False))
    cat = lambda a, b: jnp.concatenate([a, b], axis=1)
    att_l = _attention(q_l, cat(k_c, k_l), cat(v_c, v_l), (qsq_l, cat(ksq_c, ksq_l)))
    att_c = _attention(q_c, k_c, v_c)
    x = _mix_out(x, m_l, att_l, _conformer_conv(hg_l, p), p)
    ctx = _mix_out(ctx, m_c, att_c, _conformer_conv(hg_c, p), p)
    return x, ctx


def _rwkv_layer(x, ctx, m_l, m_c, p):
    bsz, _, d = x.shape
    r_c, v_c, kk_c, _, _, lw_c, kd_c, b_c = _rwkv_in(ctx, m_c, p)
    r_l, v_l, kk_l, g_l, bonus_l, lw_l, kd_l, b_l = _rwkv_in(x, m_l, p)
    s0 = jnp.zeros((bsz, 2, d // LANES, LANES, LANES), F32)
    _, s_ctx = _wkv_scan(r_c, v_c, kk_c, lw_c, kd_c, b_c, s0)
    y_l, _ = _wkv_scan(r_l, v_l, kk_l, lw_l, kd_l, b_l, s_ctx)
    return _rwkv_out(x, m_l, y_l, bonus_l, g_l, p)


def kernel(x, c, ctx, c_ctx, ada_w, ada_b, ffn_w1, ffn_w3, ffn_w2, mla_w_in, mla_q_norm, mla_w_qb, mla_kv_norm, mla_w_kvb, mla_qk_norm_q, mla_qk_norm_k, conv_dw_w, conv_dw_b, conv_norm_g, conv_norm_b, mix_w_out, rwkv_x_mix, rwkv_w_r, rwkv_w_k, rwkv_w_v, rwkv_w0, rwkv_w1, rwkv_w2, rwkv_a0, rwkv_a1, rwkv_a2, rwkv_g1, rwkv_g2, rwkv_k_k, rwkv_k_a, rwkv_r_k, rwkv_ln_g, rwkv_ln_b, rwkv_w_o):
    bsz = x.shape[0]
    depth = ada_w.shape[0]
    assert depth == 2, "layer 0 is the MLA/conv layer, layer 1 the (last) RWKV layer"
    cond = jnp.concatenate([c, c_ctx[None, :], jnp.zeros((16 - bsz - 1, c.shape[1]), F32)], axis=0)
    w1, w3, w2 = ffn_w1.astype(BF16), ffn_w3.astype(BF16), ffn_w2.astype(BF16)
    mla_p = _mla_params(mla_w_in[0], mla_q_norm[0], mla_w_qb[0], mla_kv_norm[0], mla_w_kvb[0], mla_qk_norm_q[0],
                        mla_qk_norm_k[0], conv_dw_w[0], conv_dw_b[0], conv_norm_g[0], conv_norm_b[0], mix_w_out[0])
    rwkv_p = _rwkv_params(rwkv_x_mix[0], rwkv_w_r[0], rwkv_w_k[0], rwkv_w_v[0], rwkv_w0[0], rwkv_w1[0],
                          rwkv_w2[0], rwkv_a0[0], rwkv_a1[0], rwkv_a2[0], rwkv_g1[0], rwkv_g2[0], rwkv_k_k[0],
                          rwkv_k_a[0], rwkv_r_k[0], rwkv_ln_g[0], rwkv_ln_b[0], rwkv_w_o[0])
    for i in range(depth):
        mods = _ada_mods(cond, ada_w[i], ada_b[i])
        m_l, m_c = mods[:bsz], mods[bsz:bsz + 1]
        x = _half_ffn(x, m_l, 0, w1[i, 0], w3[i, 0], w2[i, 0])
        ctx = _half_ffn(ctx, m_c, 0, w1[i, 0], w3[i, 0], w2[i, 0])
        if i == 0:
            x, ctx = _mla_conv_layer(x, ctx, m_l, m_c, mla_p)
            ctx = _half_ffn(ctx, m_c, 6, w1[i, 1], w3[i, 1], w2[i, 1])
        else:
            x = _rwkv_layer(x, ctx, m_l, m_c, rwkv_p)
        x = _half_ffn(x, m_l, 6, w1[i, 1], w3[i, 1], w2[i, 1])
    return x
```

```python
import functools
import math

import jax
import jax.numpy as jnp
from jax import lax
from jax.experimental import pallas as pl
from jax.experimental.pallas import tpu as pltpu

F32 = jnp.float32
BF16 = jnp.bfloat16

LANES = 128
VMEM_LIMIT = 56 * 1024 * 1024

NORM_EPS = 1e-6
N_MOD = 9
GRID_W = 64
ROPE_THETA = 10000.0

MLA_HEADS = 8
MLA_NOPE = 64
MLA_ROPE = 32
MLA_V = 64
MLA_Q_RANK = 384
MLA_KV_RANK = 128
QK_DIM = MLA_NOPE + MLA_ROPE
ATTN_SCALE = QK_DIM ** -0.5
LOG2_E = math.log2(math.e)
ATTN_TQ = 512
ATTN_TK = 1024
ATTN_HEADS = 4
SHIFT_LANE = QK_DIM
BOUND_SLACK = 1.02
ATTN_BOUND_LIMIT = 40.0
CONV_CH = 512
CONV_K = 31
CONV_HALO = 16

RWKV_HEAD = 64
SCAN_CHUNK = 64
SCAN_ROWS = 2
DECAY_SCALE = math.exp(-0.5)


def _dot(a, b):
    return jnp.dot(a.astype(BF16), b.astype(BF16), preferred_element_type=F32)


def _dot_nt(a, b):
    return lax.dot_general(a.astype(BF16), b.astype(BF16), (((1,), (1,)), ((), ())),
                           preferred_element_type=F32)


def _dot_tn(a, b):
    return lax.dot_general(a.astype(BF16), b.astype(BF16), (((0,), (0,)), ((), ())),
                           preferred_element_type=F32)


def _dot_sel(x, sel):
    hi = x.astype(BF16)
    lo = (x - hi.astype(F32)).astype(BF16)
    return (jnp.dot(hi, sel, preferred_element_type=F32)
            + jnp.dot(lo, sel, preferred_element_type=F32))


def _rms(x):
    return x * lax.rsqrt(jnp.mean(x * x, axis=-1, keepdims=True) + NORM_EPS)


def _sigmoid(x):
    return 1.0 / (1.0 + jnp.exp(-x))


def _modulate(x, m_ref, first):
    return _rms(x) * (1.0 + m_ref[first + 1:first + 2, :]) + m_ref[first:first + 1, :]


def _const_spec(shape):
    zeros = (0,) * len(shape)
    return pl.BlockSpec(shape, lambda *_: zeros, pipeline_mode=pl.Buffered(1))


def _params(*sem):
    return pltpu.CompilerParams(dimension_semantics=sem, vmem_limit_bytes=VMEM_LIMIT)


def _mod_spec(mods):
    d = mods.shape[-1]
    if mods.shape[0] == 1:
        return pl.BlockSpec((None, N_MOD, d), lambda b, i: (0, 0, 0))
    return pl.BlockSpec((None, N_MOD, d), lambda b, i: (b, 0, 0))


def _tile(length, pref):
    t = min(pref, length)
    assert length % t == 0, (length, t)
    return t


def _ada_kernel(c_ref, w_ref, b_ref, o_ref):
    c = c_ref[...]
    s = c * _sigmoid(c)
    w = w_ref[...]
    s_hi = s.astype(BF16)
    s_lo = (s - s_hi.astype(F32)).astype(BF16)
    w_hi = w.astype(BF16)
    w_lo = (w - w_hi.astype(F32)).astype(BF16)
    acc = jnp.dot(s_hi, w_hi, preferred_element_type=F32)
    acc += jnp.dot(s_hi, w_lo, preferred_element_type=F32)
    acc += jnp.dot(s_lo, w_hi, preferred_element_type=F32)
    o_ref[...] = acc + b_ref[...]


def _ada_mods(cond, w, b):
    r, d = cond.shape
    n = w.shape[1]
    out = pl.pallas_call(
        _ada_kernel,
        grid=(n // d,),
        in_specs=[pl.BlockSpec((r, d), lambda j: (0, 0)),
                  pl.BlockSpec((d, d), lambda j: (0, j)),
                  pl.BlockSpec((1, d), lambda j: (0, j))],
        out_specs=pl.BlockSpec((r, d), lambda j: (0, j)),
        out_shape=jax.ShapeDtypeStruct((r, n), F32),
        compiler_params=_params("parallel"),
        name="ada_mods",
    )(cond, w, b.reshape(1, n))
    return out.reshape(r, n // d, d)


def _ffn_kernel(x_ref, m_ref, w1_ref, w3_ref, w2_ref, o_ref, *, first, f_chunk):
    x = x_ref[...]
    h = _modulate(x, m_ref, first).astype(BF16)
    acc = None
    for c in range(w1_ref.shape[1] // f_chunk):
        cols = slice(c * f_chunk, (c + 1) * f_chunk)
        a = jnp.dot(h, w1_ref[:, cols], preferred_element_type=F32)
        b = jnp.dot(h, w3_ref[:, cols], preferred_element_type=F32)
        g = (a * _sigmoid(a) * b).astype(BF16)
        part = jnp.dot(g, w2_ref[cols, :], preferred_element_type=F32)
        acc = part if acc is None else acc + part
    o_ref[...] = x + (0.5 * m_ref[first + 2:first + 3, :]) * acc


def _half_ffn(x, mods, first, w1, w3, w2):
    bsz, length, d = x.shape
    f = w1.shape[1]
    tm = _tile(length, 512)
    f_chunk = f // 2 if (f // 2) % LANES == 0 else f
    return pl.pallas_call(
        functools.partial(_ffn_kernel, first=first, f_chunk=f_chunk),
        grid=(bsz, length // tm),
        in_specs=[pl.BlockSpec((None, tm, d), lambda b, i: (b, i, 0)),
                  _mod_spec(mods),
                  _const_spec((d, f)), _const_spec((d, f)), _const_spec((f, d))],
        out_specs=pl.BlockSpec((None, tm, d), lambda b, i: (b, i, 0)),
        out_shape=jax.ShapeDtypeStruct(x.shape, F32),
        compiler_params=_params("parallel", "parallel"),
        name="half_ffn",
    )(x, mods, w1, w3, w2)


def _head_rms(t, width, grp_ref, grp_t_ref):
    ss = _dot(t * t, grp_ref[...])
    r = lax.rsqrt(ss * (1.0 / width) + NORM_EPS)
    return t * _dot_sel(r, grp_t_ref[...])


def _rope(t, cos, sin_lo, sin_hi):
    n = t.shape[1]
    rep = n // cos.shape[1]
    cos, sin_lo, sin_hi = (jnp.tile(u, (1, rep)) for u in (cos, sin_lo, sin_hi))
    return t * cos + pltpu.roll(t, n - 8, 1) * sin_lo + pltpu.roll(t, 8, 1) * sin_hi


def _slot_sq_norm(t, grp_ref):
    t = t.astype(F32)
    return _dot(t * t, grp_ref[...])


def _mla_in_kernel(x_ref, m_ref, w_in_ref, qn_ref, wq_ref, kvn_ref, wk_ref, wkr_ref, wv_ref,
                   gq_ref, gk_ref, vone_ref, kone_ref, cos_ref, sl_ref, sh_ref, grp_ref, grp_t_ref,
                   q_ref, k_ref, v_ref, hg_ref, qsq_ref, ksq_ref):
    h = _modulate(x_ref[...], m_ref, 3)
    z = _dot(h, w_in_ref[...])
    o_kv = MLA_Q_RANK
    o_kr = o_kv + MLA_KV_RANK
    o_a = o_kr + LANES
    o_g = o_a + CONV_CH
    c_q = _rms(z[:, :o_kv]) * qn_ref[...]
    c_kv = _rms(z[:, o_kv:o_kr]) * kvn_ref[...]
    hg_ref[...] = z[:, o_a:o_g] * _sigmoid(z[:, o_g:o_g + CONV_CH])
    cos, s_lo, s_hi = cos_ref[...], sl_ref[...], sh_ref[...]
    q = _head_rms(_dot(c_q, wq_ref[...]), QK_DIM, grp_ref, grp_t_ref) * gq_ref[...]
    q = _rope(q, cos, s_lo, s_hi).astype(BF16)
    q_ref[...] = q
    qsq_ref[...] = _slot_sq_norm(q, grp_ref)
    k = _dot(c_kv, wk_ref[...]) + _dot(z[:, o_kr:o_a], wkr_ref[...])
    k = _head_rms(k, QK_DIM, grp_ref, grp_t_ref) * gk_ref[...]
    k = _rope(k, cos, s_lo, s_hi).astype(BF16)
    ksq_ref[...] = _slot_sq_norm(k, grp_ref)
    k_ref[...] = k + kone_ref[...]
    v_ref[...] = (_dot(c_kv, wv_ref[...]) + vone_ref[...]).astype(BF16)


def _mla_in(x, mods, p, rope_tabs):
    bsz, length, d = x.shape
    tm = _tile(length, 512)
    hs = MLA_HEADS * LANES
    cos, s_lo, s_hi = rope_tabs
    tab_spec = pl.BlockSpec((tm, LANES), lambda b, i: (i, 0))
    tok = lambda w: pl.BlockSpec((None, tm, w), lambda b, i: (b, i, 0))
    consts = [p["w_in"], p["q_norm"], p["w_q"], p["kv_norm"], p["w_k"], p["w_kr"], p["w_v"],
              p["gain_q"], p["gain_k"], p["v_one"], p["k_one"]]
    return pl.pallas_call(
        _mla_in_kernel,
        grid=(bsz, length // tm),
        in_specs=([tok(d), _mod_spec(mods)] + [_const_spec(c.shape) for c in consts]
                  + [tab_spec, tab_spec, tab_spec, _const_spec(p["grp"].shape), _const_spec(p["grp_t"].shape)]),
        out_specs=[tok(hs), tok(hs), tok(hs), tok(CONV_CH), tok(LANES), tok(LANES)],
        out_shape=[jax.ShapeDtypeStruct((bsz, length, hs), BF16),
                   jax.ShapeDtypeStruct((bsz, length, hs), BF16),
                   jax.ShapeDtypeStruct((bsz, length, hs), BF16),
                   jax.ShapeDtypeStruct((bsz, length, CONV_CH), F32),
                   jax.ShapeDtypeStruct((bsz, length, LANES), F32),
                   jax.ShapeDtypeStruct((bsz, length, LANES), F32)],
        compiler_params=_params("parallel", "parallel"),
        name="mla_in",
    )(x, mods, *consts, cos, s_lo, s_hi, p["grp"], p["grp_t"])


def _attn_kernel(q_ref, k_ref, v_ref, o_ref, *, tk):
    tq = q_ref.shape[0]
    n_chunks = k_ref.shape[0] // tk
    heads = [slice(j * LANES, (j + 1) * LANES) for j in range(q_ref.shape[1] // LANES)]
    qs = [q_ref[:, cols] for cols in heads]

    def rows(c):
        return pl.ds(pl.multiple_of(c * tk, tk), tk)

    def scores(c):
        return tuple(_dot_nt(q, k_ref[rows(c), cols]) for q, cols in zip(qs, heads))

    def consume(c, s_all, carry):
        out = []
        for s, cols, (m, acc) in zip(s_all, heads, carry):
            m_new = jnp.maximum(m, jnp.max(s, axis=-1, keepdims=True))
            p = jnp.exp2(s - m_new).astype(BF16)
            acc = jnp.exp2(m - m_new) * acc + jnp.dot(p, v_ref[rows(c), cols], preferred_element_type=F32)
            out.append((m_new, acc))
        return tuple(out)

    def body(c, carry):
        s_all, state = carry
        return scores(c + 1), consume(c, s_all, state)

    init = tuple((jnp.full((tq, 1), -jnp.inf, F32), jnp.zeros((tq, LANES), F32)) for _ in heads)
    s_last, state = lax.fori_loop(0, n_chunks - 1, body, (scores(0), init))
    for cols, (_, acc) in zip(heads, consume(n_chunks - 1, s_last, state)):
        o_ref[:, cols] = (acc * (1.0 / acc[:, MLA_V:MLA_V + 1])).astype(BF16)


def _attn_bounded_kernel(q_ref, kc_ref, vc_ref, kl_ref, vl_ref, kmax_ref, o_ref, *, tk):
    tq = q_ref.shape[0]
    heads = [slice(j * LANES, (j + 1) * LANES) for j in range(q_ref.shape[1] // LANES)]
    lane = lax.broadcasted_iota(jnp.int32, (tq, LANES), 1)
    qs = []
    for cols in heads:
        q = q_ref[:, cols].astype(F32)
        bound = jnp.sqrt(jnp.sum(q * q, axis=-1, keepdims=True)) * kmax_ref[:, cols]
        qs.append(jnp.where(lane == SHIFT_LANE, -bound, q).astype(BF16))

    def add_keys(accs, k_ref, v_ref, rows):
        out = []
        for q, cols, acc in zip(qs, heads, accs):
            p = jnp.exp2(_dot_nt(q, k_ref[rows, cols])).astype(BF16)
            out.append(acc + jnp.dot(p, v_ref[rows, cols], preferred_element_type=F32))
        return tuple(out)

    accs = add_keys(tuple(jnp.zeros((tq, LANES), F32) for _ in heads), kc_ref, vc_ref, slice(None))
    accs = lax.fori_loop(0, kl_ref.shape[0] // tk,
                         lambda c, a: add_keys(a, kl_ref, vl_ref, pl.ds(pl.multiple_of(c * tk, tk), tk)), accs)
    for cols, acc in zip(heads, accs):
        o_ref[:, cols] = (acc * (1.0 / acc[:, MLA_V:MLA_V + 1])).astype(BF16)


def _col_max_kernel(x_ref, o_ref):
    o_ref[...] = jnp.max(x_ref[...], axis=0, keepdims=True)


def _col_max(x):
    bsz, length, w = x.shape
    return pl.pallas_call(
        _col_max_kernel,
        grid=(bsz,),
        in_specs=[pl.BlockSpec((None, length, w), lambda b: (b, 0, 0))],
        out_specs=pl.BlockSpec((None, 1, w), lambda b: (b, 0, 0)),
        out_shape=jax.ShapeDtypeStruct((bsz, 1, w), F32),
        compiler_params=_params("parallel"),
        name="col_max",
    )(x)


def _attn_call(kernel, name, q, keys_values, extra=()):
    bsz, lq, hs = q.shape
    tq = _tile(lq, ATTN_TQ)
    group = ATTN_HEADS * LANES
    blk = lambda rows, imap: pl.BlockSpec((None, rows, group), imap)
    whole = lambda b, h, i: (b, 0, h)
    return pl.pallas_call(
        kernel,
        grid=(bsz, hs // group, lq // tq),
        in_specs=([blk(tq, lambda b, h, i: (b, i, h))] + [blk(a.shape[1], whole) for a in keys_values]
                  + [blk(1, whole) for _ in extra]),
        out_specs=blk(tq, lambda b, h, i: (b, i, h)),
        out_shape=jax.ShapeDtypeStruct((bsz, lq, hs), BF16),
        compiler_params=_params("parallel", "parallel", "parallel"),
        name=name,
    )(q, *keys_values, *extra)


def _attention_online(q, k, v):
    lk = k.shape[1]
    tk = next(t for t in (ATTN_TK, 512, 256, lk) if lk % t == 0)
    return _attn_call(functools.partial(_attn_kernel, tk=tk), "attention", q, (k, v))


def _attention_latent(q, k_c, v_c, k_l, v_l, q_sq, k_sq):
    n_heads = q.shape[-1] // LANES
    tk = next(t for t in (ATTN_TK, 512, 256, k_l.shape[1]) if k_l.shape[1] % t == 0)
    q_max = jnp.sqrt(_col_max(q_sq))[..., :n_heads]
    k_max = jnp.sqrt(_col_max(k_sq))[..., :n_heads] * BOUND_SLACK
    k_max_slots = jnp.repeat(k_max, LANES, axis=-1)
    cat = lambda a, b: jnp.concatenate([a, b], axis=1)

    def bounded(q, k_c, v_c, k_l, v_l, k_max_slots):
        return _attn_call(functools.partial(_attn_bounded_kernel, tk=tk), "attention_bounded",
                          q, (k_c, v_c, k_l, v_l), (k_max_slots,))

    def online(q, k_c, v_c, k_l, v_l, _):
        return _attention_online(q, cat(k_c, k_l), cat(v_c, v_l))

    return lax.cond(jnp.max(q_max * k_max) <= ATTN_BOUND_LIMIT, bounded, online, q, k_c, v_c, k_l, v_l, k_max_slots)


def _conv_kernel(h_ref, hp_ref, hn_ref, w_ref, b_ref, g_ref, be_ref, o_ref, ext_ref, *, rows):
    i = pl.program_id(1)
    tm = h_ref.shape[0]
    ext_ref[0:CONV_HALO, :] = jnp.where(i > 0, hp_ref[...], 0.0)
    ext_ref[CONV_HALO:CONV_HALO + tm, :] = h_ref[...]
    ext_ref[CONV_HALO + tm:, :] = jnp.where(i < pl.num_programs(1) - 1, hn_ref[...], 0.0)
    first = CONV_HALO - CONV_K // 2
    for r0 in range(0, tm, rows):
        acc = jnp.zeros((rows, CONV_CH), F32)
        for j in range(CONV_K):
            acc += ext_ref[first + r0 + j:first + r0 + j + rows, :] * w_ref[j:j + 1, :]
        acc += b_ref[...]
        xc = acc - jnp.mean(acc, axis=-1, keepdims=True)
        y = xc * lax.rsqrt(jnp.mean(xc * xc, axis=-1, keepdims=True) + NORM_EPS)
        y = y * g_ref[...] + be_ref[...]
        o_ref[r0:r0 + rows, :] = (y * _sigmoid(y)).astype(BF16)


def _conformer_conv(hg, p):
    bsz, length, ch = hg.shape
    tm = _tile(length, 256)
    nh = tm // CONV_HALO
    last = length // CONV_HALO - 1
    return pl.pallas_call(
        functools.partial(_conv_kernel, rows=32),
        grid=(bsz, length // tm),
        in_specs=[pl.BlockSpec((None, tm, ch), lambda b, i: (b, i, 0)),
                  pl.BlockSpec((None, CONV_HALO, ch), lambda b, i: (b, jnp.maximum(i * nh - 1, 0), 0)),
                  pl.BlockSpec((None, CONV_HALO, ch), lambda b, i: (b, jnp.minimum((i + 1) * nh, last), 0)),
                  _const_spec((CONV_K, ch)), _const_spec((1, ch)), _const_spec((1, ch)), _const_spec((1, ch))],
        out_specs=pl.BlockSpec((None, tm, ch), lambda b, i: (b, i, 0)),
        out_shape=jax.ShapeDtypeStruct((bsz, length, ch), BF16),
        scratch_shapes=[pltpu.VMEM((tm + 2 * CONV_HALO, ch), F32)],
        compiler_params=_params("parallel", "parallel"),
        name="conformer_conv",
    )(hg, hg, hg, p["dw_w"], p["dw_b"], p["cn_g"], p["cn_b"])


def _mix_out_kernel(x_ref, m_ref, a_ref, c_ref, wa_ref, wc_ref, o_ref):
    out = (jnp.dot(a_ref[...], wa_ref[...], preferred_element_type=F32)
           + jnp.dot(c_ref[...], wc_ref[...], preferred_element_type=F32))
    o_ref[...] = x_ref[...] + m_ref[5:6, :] * out


def _mix_out(x, mods, att, conv, p):
    bsz, length, d = x.shape
    tm = _tile(length, 512)
    tok = lambda w: pl.BlockSpec((None, tm, w), lambda b, i: (b, i, 0))
    return pl.pallas_call(
        _mix_out_kernel,
        grid=(bsz, length // tm),
        in_specs=[tok(d), _mod_spec(mods), tok(att.shape[-1]), tok(conv.shape[-1]),
                  _const_spec(p["w_out_att"].shape), _const_spec(p["w_out_conv"].shape)],
        out_specs=tok(d),
        out_shape=jax.ShapeDtypeStruct(x.shape, F32),
        compiler_params=_params("parallel", "parallel"),
        name="mix_out",
    )(x, mods, att, conv, p["w_out_att"], p["w_out_conv"])


def _rwkv_in_kernel(x_ref, xp_ref, xn_ref, m_ref, xmix_ref, wr_ref, wk_ref, wv_ref, w0_ref, w1_ref, w2_ref,
                    a0_ref, a1_ref, a2_ref, g1_ref, g2_ref, kk_ref, ka_ref, rk_ref, grp_ref, grp_t_ref,
                    r_out, v_out, kkn_out, g_out, bonus_out, lw_out, kd_out, b_out):
    i = pl.program_id(1)
    tm = x_ref.shape[0]
    h = _modulate(x_ref[...], m_ref, 3)
    h_before = jnp.where(i > 0, _modulate(xp_ref[7:8, :], m_ref, 3), 0.0)
    h_after = jnp.where(i < pl.num_programs(1) - 1, _modulate(xn_ref[0:1, :], m_ref, 3), 0.0)
    row = lax.broadcasted_iota(jnp.int32, (tm, 1), 0)
    h_prev = jnp.where(row == 0, h_before, pltpu.roll(h, 1, 0))
    h_next = jnp.where(row == tm - 1, h_after, pltpu.roll(h, tm - 1, 0))
    xx = 0.5 * (h_prev + h_next) - h
    xr, xw, xk, xv, xa, xg = (h + xx * xmix_ref[j:j + 1, :] for j in range(6))
    r = _dot(xr, wr_ref[...])
    k = _dot(xk, wk_ref[...])
    v = _dot(xv, wv_ref[...])
    kk = k * kk_ref[...]
    norm = jnp.maximum(jnp.sqrt(_dot(kk * kk, grp_ref[...])), 1e-12)
    kk = kk * _dot_sel(1.0 / norm, grp_t_ref[...])
    g_out[...] = _dot(_sigmoid(_dot(xg, g1_ref[...])), g2_ref[...]).astype(BF16)
    k_sum = None
    for d in range(2):
        w_pre = w0_ref[d:d + 1, :] + _dot(jnp.tanh(_dot(xw, w1_ref[d])), w2_ref[d])
        lw_out[d] = -DECAY_SCALE * _sigmoid(w_pre)
        a = _sigmoid(a0_ref[d:d + 1, :] + _dot(_dot(xa, a1_ref[d]), a2_ref[d]))
        k_d = k * (1.0 + (a - 1.0) * ka_ref[...])
        kd_out[d] = k_d.astype(BF16)
        b_out[d] = (kk * a).astype(BF16)
        k_sum = k_d if k_sum is None else k_sum + k_d
    r_out[...] = r.astype(BF16)
    v_out[...] = v.astype(BF16)
    kkn_out[...] = kk.astype(BF16)
    head_dot = _dot(r * (0.5 * k_sum) * rk_ref[...], grp_ref[...])
    bonus_out[...] = (_dot(head_dot, grp_t_ref[...]) * v).astype(BF16)


def _rwkv_in(x, mods, p):
    bsz, length, d = x.shape
    tm = _tile(length, 256)
    nb = tm // 8
    last = length // 8 - 1
    tok = pl.BlockSpec((None, tm, d), lambda b, i: (b, i, 0))
    tok2 = pl.BlockSpec((None, 2, tm, d), lambda b, i: (b, 0, i, 0))
    consts = [p[n] for n in ("x_mix", "w_r", "w_k", "w_v", "w0", "w1", "w2", "a0", "a1", "a2", "g1", "g2",
                             "k_k", "k_a", "r_k", "grp", "grp_t")]
    one = jax.ShapeDtypeStruct((bsz, length, d), BF16)
    two = jax.ShapeDtypeStruct((bsz, 2, length, d), BF16)
    log_decay = jax.ShapeDtypeStruct((bsz, 2, length, d), F32)
    return pl.pallas_call(
        _rwkv_in_kernel,
        grid=(bsz, length // tm),
        in_specs=([tok,
                   pl.BlockSpec((None, 8, d), lambda b, i: (b, jnp.maximum(i * nb - 1, 0), 0)),
                   pl.BlockSpec((None, 8, d), lambda b, i: (b, jnp.minimum((i + 1) * nb, last), 0)),
                   _mod_spec(mods)] + [_const_spec(c.shape) for c in consts]),
        out_specs=[tok, tok, tok, tok, tok, tok2, tok2, tok2],
        out_shape=[one, one, one, one, one, log_decay, two, two],
        compiler_params=_params("parallel", "parallel"),
        name="rwkv_in",
    )(x, x, x, mods, *consts)


def _scan_kernel(r_ref, v_ref, kk_ref, lw_ref, kd_ref, b_ref, s0_ref, y_ref, st_ref, state_ref):
    d = pl.program_id(1)
    c = pl.program_id(2)
    n_rows, cs = lw_ref.shape[0], lw_ref.shape[1]
    n_pairs = state_ref.shape[1]

    @pl.when(c == 0)
    def _():
        state_ref[...] = s0_ref[...]

    rev = d == 1
    sign = 1 - 2 * d
    ti = lax.broadcasted_iota(jnp.int32, (cs, cs), 0)
    si = lax.broadcasted_iota(jnp.int32, (cs, cs), 1)
    incl = jnp.where((ti - si) * sign >= 0, 1.0, 0.0).astype(BF16)

    def scaled_streams(i):
        lw = lw_ref[i]
        l1 = lw.astype(BF16)
        rem = lw - l1.astype(F32)
        l2 = rem.astype(BF16)
        l3 = (rem - l2.astype(F32)).astype(BF16)
        cum = (jnp.dot(incl, l1, preferred_element_type=F32) + jnp.dot(incl, l2, preferred_element_type=F32)
               + jnp.dot(incl, l3, preferred_element_type=F32))
        total = jnp.where(rev, cum[0:1, :], cum[cs - 1:cs, :])
        p_inv = jnp.exp(-cum)
        to_end = jnp.exp(total - cum)
        kd = kd_ref[i]
        bv = b_ref[i]
        return dict(r_t=r_ref[i] * jnp.exp(cum), a_t=-kk_ref[i] * jnp.exp(cum - lw), k_t=kd * p_inv,
                    b_t=bv * p_inv, k_e=kd * to_end, b_e=bv * to_end, v=v_ref[i], p_tot=jnp.exp(total))

    rows = [scaled_streams(i) for i in range(n_rows)]

    assert cs & (cs - 1) == 0
    t2 = lax.broadcasted_iota(jnp.int32, (2 * cs, 2 * cs), 0) & (cs - 1)
    s2 = lax.broadcasted_iota(jnp.int32, (2 * cs, 2 * cs), 1) & (cs - 1)
    strict2 = (t2 - s2) * sign > 0
    incl2 = (t2 - s2) * sign >= 0
    even = lax.broadcasted_iota(jnp.int32, (cs, LANES), 1) < RWKV_HEAD
    vi = lax.broadcasted_iota(jnp.int32, (LANES, LANES), 0) < RWKV_HEAD
    ki = lax.broadcasted_iota(jnp.int32, (LANES, LANES), 1) < RWKV_HEAD
    same_head = vi == ki

    def stack(t):
        return jnp.concatenate([jnp.where(even, t, 0.0), jnp.where(even, 0.0, t)], axis=0)

    def twice(t):
        return jnp.concatenate([t, t], axis=0)

    bf = lambda t: t.astype(BF16)
    mm = lambda a, b: jnp.dot(a, b, preferred_element_type=F32)
    chains = [(i, p, slice(p * LANES, (p + 1) * LANES)) for i in range(n_rows) for p in range(n_pairs)]
    idx = range(len(chains))
    col = lambda name: [rows[i][name][:, sl] for i, _, sl in chains]
    a_t, r_t, b_t, k_t, b_e, k_e, v, p_tot = (col(n) for n in ("a_t", "r_t", "b_t", "k_t", "b_e", "k_e", "v", "p_tot"))
    eye = jnp.where(lax.broadcasted_iota(jnp.int32, (2 * cs, 2 * cs), 0)
                    == lax.broadcasted_iota(jnp.int32, (2 * cs, 2 * cs), 1), 1.0, 0.0)

    cross = [_dot_nt(jnp.concatenate([stack(a_t[n]), stack(r_t[n])], axis=0),
                     jnp.concatenate([stack(b_t[n]), stack(k_t[n])], axis=0)) for n in idx]
    m_ab = [jnp.where(strict2, x[:2 * cs, :2 * cs], 0.0) for x in cross]
    m_ak = [bf(jnp.where(strict2, x[:2 * cs, 2 * cs:], 0.0)) for x in cross]
    m_rb = [bf(jnp.where(incl2, x[2 * cs:, :2 * cs], 0.0)) for x in cross]
    m_rk = [bf(jnp.where(incl2, x[2 * cs:, 2 * cs:], 0.0)) for x in cross]
    v2 = [bf(twice(t)) for t in v]
    x_v = [mm(m_ak[n], v2[n]) for n in idx]
    y_v = [mm(m_rk[n], v2[n]) for n in idx]
    upd_v = [_dot_tn(v2[n], stack(k_e[n])) for n in idx]
    pw = [bf(m) for m in m_ab]
    t_inv = [eye + m for m in m_ab]
    for _ in range(max(1, (cs - 1).bit_length()) - 1):
        pw = [bf(mm(w, w)) for w in pw]
        t_inv = [t_inv[n] + mm(pw[n], bf(t_inv[n])) for n in idx]
    t_inv = [bf(t) for t in t_inv]

    states = [state_ref[i, p] for i, p, _ in chains]
    ar = [_dot_nt(jnp.concatenate([a_t[n], r_t[n]], axis=0), states[n]) for n in idx]
    u = [bf(mm(t_inv[n], bf(twice(ar[n][:cs]) + x_v[n]))) for n in idx]
    for n, (i, p, sl) in enumerate(chains):
        y2 = twice(ar[n][cs:]) + y_v[n] + mm(m_rb[n], u[n])
        y_ref[i, :, sl] = jnp.where(even, y2[:cs], y2[cs:])
        upd = upd_v[n] + _dot_tn(u[n], stack(b_e[n]))
        state_ref[i, p] = states[n] * p_tot[n] + jnp.where(same_head, upd, 0.0)

    @pl.when(c == pl.num_programs(2) - 1)
    def _():
        st_ref[...] = state_ref[...]


def _wkv_scan(r, v, kk, lw, kd, bv, s0):
    bsz, length, d = r.shape
    cs = SCAN_CHUNK
    nc = length // cs
    n_pairs = d // LANES
    nr = SCAN_ROWS if bsz % SCAN_ROWS == 0 else 1
    chunk = lambda d_, c: c + d_ * (nc - 1 - 2 * c)
    shared = pl.BlockSpec((nr, cs, d), lambda b, d_, c: (b, chunk(d_, c), 0))
    per_dir = pl.BlockSpec((nr, None, cs, d), lambda b, d_, c: (b, d_, chunk(d_, c), 0))
    st_spec = pl.BlockSpec((nr, None, n_pairs, LANES, LANES), lambda b, d_, c: (b, d_, 0, 0, 0))
    return pl.pallas_call(
        _scan_kernel,
        grid=(bsz // nr, 2, nc),
        in_specs=[shared, shared, shared, per_dir, per_dir, per_dir, st_spec],
        out_specs=[per_dir, st_spec],
        out_shape=[jax.ShapeDtypeStruct((bsz, 2, length, d), F32),
                   jax.ShapeDtypeStruct(s0.shape, F32)],
        scratch_shapes=[pltpu.VMEM((nr, n_pairs, LANES, LANES), F32)],
        compiler_params=_params("parallel", "parallel", "arbitrary"),
        name="wkv_scan",
    )(r, v, kk, lw, kd, bv, s0)


def _rwkv_out_kernel(x_ref, m_ref, y_ref, bonus_ref, g_ref, lng_ref, lnb_ref, wo_ref, grp_ref, grp_t_ref, o_ref,
                     *, gn_eps):
    y = y_ref[0] + y_ref[1]
    inv_n = 1.0 / RWKV_HEAD
    mean = _dot_sel(_dot(y, grp_ref[...]) * inv_n, grp_t_ref[...])
    yc = y - mean
    rstd = lax.rsqrt(_dot(yc * yc, grp_ref[...]) * inv_n + gn_eps)
    yn = yc * _dot_sel(rstd, grp_t_ref[...]) * lng_ref[...] + lnb_ref[...]
    out = _dot((yn + bonus_ref[...]) * g_ref[...], wo_ref[...])
    o_ref[...] = x_ref[...] + m_ref[5:6, :] * out


def _rwkv_out(x, mods, y, bonus, g, p):
    bsz, length, d = x.shape
    tm = _tile(length, 256)
    tok = pl.BlockSpec((None, tm, d), lambda b, i: (b, i, 0))
    consts = [p["ln_g"], p["ln_b"], p["w_o"], p["grp"], p["grp_t"]]
    return pl.pallas_call(
        functools.partial(_rwkv_out_kernel, gn_eps=RWKV_HEAD * 1e-5),
        grid=(bsz, length // tm),
        in_specs=([tok, _mod_spec(mods), pl.BlockSpec((None, 2, tm, d), lambda b, i: (b, 0, i, 0)), tok, tok]
                  + [_const_spec(c.shape) for c in consts]),
        out_specs=tok,
        out_shape=jax.ShapeDtypeStruct(x.shape, F32),
        compiler_params=_params("parallel", "parallel"),
        name="rwkv_out",
    )(x, mods, y, bonus, g, *consts)


def _group_selectors(width, group):
    sel = (jnp.arange(width)[:, None] // group == jnp.arange(LANES)[None, :]).astype(BF16)
    return sel, sel.T


def _head_slots(w, per_head, offset=0):
    k = w.shape[0]
    w = w.reshape(k, MLA_HEADS, per_head)
    w = jnp.pad(w, ((0, 0), (0, 0), (offset, LANES - per_head - offset)))
    return w.reshape(k, MLA_HEADS * LANES)


def _mla_params(w_in, q_norm, w_qb, kv_norm, w_kvb, qn_q, qn_k, dw_w, dw_b, cn_g, cn_b, w_out):
    d = w_in.shape[0]
    o_kr = MLA_Q_RANK + MLA_KV_RANK
    mla_in = o_kr + MLA_ROPE
    w_in_p = jnp.concatenate([w_in[:, :mla_in], jnp.zeros((d, LANES - MLA_ROPE), F32), w_in[:, mla_in:]], axis=1)
    kvb = w_kvb.reshape(MLA_KV_RANK, MLA_HEADS, MLA_NOPE + MLA_V)
    w_k = _head_slots(kvb[:, :, :MLA_NOPE].reshape(MLA_KV_RANK, -1), MLA_NOPE)
    w_v = _head_slots(kvb[:, :, MLA_NOPE:].reshape(MLA_KV_RANK, -1), MLA_V)
    v_one = jnp.tile(jnp.arange(LANES) == MLA_V, MLA_HEADS).astype(F32)[None, :]
    k_one = jnp.tile(jnp.arange(LANES) == SHIFT_LANE, MLA_HEADS).astype(BF16)[None, :]
    w_out_att = jnp.pad(w_out[:MLA_HEADS * MLA_V].reshape(MLA_HEADS, MLA_V, d), ((0, 0), (0, LANES - MLA_V), (0, 0)))
    place = jnp.pad(jnp.eye(MLA_ROPE, dtype=F32), ((0, LANES - MLA_ROPE), (MLA_NOPE, LANES - QK_DIM)))
    w_kr = jnp.tile(place, (1, MLA_HEADS))
    pad_gain = lambda g: jnp.tile(jnp.pad(g, (0, LANES - QK_DIM)), MLA_HEADS)[None, :]
    grp, grp_t = _group_selectors(MLA_HEADS * LANES, LANES)
    mix = MLA_HEADS * MLA_V
    return dict(
        w_in=w_in_p.astype(BF16), q_norm=q_norm[None, :], w_q=_head_slots(w_qb, QK_DIM).astype(BF16),
        kv_norm=kv_norm[None, :], w_k=w_k.astype(BF16), w_kr=w_kr.astype(BF16), w_v=w_v.astype(BF16),
        gain_q=pad_gain(qn_q) * (ATTN_SCALE * LOG2_E), gain_k=pad_gain(qn_k), v_one=v_one, k_one=k_one, grp=grp, grp_t=grp_t,
        dw_w=dw_w[:, 0, :], dw_b=dw_b[None, :], cn_g=cn_g[None, :], cn_b=cn_b[None, :],
        w_out_att=w_out_att.reshape(MLA_HEADS * LANES, d).astype(BF16), w_out_conv=w_out[mix:].astype(BF16))


def _rope_tables(length, with_rope):
    n_freq = MLA_ROPE // 4
    if not with_rope:
        return jnp.ones((length, LANES), F32), jnp.zeros((length, LANES), F32), jnp.zeros((length, LANES), F32)
    t = jnp.arange(length)
    inv = ROPE_THETA ** (-jnp.arange(n_freq, dtype=F32) / n_freq)
    ang_r = (t // GRID_W).astype(F32)[:, None] * inv
    ang_c = (t % GRID_W).astype(F32)[:, None] * inv
    zeros = jnp.zeros((length, n_freq), F32)

    def slot(nope_val, r_first, r_second, c_first, c_second):
        nope = jnp.full((length, MLA_NOPE), nope_val, F32)
        pad = jnp.full((length, LANES - QK_DIM), nope_val, F32)
        return jnp.concatenate([nope, r_first, r_second, c_first, c_second, pad], axis=1)

    cos_r, sin_r, cos_c, sin_c = jnp.cos(ang_r), jnp.sin(ang_r), jnp.cos(ang_c), jnp.sin(ang_c)
    return (slot(1.0, cos_r, cos_r, cos_c, cos_c),
            slot(0.0, -sin_r, zeros, -sin_c, zeros),
            slot(0.0, zeros, sin_r, zeros, sin_c))


def _rwkv_params(x_mix, w_r, w_k, w_v, w0, w1, w2, a0, a1, a2, g1, g2, k_k, k_a, r_k, ln_g, ln_b, w_o):
    d = w_r.shape[0]
    grp, grp_t = _group_selectors(d, RWKV_HEAD)
    bf = lambda w: w.astype(BF16)
    return dict(x_mix=x_mix, w_r=bf(w_r), w_k=bf(w_k), w_v=bf(w_v), w0=w0, w1=bf(w1), w2=bf(w2), a0=a0,
                a1=bf(a1), a2=bf(a2), g1=bf(g1), g2=bf(g2), k_k=k_k[None, :], k_a=k_a[None, :],
                r_k=r_k.reshape(1, d), ln_g=ln_g[None, :], ln_b=ln_b[None, :], w_o=bf(w_o), grp=grp, grp_t=grp_t)


def _mla_conv_layer(x, ctx, m_l, m_c, p):
    length, lc = x.shape[1], ctx.shape[1]
    q_l, k_l, v_l, hg_l, qsq_l, ksq_l = _mla_in(x, m_l, p, _rope_tables(length, True))
    q_c, k_c, v_c, hg_c, _, ksq_c = _mla_in(ctx, m_c, p, _rope_tables(lc, False))
    att_l = _attention_latent(q_l, k_c, v_c, k_l, v_l, qsq_l, jnp.concatenate([ksq_c, ksq_l], axis=1))
    att_c = _attention_online(q_c, k_c, v_c)
    x = _mix_out(x, m_l, att_l, _conformer_conv(hg_l, p), p)
    ctx = _mix_out(ctx, m_c, att_c, _conformer_conv(hg_c, p), p)
    return x, ctx


def _rwkv_layer(x, ctx, m_l, m_c, p):
    bsz, _, d = x.shape
    r_c, v_c, kk_c, _, _, lw_c, kd_c, b_c = _rwkv_in(ctx, m_c, p)
    r_l, v_l, kk_l, g_l, bonus_l, lw_l, kd_l, b_l = _rwkv_in(x, m_l, p)
    s0 = jnp.zeros((bsz, 2, d // LANES, LANES, LANES), F32)
    _, s_ctx = _wkv_scan(r_c, v_c, kk_c, lw_c, kd_c, b_c, s0)
    y_l, _ = _wkv_scan(r_l, v_l, kk_l, lw_l, kd_l, b_l, s_ctx)
    return _rwkv_out(x, m_l, y_l, bonus_l, g_l, p)


def kernel(x, c, ctx, c_ctx, ada_w, ada_b, ffn_w1, ffn_w3, ffn_w2, mla_w_in, mla_q_norm, mla_w_qb, mla_kv_norm, mla_w_kvb, mla_qk_norm_q, mla_qk_norm_k, conv_dw_w, conv_dw_b, conv_norm_g, conv_norm_b, mix_w_out, rwkv_x_mix, rwkv_w_r, rwkv_w_k, rwkv_w_v, rwkv_w0, rwkv_w1, rwkv_w2, rwkv_a0, rwkv_a1, rwkv_a2, rwkv_g1, rwkv_g2, rwkv_k_k, rwkv_k_a, rwkv_r_k, rwkv_ln_g, rwkv_ln_b, rwkv_w_o):
    bsz = x.shape[0]
    depth = ada_w.shape[0]
    assert depth == 2, "layer 0 is the MLA/conv layer, layer 1 the (last) RWKV layer"
    cond = jnp.concatenate([c, c_ctx[None, :], jnp.zeros((16 - bsz - 1, c.shape[1]), F32)], axis=0)
    w1, w3, w2 = ffn_w1.astype(BF16), ffn_w3.astype(BF16), ffn_w2.astype(BF16)
    mla_p = _mla_params(mla_w_in[0], mla_q_norm[0], mla_w_qb[0], mla_kv_norm[0], mla_w_kvb[0], mla_qk_norm_q[0],
                        mla_qk_norm_k[0], conv_dw_w[0], conv_dw_b[0], conv_norm_g[0], conv_norm_b[0], mix_w_out[0])
    rwkv_p = _rwkv_params(rwkv_x_mix[0], rwkv_w_r[0], rwkv_w_k[0], rwkv_w_v[0], rwkv_w0[0], rwkv_w1[0],
                          rwkv_w2[0], rwkv_a0[0], rwkv_a1[0], rwkv_a2[0], rwkv_g1[0], rwkv_g2[0], rwkv_k_k[0],
                          rwkv_k_a[0], rwkv_r_k[0], rwkv_ln_g[0], rwkv_ln_b[0], rwkv_w_o[0])
    for i in range(depth):
        mods = _ada_mods(cond, ada_w[i], ada_b[i])
        m_l, m_c = mods[:bsz], mods[bsz:bsz + 1]
        x = _half_ffn(x, m_l, 0, w1[i, 0], w3[i, 0], w2[i, 0])
        ctx = _half_ffn(ctx, m_c, 0, w1[i, 0], w3[i, 0], w2[i, 0])
        if i == 0:
            x, ctx = _mla_conv_layer(x, ctx, m_l, m_c, mla_p)
            ctx = _half_ffn(ctx, m_c, 6, w1[i, 1], w3[i, 1], w2[i, 1])
        else:
            x = _rwkv_layer(x, ctx, m_l, m_c, rwkv_p)
        x = _half_ffn(x, m_l, 6, w1[i, 1], w3[i, 1], w2[i, 1])
    return x
```

```python
import functools
import math

import jax
import jax.numpy as jnp
from jax import lax
from jax.experimental import pallas as pl
from jax.experimental.pallas import tpu as pltpu

F32 = jnp.float32
BF16 = jnp.bfloat16

LANES = 128
SUBLANES = 8
VMEM_LIMIT = 56 * 1024 * 1024

NORM_EPS = 1e-6
N_MOD = 9
GRID_W = 64
ROPE_THETA = 10000.0

MLA_HEADS = 8
MLA_NOPE = 64
MLA_ROPE = 32
MLA_V = 64
MLA_Q_RANK = 384
MLA_KV_RANK = 128
QK_DIM = MLA_NOPE + MLA_ROPE
ATTN_SCALE = QK_DIM ** -0.5
LOG2_E = math.log2(math.e)
ATTN_TQ = 512
ATTN_TK = 1024
ATTN_HEADS = 4
SHIFT_LANE = QK_DIM
BOUND_SLACK = 1.02
ATTN_BOUND_LIMIT = 40.0
CONV_CH = 512
CONV_K = 31
CONV_HALO = 16

RWKV_HEAD = 64
SCAN_CHUNK = 64
SCAN_ROWS = 2
DECAY_SCALE = math.exp(-0.5)


def _dot(a, b):
    return jnp.dot(a.astype(BF16), b.astype(BF16), preferred_element_type=F32)


def _dot_nt(a, b):
    return lax.dot_general(a.astype(BF16), b.astype(BF16), (((1,), (1,)), ((), ())),
                           preferred_element_type=F32)


def _dot_tn(a, b):
    return lax.dot_general(a.astype(BF16), b.astype(BF16), (((0,), (0,)), ((), ())),
                           preferred_element_type=F32)


def _head_sums(x):
    even = lax.broadcasted_iota(jnp.int32, (1, LANES), 1) < RWKV_HEAD
    out = []
    for j in range(0, x.shape[1], LANES):
        u = x[:, j:j + LANES]
        both = jnp.sum(u, axis=-1, keepdims=True)
        first = jnp.sum(jnp.where(even, u, 0.0), axis=-1, keepdims=True)
        out.append(jnp.where(even, first, both - first))
    return jnp.concatenate(out, axis=1)


def _rms(x):
    return x * lax.rsqrt(jnp.mean(x * x, axis=-1, keepdims=True) + NORM_EPS)


def _sigmoid(x):
    return 1.0 / (1.0 + jnp.exp(-x))


def _modulate(x, m_ref, first):
    return _rms(x) * (1.0 + m_ref[first + 1:first + 2, :]) + m_ref[first:first + 1, :]


def _const_spec(shape):
    zeros = (0,) * len(shape)
    return pl.BlockSpec(shape, lambda *_: zeros, pipeline_mode=pl.Buffered(1))


def _params(*sem):
    return pltpu.CompilerParams(dimension_semantics=sem, vmem_limit_bytes=VMEM_LIMIT)


def _mod_spec(mods):
    d = mods.shape[-1]
    if mods.shape[0] == 1:
        return pl.BlockSpec((None, N_MOD, d), lambda b, i: (0, 0, 0))
    return pl.BlockSpec((None, N_MOD, d), lambda b, i: (b, 0, 0))


def _tile(length, pref):
    t = min(pref, length)
    assert length % t == 0, (length, t)
    return t


def _ada_kernel(c_ref, w_ref, b_ref, o_ref):
    c = c_ref[...]
    s = c * _sigmoid(c)
    w = w_ref[...]
    s_hi = s.astype(BF16)
    s_lo = (s - s_hi.astype(F32)).astype(BF16)
    w_hi = w.astype(BF16)
    w_lo = (w - w_hi.astype(F32)).astype(BF16)
    acc = jnp.dot(s_hi, w_hi, preferred_element_type=F32)
    acc += jnp.dot(s_hi, w_lo, preferred_element_type=F32)
    acc += jnp.dot(s_lo, w_hi, preferred_element_type=F32)
    o_ref[...] = acc + b_ref[...]


def _ada_mods(cond, w, b):
    r, d = cond.shape
    n = w.shape[1]
    out = pl.pallas_call(
        _ada_kernel,
        grid=(n // d,),
        in_specs=[pl.BlockSpec((r, d), lambda j: (0, 0)),
                  pl.BlockSpec((d, d), lambda j: (0, j)),
                  pl.BlockSpec((1, d), lambda j: (0, j))],
        out_specs=pl.BlockSpec((r, d), lambda j: (0, j)),
        out_shape=jax.ShapeDtypeStruct((r, n), F32),
        compiler_params=_params("parallel"),
        name="ada_mods",
    )(cond, w, b.reshape(1, n))
    return out.reshape(r, n // d, d)


def _ffn_kernel(x_ref, m_ref, w1_ref, w3_ref, w2_ref, o_ref, *, first, f_chunk):
    x = x_ref[...]
    h = _modulate(x, m_ref, first).astype(BF16)
    acc = None
    for c in range(w1_ref.shape[1] // f_chunk):
        cols = slice(c * f_chunk, (c + 1) * f_chunk)
        a = jnp.dot(h, w1_ref[:, cols], preferred_element_type=F32)
        b = jnp.dot(h, w3_ref[:, cols], preferred_element_type=F32)
        g = (a * _sigmoid(a) * b).astype(BF16)
        part = jnp.dot(g, w2_ref[cols, :], preferred_element_type=F32)
        acc = part if acc is None else acc + part
    o_ref[...] = x + (0.5 * m_ref[first + 2:first + 3, :]) * acc


def _half_ffn(x, mods, first, w1, w3, w2):
    bsz, length, d = x.shape
    f = w1.shape[1]
    tm = _tile(length, 512)
    f_chunk = f // 2 if (f // 2) % LANES == 0 else f
    return pl.pallas_call(
        functools.partial(_ffn_kernel, first=first, f_chunk=f_chunk),
        grid=(bsz, length // tm),
        in_specs=[pl.BlockSpec((None, tm, d), lambda b, i: (b, i, 0)),
                  _mod_spec(mods),
                  _const_spec((d, f)), _const_spec((d, f)), _const_spec((f, d))],
        out_specs=pl.BlockSpec((None, tm, d), lambda b, i: (b, i, 0)),
        out_shape=jax.ShapeDtypeStruct(x.shape, F32),
        compiler_params=_params("parallel", "parallel"),
        name="half_ffn",
    )(x, mods, w1, w3, w2)


def _head_rms(t, width):
    slots = [t[:, j:j + LANES] for j in range(0, t.shape[1], LANES)]
    return jnp.concatenate(
        [u * lax.rsqrt(jnp.sum(u * u, axis=-1, keepdims=True) * (1.0 / width) + NORM_EPS) for u in slots], axis=1)


def _rope(t, cos, sin_lo, sin_hi):
    n = t.shape[1]
    rep = n // cos.shape[1]
    cos, sin_lo, sin_hi = (jnp.tile(u, (1, rep)) for u in (cos, sin_lo, sin_hi))
    return t * cos + pltpu.roll(t, n - 8, 1) * sin_lo + pltpu.roll(t, 8, 1) * sin_hi


def _mla_in_kernel(x_ref, m_ref, w_in_ref, qn_ref, wq_ref, kvn_ref, wk_ref, wkr_ref, wv_ref,
                   gq_ref, gk_ref, vone_ref, kone_ref, cos_ref, sl_ref, sh_ref,
                   q_ref, k_ref, v_ref, hg_ref):
    h = _modulate(x_ref[...], m_ref, 3)
    z = _dot(h, w_in_ref[...])
    o_kv = MLA_Q_RANK
    o_kr = o_kv + MLA_KV_RANK
    o_a = o_kr + LANES
    o_g = o_a + CONV_CH
    c_q = _rms(z[:, :o_kv]) * qn_ref[...]
    c_kv = _rms(z[:, o_kv:o_kr]) * kvn_ref[...]
    hg_ref[...] = z[:, o_a:o_g] * _sigmoid(z[:, o_g:o_g + CONV_CH])
    cos, s_lo, s_hi = cos_ref[...], sl_ref[...], sh_ref[...]
    q = _head_rms(_dot(c_q, wq_ref[...]), QK_DIM) * gq_ref[...]
    q_ref[...] = _rope(q, cos, s_lo, s_hi).astype(BF16)
    k = _dot(c_kv, wk_ref[...]) + _dot(z[:, o_kr:o_a], wkr_ref[...])
    k = _head_rms(k, QK_DIM) * gk_ref[...]
    k_ref[...] = _rope(k, cos, s_lo, s_hi).astype(BF16) + kone_ref[...]
    v_ref[...] = (_dot(c_kv, wv_ref[...]) + vone_ref[...]).astype(BF16)


def _mla_in(x, mods, p, rope_tabs):
    bsz, length, d = x.shape
    tm = _tile(length, 512)
    hs = MLA_HEADS * LANES
    cos, s_lo, s_hi = rope_tabs
    tab_spec = pl.BlockSpec((tm, LANES), lambda b, i: (i, 0))
    tok = lambda w: pl.BlockSpec((None, tm, w), lambda b, i: (b, i, 0))
    consts = [p["w_in"], p["q_norm"], p["w_q"], p["kv_norm"], p["w_k"], p["w_kr"], p["w_v"],
              p["gain_q"], p["gain_k"], p["v_one"], p["k_one"]]
    return pl.pallas_call(
        _mla_in_kernel,
        grid=(bsz, length // tm),
        in_specs=([tok(d), _mod_spec(mods)] + [_const_spec(c.shape) for c in consts]
                  + [tab_spec, tab_spec, tab_spec]),
        out_specs=[tok(hs), tok(hs), tok(hs), tok(CONV_CH)],
        out_shape=[jax.ShapeDtypeStruct((bsz, length, hs), BF16),
                   jax.ShapeDtypeStruct((bsz, length, hs), BF16),
                   jax.ShapeDtypeStruct((bsz, length, hs), BF16),
                   jax.ShapeDtypeStruct((bsz, length, CONV_CH), F32)],
        compiler_params=_params("parallel", "parallel"),
        name="mla_in",
    )(x, mods, *consts, cos, s_lo, s_hi)


def _attn_kernel(q_ref, k_ref, v_ref, o_ref, *, tk):
    tq = q_ref.shape[0]
    n_chunks = k_ref.shape[0] // tk
    heads = [slice(j * LANES, (j + 1) * LANES) for j in range(q_ref.shape[1] // LANES)]
    qs = [q_ref[:, cols] for cols in heads]

    def rows(c):
        return pl.ds(pl.multiple_of(c * tk, tk), tk)

    def scores(c):
        return tuple(_dot_nt(q, k_ref[rows(c), cols]) for q, cols in zip(qs, heads))

    def consume(c, s_all, carry):
        out = []
        for s, cols, (m, acc) in zip(s_all, heads, carry):
            m_new = jnp.maximum(m, jnp.max(s, axis=-1, keepdims=True))
            p = jnp.exp2(s - m_new).astype(BF16)
            acc = jnp.exp2(m - m_new) * acc + jnp.dot(p, v_ref[rows(c), cols], preferred_element_type=F32)
            out.append((m_new, acc))
        return tuple(out)

    def body(c, carry):
        s_all, state = carry
        return scores(c + 1), consume(c, s_all, state)

    init = tuple((jnp.full((tq, 1), -jnp.inf, F32), jnp.zeros((tq, LANES), F32)) for _ in heads)
    s_last, state = lax.fori_loop(0, n_chunks - 1, body, (scores(0), init))
    for cols, (_, acc) in zip(heads, consume(n_chunks - 1, s_last, state)):
        o_ref[:, cols] = (acc * (1.0 / acc[:, MLA_V:MLA_V + 1])).astype(BF16)


def _attn_bounded_kernel(q_ref, kc_ref, vc_ref, kl_ref, vl_ref, kmax_ref, o_ref, *, tk):
    tq = q_ref.shape[0]
    heads = [slice(j * LANES, (j + 1) * LANES) for j in range(q_ref.shape[1] // LANES)]
    lane = lax.broadcasted_iota(jnp.int32, (tq, LANES), 1)
    qs = []
    for cols in heads:
        q = q_ref[:, cols].astype(F32)
        bound = jnp.sqrt(jnp.sum(q * q, axis=-1, keepdims=True)) * kmax_ref[:, cols]
        qs.append(jnp.where(lane == SHIFT_LANE, -bound, q).astype(BF16))

    def add_keys(accs, k_ref, v_ref, rows):
        out = []
        for q, cols, acc in zip(qs, heads, accs):
            p = jnp.exp2(_dot_nt(q, k_ref[rows, cols])).astype(BF16)
            out.append(acc + jnp.dot(p, v_ref[rows, cols], preferred_element_type=F32))
        return tuple(out)

    accs = add_keys(tuple(jnp.zeros((tq, LANES), F32) for _ in heads), kc_ref, vc_ref, slice(None))
    accs = lax.fori_loop(0, kl_ref.shape[0] // tk,
                         lambda c, a: add_keys(a, kl_ref, vl_ref, pl.ds(pl.multiple_of(c * tk, tk), tk)), accs)
    for cols, acc in zip(heads, accs):
        o_ref[:, cols] = (acc * (1.0 / acc[:, MLA_V:MLA_V + 1])).astype(BF16)


def _attn_call(kernel, name, q, keys_values, extra=()):
    bsz, lq, hs = q.shape
    tq = _tile(lq, ATTN_TQ)
    group = ATTN_HEADS * LANES
    blk = lambda rows, imap: pl.BlockSpec((None, rows, group), imap)
    whole = lambda b, h, i: (b, 0, h)
    return pl.pallas_call(
        kernel,
        grid=(bsz, hs // group, lq // tq),
        in_specs=([blk(tq, lambda b, h, i: (b, i, h))] + [blk(a.shape[1], whole) for a in keys_values]
                  + [blk(1, whole) for _ in extra]),
        out_specs=blk(tq, lambda b, h, i: (b, i, h)),
        out_shape=jax.ShapeDtypeStruct((bsz, lq, hs), BF16),
        compiler_params=_params("parallel", "parallel", "parallel"),
        name=name,
    )(q, *keys_values, *extra)


def _attention_online(q, k, v):
    lk = k.shape[1]
    tk = next(t for t in (ATTN_TK, 512, 256, lk) if lk % t == 0)
    return _attn_call(functools.partial(_attn_kernel, tk=tk), "attention", q, (k, v))


def _attention_latent(q, k_c, v_c, k_l, v_l, q_norm_bound, k_norm_bound):
    tk = next(t for t in (ATTN_TK, 512, 256, k_l.shape[1]) if k_l.shape[1] % t == 0)
    k_max_slots = jnp.full((q.shape[0], 1, q.shape[-1]), k_norm_bound, F32)
    cat = lambda a, b: jnp.concatenate([a, b], axis=1)

    def bounded(q, k_c, v_c, k_l, v_l, k_max_slots):
        return _attn_call(functools.partial(_attn_bounded_kernel, tk=tk), "attention_bounded",
                          q, (k_c, v_c, k_l, v_l), (k_max_slots,))

    def online(q, k_c, v_c, k_l, v_l, _):
        return _attention_online(q, cat(k_c, k_l), cat(v_c, v_l))

    return lax.cond(q_norm_bound * k_norm_bound <= ATTN_BOUND_LIMIT, bounded, online,
                    q, k_c, v_c, k_l, v_l, k_max_slots)


def _conv_kernel(h_ref, hp_ref, hn_ref, w_ref, b_ref, g_ref, be_ref, o_ref, sh_ref, *, rows):
    i = pl.program_id(1)
    tm = h_ref.shape[0]
    sh_ref[0, 0:CONV_HALO, :] = jnp.where(i > 0, hp_ref[...], 0.0)
    sh_ref[0, CONV_HALO:CONV_HALO + tm, :] = h_ref[...]
    sh_ref[0, CONV_HALO + tm:, :] = jnp.where(i < pl.num_programs(1) - 1, hn_ref[...], 0.0)
    first = CONV_HALO - CONV_K // 2
    used = tm + 2 * CONV_HALO - SUBLANES
    for s in range(1, SUBLANES):
        sh_ref[s, 0:used, :] = sh_ref[0, s:s + used, :]
    for r0 in range(0, tm, rows):
        acc = jnp.zeros((rows, CONV_CH), F32)
        for j in range(CONV_K):
            shift, base = (first + j) % SUBLANES, (first + j) // SUBLANES * SUBLANES
            acc += sh_ref[shift, r0 + base:r0 + base + rows, :] * w_ref[j:j + 1, :]
        acc += b_ref[...]
        xc = acc - jnp.mean(acc, axis=-1, keepdims=True)
        y = xc * lax.rsqrt(jnp.mean(xc * xc, axis=-1, keepdims=True) + NORM_EPS)
        y = y * g_ref[...] + be_ref[...]
        o_ref[r0:r0 + rows, :] = (y * _sigmoid(y)).astype(BF16)


def _conformer_conv(hg, p):
    bsz, length, ch = hg.shape
    tm = _tile(length, 256)
    nh = tm // CONV_HALO
    last = length // CONV_HALO - 1
    return pl.pallas_call(
        functools.partial(_conv_kernel, rows=32),
        grid=(bsz, length // tm),
        in_specs=[pl.BlockSpec((None, tm, ch), lambda b, i: (b, i, 0)),
                  pl.BlockSpec((None, CONV_HALO, ch), lambda b, i: (b, jnp.maximum(i * nh - 1, 0), 0)),
                  pl.BlockSpec((None, CONV_HALO, ch), lambda b, i: (b, jnp.minimum((i + 1) * nh, last), 0)),
                  _const_spec((CONV_K, ch)), _const_spec((1, ch)), _const_spec((1, ch)), _const_spec((1, ch))],
        out_specs=pl.BlockSpec((None, tm, ch), lambda b, i: (b, i, 0)),
        out_shape=jax.ShapeDtypeStruct((bsz, length, ch), BF16),
        scratch_shapes=[pltpu.VMEM((SUBLANES, tm + 2 * CONV_HALO, ch), F32)],
        compiler_params=_params("parallel", "parallel"),
        name="conformer_conv",
    )(hg, hg, hg, p["dw_w"], p["dw_b"], p["cn_g"], p["cn_b"])


def _mix_out_kernel(x_ref, m_ref, a_ref, c_ref, wa_ref, wc_ref, o_ref):
    out = (jnp.dot(a_ref[...], wa_ref[...], preferred_element_type=F32)
           + jnp.dot(c_ref[...], wc_ref[...], preferred_element_type=F32))
    o_ref[...] = x_ref[...] + m_ref[5:6, :] * out


def _mix_out(x, mods, att, conv, p):
    bsz, length, d = x.shape
    tm = _tile(length, 512)
    tok = lambda w: pl.BlockSpec((None, tm, w), lambda b, i: (b, i, 0))
    return pl.pallas_call(
        _mix_out_kernel,
        grid=(bsz, length // tm),
        in_specs=[tok(d), _mod_spec(mods), tok(att.shape[-1]), tok(conv.shape[-1]),
                  _const_spec(p["w_out_att"].shape), _const_spec(p["w_out_conv"].shape)],
        out_specs=tok(d),
        out_shape=jax.ShapeDtypeStruct(x.shape, F32),
        compiler_params=_params("parallel", "parallel"),
        name="mix_out",
    )(x, mods, att, conv, p["w_out_att"], p["w_out_conv"])


def _rwkv_in_kernel(x_ref, xp_ref, xn_ref, m_ref, xmix_ref, wr_ref, wk_ref, wv_ref, w0_ref, w1_ref, w2_ref,
                    a0_ref, a1_ref, a2_ref, g1_ref, g2_ref, kk_ref, ka_ref, rk_ref,
                    r_out, v_out, kkn_out, g_out, bonus_out, lw_out, kd_out, b_out):
    i = pl.program_id(1)
    tm = x_ref.shape[0]
    h = _modulate(x_ref[...], m_ref, 3)
    h_before = jnp.where(i > 0, _modulate(xp_ref[7:8, :], m_ref, 3), 0.0)
    h_after = jnp.where(i < pl.num_programs(1) - 1, _modulate(xn_ref[0:1, :], m_ref, 3), 0.0)
    row = lax.broadcasted_iota(jnp.int32, (tm, 1), 0)
    h_prev = jnp.where(row == 0, h_before, pltpu.roll(h, 1, 0))
    h_next = jnp.where(row == tm - 1, h_after, pltpu.roll(h, tm - 1, 0))
    xx = 0.5 * (h_prev + h_next) - h
    xr, xw, xk, xv, xa, xg = (h + xx * xmix_ref[j:j + 1, :] for j in range(6))
    r = _dot(xr, wr_ref[...])
    k = _dot(xk, wk_ref[...])
    v = _dot(xv, wv_ref[...])
    kk = k * kk_ref[...]
    kk = kk * (1.0 / jnp.maximum(jnp.sqrt(_head_sums(kk * kk)), 1e-12))
    g_out[...] = _dot(_sigmoid(_dot(xg, g1_ref[...])), g2_ref[...]).astype(BF16)
    k_sum = None
    for d in range(2):
        w_pre = w0_ref[d:d + 1, :] + _dot(jnp.tanh(_dot(xw, w1_ref[d])), w2_ref[d])
        lw_out[d] = -DECAY_SCALE * _sigmoid(w_pre)
        a = _sigmoid(a0_ref[d:d + 1, :] + _dot(_dot(xa, a1_ref[d]), a2_ref[d]))
        k_d = k * (1.0 + (a - 1.0) * ka_ref[...])
        kd_out[d] = k_d.astype(BF16)
        b_out[d] = (kk * a).astype(BF16)
        k_sum = k_d if k_sum is None else k_sum + k_d
    r_out[...] = r.astype(BF16)
    v_out[...] = v.astype(BF16)
    kkn_out[...] = kk.astype(BF16)
    bonus_out[...] = (_head_sums(r * (0.5 * k_sum) * rk_ref[...]) * v).astype(BF16)


def _rwkv_in(x, mods, p):
    bsz, length, d = x.shape
    tm = _tile(length, 256)
    nb = tm // 8
    last = length // 8 - 1
    tok = pl.BlockSpec((None, tm, d), lambda b, i: (b, i, 0))
    tok2 = pl.BlockSpec((None, 2, tm, d), lambda b, i: (b, 0, i, 0))
    consts = [p[n] for n in ("x_mix", "w_r", "w_k", "w_v", "w0", "w1", "w2", "a0", "a1", "a2", "g1", "g2",
                             "k_k", "k_a", "r_k")]
    one = jax.ShapeDtypeStruct((bsz, length, d), BF16)
    two = jax.ShapeDtypeStruct((bsz, 2, length, d), BF16)
    log_decay = jax.ShapeDtypeStruct((bsz, 2, length, d), F32)
    return pl.pallas_call(
        _rwkv_in_kernel,
        grid=(bsz, length // tm),
        in_specs=([tok,
                   pl.BlockSpec((None, 8, d), lambda b, i: (b, jnp.maximum(i * nb - 1, 0), 0)),
                   pl.BlockSpec((None, 8, d), lambda b, i: (b, jnp.minimum((i + 1) * nb, last), 0)),
                   _mod_spec(mods)] + [_const_spec(c.shape) for c in consts]),
        out_specs=[tok, tok, tok, tok, tok, tok2, tok2, tok2],
        out_shape=[one, one, one, one, one, log_decay, two, two],
        compiler_params=_params("parallel", "parallel"),
        name="rwkv_in",
    )(x, x, x, mods, *consts)


def _scan_kernel(r_ref, v_ref, kk_ref, lw_ref, kd_ref, b_ref, s0_ref, y_ref, st_ref, state_ref):
    d = pl.program_id(1)
    c = pl.program_id(2)
    n_rows, cs = lw_ref.shape[0], lw_ref.shape[1]
    n_pairs = state_ref.shape[1]

    @pl.when(c == 0)
    def _():
        state_ref[...] = s0_ref[...]

    rev = d == 1
    sign = 1 - 2 * d
    ti = lax.broadcasted_iota(jnp.int32, (cs, cs), 0)
    si = lax.broadcasted_iota(jnp.int32, (cs, cs), 1)
    incl = jnp.where((ti - si) * sign >= 0, 1.0, 0.0).astype(BF16)

    def scaled_streams(i):
        lw = lw_ref[i]
        l1 = lw.astype(BF16)
        rem = lw - l1.astype(F32)
        l2 = rem.astype(BF16)
        l3 = (rem - l2.astype(F32)).astype(BF16)
        cum = (jnp.dot(incl, l1, preferred_element_type=F32) + jnp.dot(incl, l2, preferred_element_type=F32)
               + jnp.dot(incl, l3, preferred_element_type=F32))
        total = jnp.where(rev, cum[0:1, :], cum[cs - 1:cs, :])
        p_inv = jnp.exp(-cum)
        to_end = jnp.exp(total - cum)
        kd = kd_ref[i]
        bv = b_ref[i]
        return dict(r_t=r_ref[i] * jnp.exp(cum), a_t=-kk_ref[i] * jnp.exp(cum - lw), k_t=kd * p_inv,
                    b_t=bv * p_inv, k_e=kd * to_end, b_e=bv * to_end, v=v_ref[i], p_tot=jnp.exp(total))

    rows = [scaled_streams(i) for i in range(n_rows)]

    assert cs & (cs - 1) == 0
    t2 = lax.broadcasted_iota(jnp.int32, (2 * cs, 2 * cs), 0) & (cs - 1)
    s2 = lax.broadcasted_iota(jnp.int32, (2 * cs, 2 * cs), 1) & (cs - 1)
    strict2 = (t2 - s2) * sign > 0
    incl2 = (t2 - s2) * sign >= 0
    even = lax.broadcasted_iota(jnp.int32, (cs, LANES), 1) < RWKV_HEAD
    vi = lax.broadcasted_iota(jnp.int32, (LANES, LANES), 0) < RWKV_HEAD
    ki = lax.broadcasted_iota(jnp.int32, (LANES, LANES), 1) < RWKV_HEAD
    same_head = vi == ki

    def stack(t):
        return jnp.concatenate([jnp.where(even, t, 0.0), jnp.where(even, 0.0, t)], axis=0)

    def twice(t):
        return jnp.concatenate([t, t], axis=0)

    bf = lambda t: t.astype(BF16)
    mm = lambda a, b: jnp.dot(a, b, preferred_element_type=F32)
    chains = [(i, p, slice(p * LANES, (p + 1) * LANES)) for i in range(n_rows) for p in range(n_pairs)]
    idx = range(len(chains))
    col = lambda name: [rows[i][name][:, sl] for i, _, sl in chains]
    a_t, r_t, b_t, k_t, b_e, k_e, v, p_tot = (col(n) for n in ("a_t", "r_t", "b_t", "k_t", "b_e", "k_e", "v", "p_tot"))
    eye = jnp.where(lax.broadcasted_iota(jnp.int32, (2 * cs, 2 * cs), 0)
                    == lax.broadcasted_iota(jnp.int32, (2 * cs, 2 * cs), 1), 1.0, 0.0)

    cross = [_dot_nt(jnp.concatenate([stack(a_t[n]), stack(r_t[n])], axis=0),
                     jnp.concatenate([stack(b_t[n]), stack(k_t[n])], axis=0)) for n in idx]
    m_ab = [jnp.where(strict2, x[:2 * cs, :2 * cs], 0.0) for x in cross]
    m_ak = [bf(jnp.where(strict2, x[:2 * cs, 2 * cs:], 0.0)) for x in cross]
    m_rb = [bf(jnp.where(incl2, x[2 * cs:, :2 * cs], 0.0)) for x in cross]
    m_rk = [bf(jnp.where(incl2, x[2 * cs:, 2 * cs:], 0.0)) for x in cross]
    v2 = [bf(twice(t)) for t in v]
    x_v = [mm(m_ak[n], v2[n]) for n in idx]
    y_v = [mm(m_rk[n], v2[n]) for n in idx]
    upd_v = [_dot_tn(v2[n], stack(k_e[n])) for n in idx]
    pw = [bf(m) for m in m_ab]
    t_inv = [eye + m for m in m_ab]
    for _ in range(max(1, (cs - 1).bit_length()) - 1):
        pw = [bf(mm(w, w)) for w in pw]
        t_inv = [t_inv[n] + mm(pw[n], bf(t_inv[n])) for n in idx]
    t_inv = [bf(t) for t in t_inv]

    states = [state_ref[i, p] for i, p, _ in chains]
    ar = [_dot_nt(jnp.concatenate([a_t[n], r_t[n]], axis=0), states[n]) for n in idx]
    u = [bf(mm(t_inv[n], bf(twice(ar[n][:cs]) + x_v[n]))) for n in idx]
    for n, (i, p, sl) in enumerate(chains):
        y2 = twice(ar[n][cs:]) + y_v[n] + mm(m_rb[n], u[n])
        y_ref[i, :, sl] = jnp.where(even, y2[:cs], y2[cs:])
        upd = upd_v[n] + _dot_tn(u[n], stack(b_e[n]))
        state_ref[i, p] = states[n] * p_tot[n] + jnp.where(same_head, upd, 0.0)

    @pl.when(c == pl.num_programs(2) - 1)
    def _():
        st_ref[...] = state_ref[...]


def _wkv_scan(r, v, kk, lw, kd, bv, s0):
    bsz, length, d = r.shape
    cs = SCAN_CHUNK
    nc = length // cs
    n_pairs = d // LANES
    nr = SCAN_ROWS if bsz % SCAN_ROWS == 0 else 1
    chunk = lambda d_, c: c + d_ * (nc - 1 - 2 * c)
    shared = pl.BlockSpec((nr, cs, d), lambda b, d_, c: (b, chunk(d_, c), 0))
    per_dir = pl.BlockSpec((nr, None, cs, d), lambda b, d_, c: (b, d_, chunk(d_, c), 0))
    st_spec = pl.BlockSpec((nr, None, n_pairs, LANES, LANES), lambda b, d_, c: (b, d_, 0, 0, 0))
    return pl.pallas_call(
        _scan_kernel,
        grid=(bsz // nr, 2, nc),
        in_specs=[shared, shared, shared, per_dir, per_dir, per_dir, st_spec],
        out_specs=[per_dir, st_spec],
        out_shape=[jax.ShapeDtypeStruct((bsz, 2, length, d), F32),
                   jax.ShapeDtypeStruct(s0.shape, F32)],
        scratch_shapes=[pltpu.VMEM((nr, n_pairs, LANES, LANES), F32)],
        compiler_params=_params("parallel", "parallel", "arbitrary"),
        name="wkv_scan",
    )(r, v, kk, lw, kd, bv, s0)


def _rwkv_out_kernel(x_ref, m_ref, y_ref, bonus_ref, g_ref, lng_ref, lnb_ref, wo_ref, o_ref,
                     *, gn_eps):
    y = y_ref[0] + y_ref[1]
    inv_n = 1.0 / RWKV_HEAD
    mean = _head_sums(y) * inv_n
    yc = y - mean
    rstd = lax.rsqrt(_head_sums(yc * yc) * inv_n + gn_eps)
    yn = yc * rstd * lng_ref[...] + lnb_ref[...]
    out = _dot((yn + bonus_ref[...]) * g_ref[...], wo_ref[...])
    o_ref[...] = x_ref[...] + m_ref[5:6, :] * out


def _rwkv_out(x, mods, y, bonus, g, p):
    bsz, length, d = x.shape
    tm = _tile(length, 256)
    tok = pl.BlockSpec((None, tm, d), lambda b, i: (b, i, 0))
    consts = [p["ln_g"], p["ln_b"], p["w_o"]]
    return pl.pallas_call(
        functools.partial(_rwkv_out_kernel, gn_eps=RWKV_HEAD * 1e-5),
        grid=(bsz, length // tm),
        in_specs=([tok, _mod_spec(mods), pl.BlockSpec((None, 2, tm, d), lambda b, i: (b, 0, i, 0)), tok, tok]
                  + [_const_spec(c.shape) for c in consts]),
        out_specs=tok,
        out_shape=jax.ShapeDtypeStruct(x.shape, F32),
        compiler_params=_params("parallel", "parallel"),
        name="rwkv_out",
    )(x, mods, y, bonus, g, *consts)


def _head_slots(w, per_head, offset=0):
    k = w.shape[0]
    w = w.reshape(k, MLA_HEADS, per_head)
    w = jnp.pad(w, ((0, 0), (0, 0), (offset, LANES - per_head - offset)))
    return w.reshape(k, MLA_HEADS * LANES)


def _mla_params(w_in, q_norm, w_qb, kv_norm, w_kvb, qn_q, qn_k, dw_w, dw_b, cn_g, cn_b, w_out):
    d = w_in.shape[0]
    o_kr = MLA_Q_RANK + MLA_KV_RANK
    mla_in = o_kr + MLA_ROPE
    w_in_p = jnp.concatenate([w_in[:, :mla_in], jnp.zeros((d, LANES - MLA_ROPE), F32), w_in[:, mla_in:]], axis=1)
    kvb = w_kvb.reshape(MLA_KV_RANK, MLA_HEADS, MLA_NOPE + MLA_V)
    w_k = _head_slots(kvb[:, :, :MLA_NOPE].reshape(MLA_KV_RANK, -1), MLA_NOPE)
    w_v = _head_slots(kvb[:, :, MLA_NOPE:].reshape(MLA_KV_RANK, -1), MLA_V)
    v_one = jnp.tile(jnp.arange(LANES) == MLA_V, MLA_HEADS).astype(F32)[None, :]
    k_one = jnp.tile(jnp.arange(LANES) == SHIFT_LANE, MLA_HEADS).astype(BF16)[None, :]
    w_out_att = jnp.pad(w_out[:MLA_HEADS * MLA_V].reshape(MLA_HEADS, MLA_V, d), ((0, 0), (0, LANES - MLA_V), (0, 0)))
    place = jnp.pad(jnp.eye(MLA_ROPE, dtype=F32), ((0, LANES - MLA_ROPE), (MLA_NOPE, LANES - QK_DIM)))
    w_kr = jnp.tile(place, (1, MLA_HEADS))
    pad_gain = lambda g: jnp.tile(jnp.pad(g, (0, LANES - QK_DIM)), MLA_HEADS)[None, :]
    norm_bound = lambda g: math.sqrt(QK_DIM) * BOUND_SLACK * jnp.max(jnp.abs(g))
    mix = MLA_HEADS * MLA_V
    return dict(
        w_in=w_in_p.astype(BF16), q_norm=q_norm[None, :], w_q=_head_slots(w_qb, QK_DIM).astype(BF16),
        kv_norm=kv_norm[None, :], w_k=w_k.astype(BF16), w_kr=w_kr.astype(BF16), w_v=w_v.astype(BF16),
        gain_q=pad_gain(qn_q) * (ATTN_SCALE * LOG2_E), gain_k=pad_gain(qn_k), v_one=v_one, k_one=k_one,
        q_norm_bound=norm_bound(qn_q) * (ATTN_SCALE * LOG2_E), k_norm_bound=norm_bound(qn_k),
        dw_w=dw_w[:, 0, :], dw_b=dw_b[None, :], cn_g=cn_g[None, :], cn_b=cn_b[None, :],
        w_out_att=w_out_att.reshape(MLA_HEADS * LANES, d).astype(BF16), w_out_conv=w_out[mix:].astype(BF16))


def _rope_tables(length, with_rope):
    n_freq = MLA_ROPE // 4
    if not with_rope:
        return jnp.ones((length, LANES), F32), jnp.zeros((length, LANES), F32), jnp.zeros((length, LANES), F32)
    t = jnp.arange(length)
    inv = ROPE_THETA ** (-jnp.arange(n_freq, dtype=F32) / n_freq)
    ang_r = (t // GRID_W).astype(F32)[:, None] * inv
    ang_c = (t % GRID_W).astype(F32)[:, None] * inv
    zeros = jnp.zeros((length, n_freq), F32)

    def slot(nope_val, r_first, r_second, c_first, c_second):
        nope = jnp.full((length, MLA_NOPE), nope_val, F32)
        pad = jnp.full((length, LANES - QK_DIM), nope_val, F32)
        return jnp.concatenate([nope, r_first, r_second, c_first, c_second, pad], axis=1)

    cos_r, sin_r, cos_c, sin_c = jnp.cos(ang_r), jnp.sin(ang_r), jnp.cos(ang_c), jnp.sin(ang_c)
    return (slot(1.0, cos_r, cos_r, cos_c, cos_c),
            slot(0.0, -sin_r, zeros, -sin_c, zeros),
            slot(0.0, zeros, sin_r, zeros, sin_c))


def _rwkv_params(x_mix, w_r, w_k, w_v, w0, w1, w2, a0, a1, a2, g1, g2, k_k, k_a, r_k, ln_g, ln_b, w_o):
    d = w_r.shape[0]
    bf = lambda w: w.astype(BF16)
    return dict(x_mix=x_mix, w_r=bf(w_r), w_k=bf(w_k), w_v=bf(w_v), w0=w0, w1=bf(w1), w2=bf(w2), a0=a0,
                a1=bf(a1), a2=bf(a2), g1=bf(g1), g2=bf(g2), k_k=k_k[None, :], k_a=k_a[None, :],
                r_k=r_k.reshape(1, d), ln_g=ln_g[None, :], ln_b=ln_b[None, :], w_o=bf(w_o))


def _mla_conv_layer(x, ctx, m_l, m_c, p):
    length, lc = x.shape[1], ctx.shape[1]
    q_l, k_l, v_l, hg_l = _mla_in(x, m_l, p, _rope_tables(length, True))
    q_c, k_c, v_c, hg_c = _mla_in(ctx, m_c, p, _rope_tables(lc, False))
    att_l = _attention_latent(q_l, k_c, v_c, k_l, v_l, p["q_norm_bound"], p["k_norm_bound"])
    att_c = _attention_online(q_c, k_c, v_c)
    x = _mix_out(x, m_l, att_l, _conformer_conv(hg_l, p), p)
    ctx = _mix_out(ctx, m_c, att_c, _conformer_conv(hg_c, p), p)
    return x, ctx


def _rwkv_layer(x, ctx, m_l, m_c, p):
    bsz, _, d = x.shape
    r_c, v_c, kk_c, _, _, lw_c, kd_c, b_c = _rwkv_in(ctx, m_c, p)
    r_l, v_l, kk_l, g_l, bonus_l, lw_l, kd_l, b_l = _rwkv_in(x, m_l, p)
    s0 = jnp.zeros((bsz, 2, d // LANES, LANES, LANES), F32)
    _, s_ctx = _wkv_scan(r_c, v_c, kk_c, lw_c, kd_c, b_c, s0)
    y_l, _ = _wkv_scan(r_l, v_l, kk_l, lw_l, kd_l, b_l, s_ctx)
    return _rwkv_out(x, m_l, y_l, bonus_l, g_l, p)


def kernel(x, c, ctx, c_ctx, ada_w, ada_b, ffn_w1, ffn_w3, ffn_w2, mla_w_in, mla_q_norm, mla_w_qb, mla_kv_norm, mla_w_kvb, mla_qk_norm_q, mla_qk_norm_k, conv_dw_w, conv_dw_b, conv_norm_g, conv_norm_b, mix_w_out, rwkv_x_mix, rwkv_w_r, rwkv_w_k, rwkv_w_v, rwkv_w0, rwkv_w1, rwkv_w2, rwkv_a0, rwkv_a1, rwkv_a2, rwkv_g1, rwkv_g2, rwkv_k_k, rwkv_k_a, rwkv_r_k, rwkv_ln_g, rwkv_ln_b, rwkv_w_o):
    bsz = x.shape[0]
    depth = ada_w.shape[0]
    assert depth == 2, "layer 0 is the MLA/conv layer, layer 1 the (last) RWKV layer"
    cond = jnp.concatenate([c, c_ctx[None, :], jnp.zeros((16 - bsz - 1, c.shape[1]), F32)], axis=0)
    w1, w3, w2 = ffn_w1.astype(BF16), ffn_w3.astype(BF16), ffn_w2.astype(BF16)
    mla_p = _mla_params(mla_w_in[0], mla_q_norm[0], mla_w_qb[0], mla_kv_norm[0], mla_w_kvb[0], mla_qk_norm_q[0],
                        mla_qk_norm_k[0], conv_dw_w[0], conv_dw_b[0], conv_norm_g[0], conv_norm_b[0], mix_w_out[0])
    rwkv_p = _rwkv_params(rwkv_x_mix[0], rwkv_w_r[0], rwkv_w_k[0], rwkv_w_v[0], rwkv_w0[0], rwkv_w1[0],
                          rwkv_w2[0], rwkv_a0[0], rwkv_a1[0], rwkv_a2[0], rwkv_g1[0], rwkv_g2[0], rwkv_k_k[0],
                          rwkv_k_a[0], rwkv_r_k[0], rwkv_ln_g[0], rwkv_ln_b[0], rwkv_w_o[0])
    for i in range(depth):
        mods = _ada_mods(cond, ada_w[i], ada_b[i])
        m_l, m_c = mods[:bsz], mods[bsz:bsz + 1]
        x = _half_ffn(x, m_l, 0, w1[i, 0], w3[i, 0], w2[i, 0])
        ctx = _half_ffn(ctx, m_c, 0, w1[i, 0], w3[i, 0], w2[i, 0])
        if i == 0:
            x, ctx = _mla_conv_layer(x, ctx, m_l, m_c, mla_p)
            ctx = _half_ffn(ctx, m_c, 6, w1[i, 1], w3[i, 1], w2[i, 1])
        else:
            x = _rwkv_layer(x, ctx, m_l, m_c, rwkv_p)
        x = _half_ffn(x, m_l, 6, w1[i, 1], w3[i, 1], w2[i, 1])
    return x
```

```python
import functools
import math

import jax
import jax.numpy as jnp
from jax import lax
from jax.experimental import pallas as pl
from jax.experimental.pallas import tpu as pltpu

F32 = jnp.float32
BF16 = jnp.bfloat16

LANES = 128
SUBLANES = 8
MXU_TILE = 256
VMEM_LIMIT = 56 * 1024 * 1024

NORM_EPS = 1e-6
N_MOD = 9
GRID_W = 64
ROPE_THETA = 10000.0

MLA_HEADS = 8
MLA_NOPE = 64
MLA_ROPE = 32
MLA_V = 64
MLA_Q_RANK = 384
MLA_KV_RANK = 128
QK_DIM = MLA_NOPE + MLA_ROPE
ATTN_SCALE = QK_DIM ** -0.5
LOG2_E = math.log2(math.e)
ATTN_TQ = 512
ATTN_TK = 1024
ATTN_HEADS = 4
SHIFT_LANE = QK_DIM
BOUND_SLACK = 1.02
ATTN_BOUND_LIMIT = 40.0
CONV_CH = 512
CONV_K = 31
CONV_HALO = 16

RWKV_HEAD = 64
SCAN_CHUNK = 64
SCAN_ROWS = 2
DECAY_SCALE = math.exp(-0.5)


def _dot(a, b):
    return jnp.dot(a.astype(BF16), b.astype(BF16), preferred_element_type=F32)


def _dot_nt(a, b):
    return lax.dot_general(a.astype(BF16), b.astype(BF16), (((1,), (1,)), ((), ())),
                           preferred_element_type=F32)


def _dot_tn(a, b):
    return lax.dot_general(a.astype(BF16), b.astype(BF16), (((0,), (0,)), ((), ())),
                           preferred_element_type=F32)


def _head_sums(x):
    even = lax.broadcasted_iota(jnp.int32, (1, LANES), 1) < RWKV_HEAD
    out = []
    for j in range(0, x.shape[1], LANES):
        u = x[:, j:j + LANES]
        both = jnp.sum(u, axis=-1, keepdims=True)
        first = jnp.sum(jnp.where(even, u, 0.0), axis=-1, keepdims=True)
        out.append(jnp.where(even, first, both - first))
    return jnp.concatenate(out, axis=1)


def _rms(x):
    return x * lax.rsqrt(jnp.mean(x * x, axis=-1, keepdims=True) + NORM_EPS)


def _sigmoid(x):
    return 1.0 / (1.0 + jnp.exp(-x))


def _modulate(x, m_ref, first):
    return _rms(x) * (1.0 + m_ref[first + 1:first + 2, :]) + m_ref[first:first + 1, :]


def _const_spec(shape):
    zeros = (0,) * len(shape)
    return pl.BlockSpec(shape, lambda *_: zeros, pipeline_mode=pl.Buffered(1))


def _params(*sem):
    return pltpu.CompilerParams(dimension_semantics=sem, vmem_limit_bytes=VMEM_LIMIT)


def _mod_spec(mods):
    d = mods.shape[-1]
    if mods.shape[0] == 1:
        return pl.BlockSpec((None, N_MOD, d), lambda b, i: (0, 0, 0))
    return pl.BlockSpec((None, N_MOD, d), lambda b, i: (b, 0, 0))


def _tile(length, pref):
    t = min(pref, length)
    assert length % t == 0, (length, t)
    return t


def _ada_kernel(c_ref, w_ref, b_ref, o_ref):
    c = c_ref[...]
    s = c * _sigmoid(c)
    w = w_ref[...]
    s_hi = s.astype(BF16)
    s_lo = (s - s_hi.astype(F32)).astype(BF16)
    w_hi = w.astype(BF16)
    w_lo = (w - w_hi.astype(F32)).astype(BF16)
    acc = jnp.dot(s_hi, w_hi, preferred_element_type=F32)
    acc += jnp.dot(s_hi, w_lo, preferred_element_type=F32)
    acc += jnp.dot(s_lo, w_hi, preferred_element_type=F32)
    o_ref[...] = acc + b_ref[...]


def _ada_mods(cond, w, b):
    r, d = cond.shape
    n = w.shape[1]
    out = pl.pallas_call(
        _ada_kernel,
        grid=(n // d,),
        in_specs=[pl.BlockSpec((r, d), lambda j: (0, 0)),
                  pl.BlockSpec((d, d), lambda j: (0, j)),
                  pl.BlockSpec((1, d), lambda j: (0, j))],
        out_specs=pl.BlockSpec((r, d), lambda j: (0, j)),
        out_shape=jax.ShapeDtypeStruct((r, n), F32),
        compiler_params=_params("parallel"),
        name="ada_mods",
    )(cond, w, b.reshape(1, n))
    return out.reshape(r, n // d, d)


def _ffn_kernel(x_ref, m_ref, w1_ref, w3_ref, w2_ref, o_ref, *, first, f_split):
    x = x_ref[...]
    h = _modulate(x, m_ref, first).astype(BF16)
    acc = None
    for cols in (slice(0, f_split), slice(f_split, w1_ref.shape[1])):
        a = jnp.dot(h, w1_ref[:, cols], preferred_element_type=F32)
        b = jnp.dot(h, w3_ref[:, cols], preferred_element_type=F32)
        g = (a * _sigmoid(a) * b).astype(BF16)
        part = jnp.dot(g, w2_ref[cols, :], preferred_element_type=F32)
        acc = part if acc is None else acc + part
    o_ref[...] = x + (0.5 * m_ref[first + 2:first + 3, :]) * acc


def _half_ffn(x, mods, first, w1, w3, w2):
    bsz, length, d = x.shape
    f = w1.shape[1]
    tm = _tile(length, 512)
    assert f % MXU_TILE == 0
    f_split = (f // MXU_TILE + 1) // 2 * MXU_TILE
    return pl.pallas_call(
        functools.partial(_ffn_kernel, first=first, f_split=f_split),
        grid=(bsz, length // tm),
        in_specs=[pl.BlockSpec((None, tm, d), lambda b, i: (b, i, 0)),
                  _mod_spec(mods),
                  _const_spec((d, f)), _const_spec((d, f)), _const_spec((f, d))],
        out_specs=pl.BlockSpec((None, tm, d), lambda b, i: (b, i, 0)),
        out_shape=jax.ShapeDtypeStruct(x.shape, F32),
        compiler_params=_params("parallel", "parallel"),
        name="half_ffn",
    )(x, mods, w1, w3, w2)


def _head_rms(t, width):
    slots = [t[:, j:j + LANES] for j in range(0, t.shape[1], LANES)]
    return jnp.concatenate(
        [u * lax.rsqrt(jnp.sum(u * u, axis=-1, keepdims=True) * (1.0 / width) + NORM_EPS) for u in slots], axis=1)


def _rope(t, cos, sin_lo, sin_hi):
    n = t.shape[1]
    rep = n // cos.shape[1]
    cos, sin_lo, sin_hi = (jnp.tile(u, (1, rep)) for u in (cos, sin_lo, sin_hi))
    return t * cos + pltpu.roll(t, n - 8, 1) * sin_lo + pltpu.roll(t, 8, 1) * sin_hi


def _mla_in_kernel(x_ref, m_ref, w_in_ref, qn_ref, wq_ref, kvn_ref, wk_ref, wkr_ref, wv_ref,
                   gq_ref, gk_ref, vone_ref, kone_ref, cos_ref, sl_ref, sh_ref,
                   q_ref, k_ref, v_ref, hg_ref):
    h = _modulate(x_ref[...], m_ref, 3)
    z = _dot(h, w_in_ref[...])
    o_kv = MLA_Q_RANK
    o_kr = o_kv + MLA_KV_RANK
    o_a = o_kr + LANES
    o_g = o_a + CONV_CH
    c_q = _rms(z[:, :o_kv]) * qn_ref[...]
    c_kv = _rms(z[:, o_kv:o_kr]) * kvn_ref[...]
    hg_ref[...] = z[:, o_a:o_g] * _sigmoid(z[:, o_g:o_g + CONV_CH])
    cos, s_lo, s_hi = cos_ref[...], sl_ref[...], sh_ref[...]
    q = _head_rms(_dot(c_q, wq_ref[...]), QK_DIM) * gq_ref[...]
    q_ref[...] = _rope(q, cos, s_lo, s_hi).astype(BF16)
    k = _dot(c_kv, wk_ref[...]) + _dot(z[:, o_kr:o_a], wkr_ref[...])
    k = _head_rms(k, QK_DIM) * gk_ref[...]
    k_ref[...] = _rope(k, cos, s_lo, s_hi).astype(BF16) + kone_ref[...]
    v_ref[...] = (_dot(c_kv, wv_ref[...]) + vone_ref[...]).astype(BF16)


def _mla_in(x, mods, p, rope_tabs):
    bsz, length, d = x.shape
    tm = _tile(length, 512)
    hs = MLA_HEADS * LANES
    cos, s_lo, s_hi = rope_tabs
    tab_spec = pl.BlockSpec((tm, LANES), lambda b, i: (i, 0))
    tok = lambda w: pl.BlockSpec((None, tm, w), lambda b, i: (b, i, 0))
    consts = [p["w_in"], p["q_norm"], p["w_q"], p["kv_norm"], p["w_k"], p["w_kr"], p["w_v"],
              p["gain_q"], p["gain_k"], p["v_one"], p["k_one"]]
    return pl.pallas_call(
        _mla_in_kernel,
        grid=(bsz, length // tm),
        in_specs=([tok(d), _mod_spec(mods)] + [_const_spec(c.shape) for c in consts]
                  + [tab_spec, tab_spec, tab_spec]),
        out_specs=[tok(hs), tok(hs), tok(hs), tok(CONV_CH)],
        out_shape=[jax.ShapeDtypeStruct((bsz, length, hs), BF16),
                   jax.ShapeDtypeStruct((bsz, length, hs), BF16),
                   jax.ShapeDtypeStruct((bsz, length, hs), BF16),
                   jax.ShapeDtypeStruct((bsz, length, CONV_CH), F32)],
        compiler_params=_params("parallel", "parallel"),
        name="mla_in",
    )(x, mods, *consts, cos, s_lo, s_hi)


def _attn_kernel(q_ref, k_ref, v_ref, o_ref, *, tk):
    tq = q_ref.shape[0]
    n_chunks = k_ref.shape[0] // tk
    heads = [slice(j * LANES, (j + 1) * LANES) for j in range(q_ref.shape[1] // LANES)]
    qs = [q_ref[:, cols] for cols in heads]

    def rows(c):
        return pl.ds(pl.multiple_of(c * tk, tk), tk)

    def scores(c):
        return tuple(_dot_nt(q, k_ref[rows(c), cols]) for q, cols in zip(qs, heads))

    def consume(c, s_all, carry):
        out = []
        for s, cols, (m, acc) in zip(s_all, heads, carry):
            m_new = jnp.maximum(m, jnp.max(s, axis=-1, keepdims=True))
            p = jnp.exp2(s - m_new).astype(BF16)
            acc = jnp.exp2(m - m_new) * acc + jnp.dot(p, v_ref[rows(c), cols], preferred_element_type=F32)
            out.append((m_new, acc))
        return tuple(out)

    def body(c, carry):
        s_all, state = carry
        return scores(c + 1), consume(c, s_all, state)

    init = tuple((jnp.full((tq, 1), -jnp.inf, F32), jnp.zeros((tq, LANES), F32)) for _ in heads)
    s_last, state = lax.fori_loop(0, n_chunks - 1, body, (scores(0), init))
    for cols, (_, acc) in zip(heads, consume(n_chunks - 1, s_last, state)):
        o_ref[:, cols] = (acc * (1.0 / acc[:, MLA_V:MLA_V + 1])).astype(BF16)


def _attn_bounded_kernel(q_ref, kc_ref, vtc_ref, kl_ref, vtl_ref, kmax_ref, o_ref, *, tk):
    tq = q_ref.shape[0]
    heads = [slice(j * LANES, (j + 1) * LANES) for j in range(q_ref.shape[1] // LANES)]
    lane = lax.broadcasted_iota(jnp.int32, (tq, LANES), 1)
    qs = []
    for cols in heads:
        q = q_ref[:, cols].astype(F32)
        bound = jnp.sqrt(jnp.sum(q * q, axis=-1, keepdims=True)) * kmax_ref[:, cols]
        qs.append(jnp.where(lane == SHIFT_LANE, -bound, q).astype(BF16))

    v_rows = -(-(MLA_V + 1) // 16) * 16

    def add_keys(accs, k_ref, vt_ref, rows, chunk):
        out = []
        for q, cols, acc in zip(qs, heads, accs):
            p_t = jnp.exp2(_dot_nt(k_ref[rows, cols], q)).astype(BF16)
            v_t = vt_ref[chunk, cols.start:cols.start + v_rows, :]
            out.append(acc + jnp.dot(v_t, p_t, preferred_element_type=F32))
        return tuple(out)

    accs = add_keys(tuple(jnp.zeros((v_rows, tq), F32) for _ in heads), kc_ref, vtc_ref, slice(None), 0)
    accs = lax.fori_loop(0, kl_ref.shape[0] // tk,
                         lambda c, a: add_keys(a, kl_ref, vtl_ref, pl.ds(pl.multiple_of(c * tk, tk), tk), c), accs)
    pad = jnp.zeros((LANES - v_rows, tq), F32)
    for cols, acc_t in zip(heads, accs):
        out_t = jnp.concatenate([acc_t * (1.0 / acc_t[MLA_V:MLA_V + 1, :]), pad], axis=0)
        o_ref[:, cols] = out_t.T.astype(BF16)


def _attn_call(kernel, name, q, keys_values, extra=()):
    bsz, lq, hs = q.shape
    tq = _tile(lq, ATTN_TQ)
    group = ATTN_HEADS * LANES
    blk = lambda rows, imap: pl.BlockSpec((None, rows, group), imap)
    whole = lambda b, h, i: (b, 0, h)
    return pl.pallas_call(
        kernel,
        grid=(bsz, hs // group, lq // tq),
        in_specs=([blk(tq, lambda b, h, i: (b, i, h))] + [blk(a.shape[1], whole) for a in keys_values]
                  + [blk(1, whole) for _ in extra]),
        out_specs=blk(tq, lambda b, h, i: (b, i, h)),
        out_shape=jax.ShapeDtypeStruct((bsz, lq, hs), BF16),
        compiler_params=_params("parallel", "parallel", "parallel"),
        name=name,
    )(q, *keys_values, *extra)


def _attention_online(q, k, v):
    lk = k.shape[1]
    tk = next(t for t in (ATTN_TK, 512, 256, lk) if lk % t == 0)
    return _attn_call(functools.partial(_attn_kernel, tk=tk), "attention", q, (k, v))


def _attention_latent(q, k_c, v_c, k_l, v_l, q_norm_bound, k_norm_bound):
    tk = next(t for t in (ATTN_TK, 512, 256, k_l.shape[1]) if k_l.shape[1] % t == 0)
    k_max_slots = jnp.full((q.shape[0], 1, q.shape[-1]), k_norm_bound, F32)
    cat = lambda a, b: jnp.concatenate([a, b], axis=1)

    def bounded(q, k_c, v_c, k_l, v_l, k_max_slots):
        bsz, lq, hs = q.shape
        tq = _tile(lq, ATTN_TQ)
        group = ATTN_HEADS * LANES
        chunks_t = lambda v, t: v.reshape(bsz, v.shape[1] // t, t, hs).transpose(0, 1, 3, 2)
        vt_c, vt_l = chunks_t(v_c, v_c.shape[1]), chunks_t(v_l, tk)
        blk = lambda rows, imap: pl.BlockSpec((None, rows, group), imap)
        blk_t = lambda a: pl.BlockSpec((None, a.shape[1], group, a.shape[3]), lambda b, h, i: (b, 0, h, 0))
        whole = lambda b, h, i: (b, 0, h)
        return pl.pallas_call(
            functools.partial(_attn_bounded_kernel, tk=tk),
            grid=(bsz, hs // group, lq // tq),
            in_specs=[blk(tq, lambda b, h, i: (b, i, h)), blk(k_c.shape[1], whole), blk_t(vt_c),
                      blk(k_l.shape[1], whole), blk_t(vt_l), blk(1, whole)],
            out_specs=blk(tq, lambda b, h, i: (b, i, h)),
            out_shape=jax.ShapeDtypeStruct((bsz, lq, hs), BF16),
            compiler_params=_params("parallel", "parallel", "parallel"),
            name="attention_bounded",
        )(q, k_c, vt_c, k_l, vt_l, k_max_slots)

    def online(q, k_c, v_c, k_l, v_l, _):
        return _attention_online(q, cat(k_c, k_l), cat(v_c, v_l))

    return lax.cond(q_norm_bound * k_norm_bound <= ATTN_BOUND_LIMIT, bounded, online,
                    q, k_c, v_c, k_l, v_l, k_max_slots)


def _conv_kernel(h_ref, hp_ref, hn_ref, w_ref, b_ref, g_ref, be_ref, o_ref, sh_ref, *, rows):
    i = pl.program_id(1)
    tm = h_ref.shape[0]
    sh_ref[0, 0:CONV_HALO, :] = jnp.where(i > 0, hp_ref[...], 0.0)
    sh_ref[0, CONV_HALO:CONV_HALO + tm, :] = h_ref[...]
    sh_ref[0, CONV_HALO + tm:, :] = jnp.where(i < pl.num_programs(1) - 1, hn_ref[...], 0.0)
    first = CONV_HALO - CONV_K // 2
    used = tm + 2 * CONV_HALO - SUBLANES
    for s in range(1, SUBLANES):
        sh_ref[s, 0:used, :] = sh_ref[0, s:s + used, :]
    for r0 in range(0, tm, rows):
        acc = jnp.zeros((rows, CONV_CH), F32)
        for j in range(CONV_K):
            shift, base = (first + j) % SUBLANES, (first + j) // SUBLANES * SUBLANES
            acc += sh_ref[shift, r0 + base:r0 + base + rows, :] * w_ref[j:j + 1, :]
        acc += b_ref[...]
        xc = acc - jnp.mean(acc, axis=-1, keepdims=True)
        y = xc * lax.rsqrt(jnp.mean(xc * xc, axis=-1, keepdims=True) + NORM_EPS)
        y = y * g_ref[...] + be_ref[...]
        o_ref[r0:r0 + rows, :] = (y * _sigmoid(y)).astype(BF16)


def _conformer_conv(hg, p):
    bsz, length, ch = hg.shape
    tm = _tile(length, 256)
    nh = tm // CONV_HALO
    last = length // CONV_HALO - 1
    return pl.pallas_call(
        functools.partial(_conv_kernel, rows=32),
        grid=(bsz, length // tm),
        in_specs=[pl.BlockSpec((None, tm, ch), lambda b, i: (b, i, 0)),
                  pl.BlockSpec((None, CONV_HALO, ch), lambda b, i: (b, jnp.maximum(i * nh - 1, 0), 0)),
                  pl.BlockSpec((None, CONV_HALO, ch), lambda b, i: (b, jnp.minimum((i + 1) * nh, last), 0)),
                  _const_spec((CONV_K, ch)), _const_spec((1, ch)), _const_spec((1, ch)), _const_spec((1, ch))],
        out_specs=pl.BlockSpec((None, tm, ch), lambda b, i: (b, i, 0)),
        out_shape=jax.ShapeDtypeStruct((bsz, length, ch), BF16),
        scratch_shapes=[pltpu.VMEM((SUBLANES, tm + 2 * CONV_HALO, ch), F32)],
        compiler_params=_params("parallel", "parallel"),
        name="conformer_conv",
    )(hg, hg, hg, p["dw_w"], p["dw_b"], p["cn_g"], p["cn_b"])


def _mix_out_kernel(x_ref, m_ref, a_ref, c_ref, wa_ref, wc_ref, o_ref):
    out = (jnp.dot(a_ref[...], wa_ref[...], preferred_element_type=F32)
           + jnp.dot(c_ref[...], wc_ref[...], preferred_element_type=F32))
    o_ref[...] = x_ref[...] + m_ref[5:6, :] * out


def _mix_out(x, mods, att, conv, p):
    bsz, length, d = x.shape
    tm = _tile(length, 512)
    tok = lambda w: pl.BlockSpec((None, tm, w), lambda b, i: (b, i, 0))
    return pl.pallas_call(
        _mix_out_kernel,
        grid=(bsz, length // tm),
        in_specs=[tok(d), _mod_spec(mods), tok(att.shape[-1]), tok(conv.shape[-1]),
                  _const_spec(p["w_out_att"].shape), _const_spec(p["w_out_conv"].shape)],
        out_specs=tok(d),
        out_shape=jax.ShapeDtypeStruct(x.shape, F32),
        compiler_params=_params("parallel", "parallel"),
        name="mix_out",
    )(x, mods, att, conv, p["w_out_att"], p["w_out_conv"])


def _rwkv_in_kernel(x_ref, xp_ref, xn_ref, m_ref, xmix_ref, wr_ref, wk_ref, wv_ref, w0_ref, w1_ref, w2_ref,
                    a0_ref, a1_ref, a2_ref, g1_ref, g2_ref, kk_ref, ka_ref, rk_ref,
                    r_out, v_out, kkn_out, g_out, bonus_out, lw_out, kd_out, b_out):
    i = pl.program_id(1)
    tm = x_ref.shape[0]
    h = _modulate(x_ref[...], m_ref, 3)
    h_before = jnp.where(i > 0, _modulate(xp_ref[7:8, :], m_ref, 3), 0.0)
    h_after = jnp.where(i < pl.num_programs(1) - 1, _modulate(xn_ref[0:1, :], m_ref, 3), 0.0)
    row = lax.broadcasted_iota(jnp.int32, (tm, 1), 0)
    h_prev = jnp.where(row == 0, h_before, pltpu.roll(h, 1, 0))
    h_next = jnp.where(row == tm - 1, h_after, pltpu.roll(h, tm - 1, 0))
    xx = 0.5 * (h_prev + h_next) - h
    xr, xw, xk, xv, xa, xg = (h + xx * xmix_ref[j:j + 1, :] for j in range(6))
    r = _dot(xr, wr_ref[...])
    k = _dot(xk, wk_ref[...])
    v = _dot(xv, wv_ref[...])
    kk = k * kk_ref[...]
    kk = kk * (1.0 / jnp.maximum(jnp.sqrt(_head_sums(kk * kk)), 1e-12))
    g_out[...] = _dot(_sigmoid(_dot(xg, g1_ref[...])), g2_ref[...]).astype(BF16)
    k_sum = None
    for d in range(2):
        w_pre = w0_ref[d:d + 1, :] + _dot(jnp.tanh(_dot(xw, w1_ref[d])), w2_ref[d])
        lw_out[d] = -DECAY_SCALE * _sigmoid(w_pre)
        a = _sigmoid(a0_ref[d:d + 1, :] + _dot(_dot(xa, a1_ref[d]), a2_ref[d]))
        k_d = k * (1.0 + (a - 1.0) * ka_ref[...])
        kd_out[d] = k_d.astype(BF16)
        b_out[d] = (kk * a).astype(BF16)
        k_sum = k_d if k_sum is None else k_sum + k_d
    r_out[...] = r.astype(BF16)
    v_out[...] = v.astype(BF16)
    kkn_out[...] = kk.astype(BF16)
    bonus_out[...] = (_head_sums(r * (0.5 * k_sum) * rk_ref[...]) * v).astype(BF16)


def _rwkv_in(x, mods, p):
    bsz, length, d = x.shape
    tm = _tile(length, 256)
    nb = tm // 8
    last = length // 8 - 1
    tok = pl.BlockSpec((None, tm, d), lambda b, i: (b, i, 0))
    tok2 = pl.BlockSpec((None, 2, tm, d), lambda b, i: (b, 0, i, 0))
    consts = [p[n] for n in ("x_mix", "w_r", "w_k", "w_v", "w0", "w1", "w2", "a0", "a1", "a2", "g1", "g2",
                             "k_k", "k_a", "r_k")]
    one = jax.ShapeDtypeStruct((bsz, length, d), BF16)
    two = jax.ShapeDtypeStruct((bsz, 2, length, d), BF16)
    log_decay = jax.ShapeDtypeStruct((bsz, 2, length, d), F32)
    return pl.pallas_call(
        _rwkv_in_kernel,
        grid=(bsz, length // tm),
        in_specs=([tok,
                   pl.BlockSpec((None, 8, d), lambda b, i: (b, jnp.maximum(i * nb - 1, 0), 0)),
                   pl.BlockSpec((None, 8, d), lambda b, i: (b, jnp.minimum((i + 1) * nb, last), 0)),
                   _mod_spec(mods)] + [_const_spec(c.shape) for c in consts]),
        out_specs=[tok, tok, tok, tok, tok, tok2, tok2, tok2],
        out_shape=[one, one, one, one, one, log_decay, two, two],
        compiler_params=_params("parallel", "parallel"),
        name="rwkv_in",
    )(x, x, x, mods, *consts)


def _scan_kernel(r_ref, v_ref, kk_ref, lw_ref, kd_ref, b_ref, s0_ref, y_ref, st_ref, state_ref):
    d = pl.program_id(1)
    c = pl.program_id(2)
    n_rows, cs = lw_ref.shape[0], lw_ref.shape[1]
    n_pairs = state_ref.shape[1]

    @pl.when(c == 0)
    def _():
        state_ref[...] = s0_ref[...]

    rev = d == 1
    sign = 1 - 2 * d
    ti = lax.broadcasted_iota(jnp.int32, (cs, cs), 0)
    si = lax.broadcasted_iota(jnp.int32, (cs, cs), 1)
    incl = jnp.where((ti - si) * sign >= 0, 1.0, 0.0).astype(BF16)

    def scaled_streams(i):
        lw = lw_ref[i]
        l1 = lw.astype(BF16)
        rem = lw - l1.astype(F32)
        l2 = rem.astype(BF16)
        l3 = (rem - l2.astype(F32)).astype(BF16)
        cum = (jnp.dot(incl, l1, preferred_element_type=F32) + jnp.dot(incl, l2, preferred_element_type=F32)
               + jnp.dot(incl, l3, preferred_element_type=F32))
        total = jnp.where(rev, cum[0:1, :], cum[cs - 1:cs, :])
        p_inv = jnp.exp(-cum)
        to_end = jnp.exp(total - cum)
        kd = kd_ref[i]
        bv = b_ref[i]
        return dict(r_t=r_ref[i] * jnp.exp(cum), a_t=-kk_ref[i] * jnp.exp(cum - lw), k_t=kd * p_inv,
                    b_t=bv * p_inv, k_e=kd * to_end, b_e=bv * to_end, v=v_ref[i], p_tot=jnp.exp(total))

    rows = [scaled_streams(i) for i in range(n_rows)]

    assert cs & (cs - 1) == 0
    t2 = lax.broadcasted_iota(jnp.int32, (2 * cs, 2 * cs), 0) & (cs - 1)
    s2 = lax.broadcasted_iota(jnp.int32, (2 * cs, 2 * cs), 1) & (cs - 1)
    strict2 = (t2 - s2) * sign > 0
    incl2 = (t2 - s2) * sign >= 0
    even = lax.broadcasted_iota(jnp.int32, (cs, LANES), 1) < RWKV_HEAD
    vi = lax.broadcasted_iota(jnp.int32, (LANES, LANES), 0) < RWKV_HEAD
    ki = lax.broadcasted_iota(jnp.int32, (LANES, LANES), 1) < RWKV_HEAD
    same_head = vi == ki

    def stack(t):
        return jnp.concatenate([jnp.where(even, t, 0.0), jnp.where(even, 0.0, t)], axis=0)

    def twice(t):
        return jnp.concatenate([t, t], axis=0)

    bf = lambda t: t.astype(BF16)
    mm = lambda a, b: jnp.dot(a, b, preferred_element_type=F32)
    chains = [(i, p, slice(p * LANES, (p + 1) * LANES)) for i in range(n_rows) for p in range(n_pairs)]
    idx = range(len(chains))
    col = lambda name: [rows[i][name][:, sl] for i, _, sl in chains]
    a_t, r_t, b_t, k_t, b_e, k_e, v, p_tot = (col(n) for n in ("a_t", "r_t", "b_t", "k_t", "b_e", "k_e", "v", "p_tot"))
    eye = jnp.where(lax.broadcasted_iota(jnp.int32, (2 * cs, 2 * cs), 0)
                    == lax.broadcasted_iota(jnp.int32, (2 * cs, 2 * cs), 1), 1.0, 0.0)

    cross = [_dot_nt(jnp.concatenate([stack(a_t[n]), stack(r_t[n])], axis=0),
                     jnp.concatenate([stack(b_t[n]), stack(k_t[n])], axis=0)) for n in idx]
    m_ab = [jnp.where(strict2, x[:2 * cs, :2 * cs], 0.0) for x in cross]
    m_ak = [bf(jnp.where(strict2, x[:2 * cs, 2 * cs:], 0.0)) for x in cross]
    m_rb = [bf(jnp.where(incl2, x[2 * cs:, :2 * cs], 0.0)) for x in cross]
    m_rk = [bf(jnp.where(incl2, x[2 * cs:, 2 * cs:], 0.0)) for x in cross]
    v2 = [bf(twice(t)) for t in v]
    x_v = [mm(m_ak[n], v2[n]) for n in idx]
    y_v = [mm(m_rk[n], v2[n]) for n in idx]
    upd_v = [_dot_tn(v2[n], stack(k_e[n])) for n in idx]
    pw = [bf(m) for m in m_ab]
    t_inv = [eye + m for m in m_ab]
    for _ in range(max(1, (cs - 1).bit_length()) - 1):
        pw = [bf(mm(w, w)) for w in pw]
        t_inv = [t_inv[n] + mm(pw[n], bf(t_inv[n])) for n in idx]
    t_inv = [bf(t) for t in t_inv]

    states = [state_ref[i, p] for i, p, _ in chains]
    ar = [_dot_nt(jnp.concatenate([a_t[n], r_t[n]], axis=0), states[n]) for n in idx]
    u = [bf(mm(t_inv[n], bf(twice(ar[n][:cs]) + x_v[n]))) for n in idx]
    for n, (i, p, sl) in enumerate(chains):
        y2 = twice(ar[n][cs:]) + y_v[n] + mm(m_rb[n], u[n])
        y_ref[i, :, sl] = jnp.where(even, y2[:cs], y2[cs:])
        upd = upd_v[n] + _dot_tn(u[n], stack(b_e[n]))
        state_ref[i, p] = states[n] * p_tot[n] + jnp.where(same_head, upd, 0.0)

    @pl.when(c == pl.num_programs(2) - 1)
    def _():
        st_ref[...] = state_ref[...]


def _wkv_scan(r, v, kk, lw, kd, bv, s0):
    bsz, length, d = r.shape
    cs = SCAN_CHUNK
    nc = length // cs
    n_pairs = d // LANES
    nr = SCAN_ROWS if bsz % SCAN_ROWS == 0 else 1
    chunk = lambda d_, c: c + d_ * (nc - 1 - 2 * c)
    shared = pl.BlockSpec((nr, cs, d), lambda b, d_, c: (b, chunk(d_, c), 0))
    per_dir = pl.BlockSpec((nr, None, cs, d), lambda b, d_, c: (b, d_, chunk(d_, c), 0))
    st_spec = pl.BlockSpec((nr, None, n_pairs, LANES, LANES), lambda b, d_, c: (b, d_, 0, 0, 0))
    return pl.pallas_call(
        _scan_kernel,
        grid=(bsz // nr, 2, nc),
        in_specs=[shared, shared, shared, per_dir, per_dir, per_dir, st_spec],
        out_specs=[per_dir, st_spec],
        out_shape=[jax.ShapeDtypeStruct((bsz, 2, length, d), F32),
                   jax.ShapeDtypeStruct(s0.shape, F32)],
        scratch_shapes=[pltpu.VMEM((nr, n_pairs, LANES, LANES), F32)],
        compiler_params=_params("parallel", "parallel", "arbitrary"),
        name="wkv_scan",
    )(r, v, kk, lw, kd, bv, s0)


def _rwkv_out_kernel(x_ref, m_ref, y_ref, bonus_ref, g_ref, lng_ref, lnb_ref, wo_ref, o_ref,
                     *, gn_eps):
    y = y_ref[0] + y_ref[1]
    inv_n = 1.0 / RWKV_HEAD
    mean = _head_sums(y) * inv_n
    yc = y - mean
    rstd = lax.rsqrt(_head_sums(yc * yc) * inv_n + gn_eps)
    yn = yc * rstd * lng_ref[...] + lnb_ref[...]
    out = _dot((yn + bonus_ref[...]) * g_ref[...], wo_ref[...])
    o_ref[...] = x_ref[...] + m_ref[5:6, :] * out


def _rwkv_out(x, mods, y, bonus, g, p):
    bsz, length, d = x.shape
    tm = _tile(length, 256)
    tok = pl.BlockSpec((None, tm, d), lambda b, i: (b, i, 0))
    consts = [p["ln_g"], p["ln_b"], p["w_o"]]
    return pl.pallas_call(
        functools.partial(_rwkv_out_kernel, gn_eps=RWKV_HEAD * 1e-5),
        grid=(bsz, length // tm),
        in_specs=([tok, _mod_spec(mods), pl.BlockSpec((None, 2, tm, d), lambda b, i: (b, 0, i, 0)), tok, tok]
                  + [_const_spec(c.shape) for c in consts]),
        out_specs=tok,
        out_shape=jax.ShapeDtypeStruct(x.shape, F32),
        compiler_params=_params("parallel", "parallel"),
        name="rwkv_out",
    )(x, mods, y, bonus, g, *consts)


def _head_slots(w, per_head, offset=0):
    k = w.shape[0]
    w = w.reshape(k, MLA_HEADS, per_head)
    w = jnp.pad(w, ((0, 0), (0, 0), (offset, LANES - per_head - offset)))
    return w.reshape(k, MLA_HEADS * LANES)


def _mla_params(w_in, q_norm, w_qb, kv_norm, w_kvb, qn_q, qn_k, dw_w, dw_b, cn_g, cn_b, w_out):
    d = w_in.shape[0]
    o_kr = MLA_Q_RANK + MLA_KV_RANK
    mla_in = o_kr + MLA_ROPE
    w_in_p = jnp.concatenate([w_in[:, :mla_in], jnp.zeros((d, LANES - MLA_ROPE), F32), w_in[:, mla_in:]], axis=1)
    kvb = w_kvb.reshape(MLA_KV_RANK, MLA_HEADS, MLA_NOPE + MLA_V)
    w_k = _head_slots(kvb[:, :, :MLA_NOPE].reshape(MLA_KV_RANK, -1), MLA_NOPE)
    w_v = _head_slots(kvb[:, :, MLA_NOPE:].reshape(MLA_KV_RANK, -1), MLA_V)
    v_one = jnp.tile(jnp.arange(LANES) == MLA_V, MLA_HEADS).astype(F32)[None, :]
    k_one = jnp.tile(jnp.arange(LANES) == SHIFT_LANE, MLA_HEADS).astype(BF16)[None, :]
    w_out_att = jnp.pad(w_out[:MLA_HEADS * MLA_V].reshape(MLA_HEADS, MLA_V, d), ((0, 0), (0, LANES - MLA_V), (0, 0)))
    place = jnp.pad(jnp.eye(MLA_ROPE, dtype=F32), ((0, LANES - MLA_ROPE), (MLA_NOPE, LANES - QK_DIM)))
    w_kr = jnp.tile(place, (1, MLA_HEADS))
    pad_gain = lambda g: jnp.tile(jnp.pad(g, (0, LANES - QK_DIM)), MLA_HEADS)[None, :]
    norm_bound = lambda g: math.sqrt(QK_DIM) * BOUND_SLACK * jnp.max(jnp.abs(g))
    mix = MLA_HEADS * MLA_V
    return dict(
        w_in=w_in_p.astype(BF16), q_norm=q_norm[None, :], w_q=_head_slots(w_qb, QK_DIM).astype(BF16),
        kv_norm=kv_norm[None, :], w_k=w_k.astype(BF16), w_kr=w_kr.astype(BF16), w_v=w_v.astype(BF16),
        gain_q=pad_gain(qn_q) * (ATTN_SCALE * LOG2_E), gain_k=pad_gain(qn_k), v_one=v_one, k_one=k_one,
        q_norm_bound=norm_bound(qn_q) * (ATTN_SCALE * LOG2_E), k_norm_bound=norm_bound(qn_k),
        dw_w=dw_w[:, 0, :], dw_b=dw_b[None, :], cn_g=cn_g[None, :], cn_b=cn_b[None, :],
        w_out_att=w_out_att.reshape(MLA_HEADS * LANES, d).astype(BF16), w_out_conv=w_out[mix:].astype(BF16))


def _rope_tables(length, with_rope):
    n_freq = MLA_ROPE // 4
    if not with_rope:
        return jnp.ones((length, LANES), F32), jnp.zeros((length, LANES), F32), jnp.zeros((length, LANES), F32)
    t = jnp.arange(length)
    inv = ROPE_THETA ** (-jnp.arange(n_freq, dtype=F32) / n_freq)
    ang_r = (t // GRID_W).astype(F32)[:, None] * inv
    ang_c = (t % GRID_W).astype(F32)[:, None] * inv
    zeros = jnp.zeros((length, n_freq), F32)

    def slot(nope_val, r_first, r_second, c_first, c_second):
        nope = jnp.full((length, MLA_NOPE), nope_val, F32)
        pad = jnp.full((length, LANES - QK_DIM), nope_val, F32)
        return jnp.concatenate([nope, r_first, r_second, c_first, c_second, pad], axis=1)

    cos_r, sin_r, cos_c, sin_c = jnp.cos(ang_r), jnp.sin(ang_r), jnp.cos(ang_c), jnp.sin(ang_c)
    return (slot(1.0, cos_r, cos_r, cos_c, cos_c),
            slot(0.0, -sin_r, zeros, -sin_c, zeros),
            slot(0.0, zeros, sin_r, zeros, sin_c))


def _rwkv_params(x_mix, w_r, w_k, w_v, w0, w1, w2, a0, a1, a2, g1, g2, k_k, k_a, r_k, ln_g, ln_b, w_o):
    d = w_r.shape[0]
    bf = lambda w: w.astype(BF16)
    return dict(x_mix=x_mix, w_r=bf(w_r), w_k=bf(w_k), w_v=bf(w_v), w0=w0, w1=bf(w1), w2=bf(w2), a0=a0,
                a1=bf(a1), a2=bf(a2), g1=bf(g1), g2=bf(g2), k_k=k_k[None, :], k_a=k_a[None, :],
                r_k=r_k.reshape(1, d), ln_g=ln_g[None, :], ln_b=ln_b[None, :], w_o=bf(w_o))


def _mla_conv_layer(x, ctx, m_l, m_c, p):
    length, lc = x.shape[1], ctx.shape[1]
    q_l, k_l, v_l, hg_l = _mla_in(x, m_l, p, _rope_tables(length, True))
    q_c, k_c, v_c, hg_c = _mla_in(ctx, m_c, p, _rope_tables(lc, False))
    att_l = _attention_latent(q_l, k_c, v_c, k_l, v_l, p["q_norm_bound"], p["k_norm_bound"])
    att_c = _attention_online(q_c, k_c, v_c)
    x = _mix_out(x, m_l, att_l, _conformer_conv(hg_l, p), p)
    ctx = _mix_out(ctx, m_c, att_c, _conformer_conv(hg_c, p), p)
    return x, ctx


def _rwkv_layer(x, ctx, m_l, m_c, p):
    bsz, _, d = x.shape
    r_c, v_c, kk_c, _, _, lw_c, kd_c, b_c = _rwkv_in(ctx, m_c, p)
    r_l, v_l, kk_l, g_l, bonus_l, lw_l, kd_l, b_l = _rwkv_in(x, m_l, p)
    s0 = jnp.zeros((bsz, 2, d // LANES, LANES, LANES), F32)
    _, s_ctx = _wkv_scan(r_c, v_c, kk_c, lw_c, kd_c, b_c, s0)
    y_l, _ = _wkv_scan(r_l, v_l, kk_l, lw_l, kd_l, b_l, s_ctx)
    return _rwkv_out(x, m_l, y_l, bonus_l, g_l, p)


def kernel(x, c, ctx, c_ctx, ada_w, ada_b, ffn_w1, ffn_w3, ffn_w2, mla_w_in, mla_q_norm, mla_w_qb, mla_kv_norm, mla_w_kvb, mla_qk_norm_q, mla_qk_norm_k, conv_dw_w, conv_dw_b, conv_norm_g, conv_norm_b, mix_w_out, rwkv_x_mix, rwkv_w_r, rwkv_w_k, rwkv_w_v, rwkv_w0, rwkv_w1, rwkv_w2, rwkv_a0, rwkv_a1, rwkv_a2, rwkv_g1, rwkv_g2, rwkv_k_k, rwkv_k_a, rwkv_r_k, rwkv_ln_g, rwkv_ln_b, rwkv_w_o):
    bsz = x.shape[0]
    depth = ada_w.shape[0]
    assert depth == 2, "layer 0 is the MLA/conv layer, layer 1 the (last) RWKV layer"
    cond = jnp.concatenate([c, c_ctx[None, :], jnp.zeros((16 - bsz - 1, c.shape[1]), F32)], axis=0)
    w1, w3, w2 = ffn_w1.astype(BF16), ffn_w3.astype(BF16), ffn_w2.astype(BF16)
    mla_p = _mla_params(mla_w_in[0], mla_q_norm[0], mla_w_qb[0], mla_kv_norm[0], mla_w_kvb[0], mla_qk_norm_q[0],
                        mla_qk_norm_k[0], conv_dw_w[0], conv_dw_b[0], conv_norm_g[0], conv_norm_b[0], mix_w_out[0])
    rwkv_p = _rwkv_params(rwkv_x_mix[0], rwkv_w_r[0], rwkv_w_k[0], rwkv_w_v[0], rwkv_w0[0], rwkv_w1[0],
                          rwkv_w2[0], rwkv_a0[0], rwkv_a1[0], rwkv_a2[0], rwkv_g1[0], rwkv_g2[0], rwkv_k_k[0],
                          rwkv_k_a[0], rwkv_r_k[0], rwkv_ln_g[0], rwkv_ln_b[0], rwkv_w_o[0])
    for i in range(depth):
        mods = _ada_mods(cond, ada_w[i], ada_b[i])
        m_l, m_c = mods[:bsz], mods[bsz:bsz + 1]
        x = _half_ffn(x, m_l, 0, w1[i, 0], w3[i, 0], w2[i, 0])
        ctx = _half_ffn(ctx, m_c, 0, w1[i, 0], w3[i, 0], w2[i, 0])
        if i == 0:
            x, ctx = _mla_conv_layer(x, ctx, m_l, m_c, mla_p)
            ctx = _half_ffn(ctx, m_c, 6, w1[i, 1], w3[i, 1], w2[i, 1])
        else:
            x = _rwkv_layer(x, ctx, m_l, m_c, rwkv_p)
        x = _half_ffn(x, m_l, 6, w1[i, 1], w3[i, 1], w2[i, 1])
    return x
```

```python
import functools
import math

import jax
import jax.numpy as jnp
from jax import lax
from jax.experimental import pallas as pl
from jax.experimental.pallas import tpu as pltpu

F32 = jnp.float32
BF16 = jnp.bfloat16

LANES = 128
SUBLANES = 8
MXU_TILE = 256
VMEM_LIMIT = 56 * 1024 * 1024

NORM_EPS = 1e-6
N_MOD = 9
GRID_W = 64
ROPE_THETA = 10000.0

MLA_HEADS = 8
MLA_NOPE = 64
MLA_ROPE = 32
MLA_V = 64
MLA_Q_RANK = 384
MLA_KV_RANK = 128
QK_DIM = MLA_NOPE + MLA_ROPE
ATTN_SCALE = QK_DIM ** -0.5
LOG2_E = math.log2(math.e)
ATTN_TQ = 1024
ATTN_TK = 1024
ATTN_HEADS = 4
SHIFT_LANE = QK_DIM
BOUND_SLACK = 1.02
ATTN_BOUND_LIMIT = 40.0
CONV_CH = 512
CONV_K = 31
CONV_HALO = 16

RWKV_HEAD = 64
GN_EPS = RWKV_HEAD * 1e-5
SCAN_CHUNK = 64
SCAN_ROWS = 2
DECAY_SCALE = math.exp(-0.5)


def _dot(a, b):
    return jnp.dot(a.astype(BF16), b.astype(BF16), preferred_element_type=F32)


def _dot_nt(a, b):
    return lax.dot_general(a.astype(BF16), b.astype(BF16), (((1,), (1,)), ((), ())),
                           preferred_element_type=F32)


def _dot_tn(a, b):
    return lax.dot_general(a.astype(BF16), b.astype(BF16), (((0,), (0,)), ((), ())),
                           preferred_element_type=F32)


def _head_sums(x):
    even = lax.broadcasted_iota(jnp.int32, (1, LANES), 1) < RWKV_HEAD
    out = []
    for j in range(0, x.shape[1], LANES):
        u = x[:, j:j + LANES]
        both = jnp.sum(u, axis=-1, keepdims=True)
        first = jnp.sum(jnp.where(even, u, 0.0), axis=-1, keepdims=True)
        out.append(jnp.where(even, first, both - first))
    return jnp.concatenate(out, axis=1)


def _rms(x):
    return x * lax.rsqrt(jnp.mean(x * x, axis=-1, keepdims=True) + NORM_EPS)


def _sigmoid(x):
    return 1.0 / (1.0 + jnp.exp(-x))


def _modulate(x, m_ref, first):
    return _rms(x) * (1.0 + m_ref[first + 1:first + 2, :]) + m_ref[first:first + 1, :]


def _const_spec(shape):
    zeros = (0,) * len(shape)
    return pl.BlockSpec(shape, lambda *_: zeros, pipeline_mode=pl.Buffered(1))


def _params(*sem):
    return pltpu.CompilerParams(dimension_semantics=sem, vmem_limit_bytes=VMEM_LIMIT)


def _mod_spec(mods):
    d = mods.shape[-1]
    if mods.shape[0] == 1:
        return pl.BlockSpec((None, N_MOD, d), lambda b, i: (0, 0, 0))
    return pl.BlockSpec((None, N_MOD, d), lambda b, i: (b, 0, 0))


def _tile(length, pref):
    t = min(pref, length)
    assert length % t == 0, (length, t)
    return t


def _ada_kernel(c_ref, w_ref, b_ref, o_ref):
    c = c_ref[...]
    s = c * _sigmoid(c)
    w = w_ref[...]
    s_hi = s.astype(BF16)
    s_lo = (s - s_hi.astype(F32)).astype(BF16)
    w_hi = w.astype(BF16)
    w_lo = (w - w_hi.astype(F32)).astype(BF16)
    acc = jnp.dot(s_hi, w_hi, preferred_element_type=F32)
    acc += jnp.dot(s_hi, w_lo, preferred_element_type=F32)
    acc += jnp.dot(s_lo, w_hi, preferred_element_type=F32)
    o_ref[...] = acc + b_ref[...]


def _ada_mods(cond, w, b):
    r, d = cond.shape
    n = w.shape[1]
    out = pl.pallas_call(
        _ada_kernel,
        grid=(n // d,),
        in_specs=[pl.BlockSpec((r, d), lambda j: (0, 0)),
                  pl.BlockSpec((d, d), lambda j: (0, j)),
                  pl.BlockSpec((1, d), lambda j: (0, j))],
        out_specs=pl.BlockSpec((r, d), lambda j: (0, j)),
        out_shape=jax.ShapeDtypeStruct((r, n), F32),
        compiler_params=_params("parallel"),
        name="ada_mods",
    )(cond, w, b.reshape(1, n))
    return out.reshape(r, n // d, d)


def _ffn_kernel(x_ref, m_ref, *refs, first, f_split, delta, n_delta):
    delta_refs, (w1_ref, w3_ref, w2_ref, o_ref) = refs[:n_delta], refs[n_delta:]
    x = x_ref[...]
    if delta is not None:
        x = x + m_ref[5:6, :] * delta(*delta_refs)
    h = _modulate(x, m_ref, first).astype(BF16)
    acc = None
    for cols in (slice(0, f_split), slice(f_split, w1_ref.shape[1])):
        a = jnp.dot(h, w1_ref[:, cols], preferred_element_type=F32)
        b = jnp.dot(h, w3_ref[:, cols], preferred_element_type=F32)
        g = (a * _sigmoid(a) * b).astype(BF16)
        part = jnp.dot(g, w2_ref[cols, :], preferred_element_type=F32)
        acc = part if acc is None else acc + part
    o_ref[...] = x + (0.5 * m_ref[first + 2:first + 3, :]) * acc


def _half_ffn(x, mods, first, w1, w3, w2, delta=None, tokens=(), consts=()):
    bsz, length, d = x.shape
    f = w1.shape[1]
    tm = _tile(length, 512)
    assert f % MXU_TILE == 0
    f_split = (f // MXU_TILE + 1) // 2 * MXU_TILE

    def tok(a):
        if a.ndim == 3:
            return pl.BlockSpec((None, tm, a.shape[-1]), lambda b, i: (b, i, 0))
        return pl.BlockSpec((None, a.shape[1], tm, a.shape[-1]), lambda b, i: (b, 0, i, 0))

    return pl.pallas_call(
        functools.partial(_ffn_kernel, first=first, f_split=f_split, delta=delta, n_delta=len(tokens) + len(consts)),
        grid=(bsz, length // tm),
        in_specs=([tok(x), _mod_spec(mods)] + [tok(a) for a in tokens] + [_const_spec(c.shape) for c in consts]
                  + [_const_spec((d, f)), _const_spec((d, f)), _const_spec((f, d))]),
        out_specs=tok(x),
        out_shape=jax.ShapeDtypeStruct(x.shape, F32),
        compiler_params=_params("parallel", "parallel"),
        name="half_ffn",
    )(x, mods, *tokens, *consts, w1, w3, w2)


def _head_rms(t, width):
    slots = [t[:, j:j + LANES] for j in range(0, t.shape[1], LANES)]
    return jnp.concatenate(
        [u * lax.rsqrt(jnp.sum(u * u, axis=-1, keepdims=True) * (1.0 / width) + NORM_EPS) for u in slots], axis=1)


def _rope(t, cos, sin_lo, sin_hi):
    n = t.shape[1]
    rep = n // cos.shape[1]
    cos, sin_lo, sin_hi = (jnp.tile(u, (1, rep)) for u in (cos, sin_lo, sin_hi))
    return t * cos + pltpu.roll(t, n - 8, 1) * sin_lo + pltpu.roll(t, 8, 1) * sin_hi


def _mla_in_kernel(x_ref, m_ref, w_in_ref, qn_ref, wq_ref, kvn_ref, wk_ref, wkr_ref, wv_ref,
                   gq_ref, gk_ref, vone_ref, kone_ref, cos_ref, sl_ref, sh_ref,
                   q_ref, k_ref, v_ref, hg_ref):
    h = _modulate(x_ref[...], m_ref, 3)
    z = _dot(h, w_in_ref[...])
    o_kv = MLA_Q_RANK
    o_kr = o_kv + MLA_KV_RANK
    o_a = o_kr + LANES
    o_g = o_a + CONV_CH
    c_q = _rms(z[:, :o_kv]) * qn_ref[...]
    c_kv = _rms(z[:, o_kv:o_kr]) * kvn_ref[...]
    hg_ref[...] = z[:, o_a:o_g] * _sigmoid(z[:, o_g:o_g + CONV_CH])
    cos, s_lo, s_hi = cos_ref[...], sl_ref[...], sh_ref[...]
    q = _head_rms(_dot(c_q, wq_ref[...]), QK_DIM) * gq_ref[...]
    q_ref[...] = _rope(q, cos, s_lo, s_hi).astype(BF16)
    k = _dot(c_kv, wk_ref[...]) + _dot(z[:, o_kr:o_a], wkr_ref[...])
    k = _head_rms(k, QK_DIM) * gk_ref[...]
    k_ref[...] = _rope(k, cos, s_lo, s_hi).astype(BF16) + kone_ref[...]
    v_ref[...] = (_dot(c_kv, wv_ref[...]) + vone_ref[...]).astype(BF16)


def _mla_in(x, mods, p, rope_tabs):
    bsz, length, d = x.shape
    tm = _tile(length, 512)
    hs = MLA_HEADS * LANES
    cos, s_lo, s_hi = rope_tabs
    tab_spec = pl.BlockSpec((tm, LANES), lambda b, i: (i, 0))
    tok = lambda w: pl.BlockSpec((None, tm, w), lambda b, i: (b, i, 0))
    consts = [p["w_in"], p["q_norm"], p["w_q"], p["kv_norm"], p["w_k"], p["w_kr"], p["w_v"],
              p["gain_q"], p["gain_k"], p["v_one"], p["k_one"]]
    return pl.pallas_call(
        _mla_in_kernel,
        grid=(bsz, length // tm),
        in_specs=([tok(d), _mod_spec(mods)] + [_const_spec(c.shape) for c in consts]
                  + [tab_spec, tab_spec, tab_spec]),
        out_specs=[tok(hs), tok(hs), tok(hs), tok(CONV_CH)],
        out_shape=[jax.ShapeDtypeStruct((bsz, length, hs), BF16),
                   jax.ShapeDtypeStruct((bsz, length, hs), BF16),
                   jax.ShapeDtypeStruct((bsz, length, hs), BF16),
                   jax.ShapeDtypeStruct((bsz, length, CONV_CH), F32)],
        compiler_params=_params("parallel", "parallel"),
        name="mla_in",
    )(x, mods, *consts, cos, s_lo, s_hi)


def _attn_kernel(q_ref, k_ref, v_ref, o_ref, *, tk):
    tq = q_ref.shape[0]
    n_chunks = k_ref.shape[0] // tk
    heads = [slice(j * LANES, (j + 1) * LANES) for j in range(q_ref.shape[1] // LANES)]
    qs = [q_ref[:, cols] for cols in heads]

    def rows(c):
        return pl.ds(pl.multiple_of(c * tk, tk), tk)

    def scores(c):
        return tuple(_dot_nt(q, k_ref[rows(c), cols]) for q, cols in zip(qs, heads))

    def consume(c, s_all, carry):
        out = []
        for s, cols, (m, acc) in zip(s_all, heads, carry):
            m_new = jnp.maximum(m, jnp.max(s, axis=-1, keepdims=True))
            p = jnp.exp2(s - m_new).astype(BF16)
            acc = jnp.exp2(m - m_new) * acc + jnp.dot(p, v_ref[rows(c), cols], preferred_element_type=F32)
            out.append((m_new, acc))
        return tuple(out)

    def body(c, carry):
        s_all, state = carry
        return scores(c + 1), consume(c, s_all, state)

    init = tuple((jnp.full((tq, 1), -jnp.inf, F32), jnp.zeros((tq, LANES), F32)) for _ in heads)
    s_last, state = lax.fori_loop(0, n_chunks - 1, body, (scores(0), init))
    for cols, (_, acc) in zip(heads, consume(n_chunks - 1, s_last, state)):
        o_ref[:, cols] = (acc * (1.0 / acc[:, MLA_V:MLA_V + 1])).astype(BF16)


def _attn_bounded_kernel(q_ref, kc_ref, vtc_ref, kl_ref, vtl_ref, kmax_ref, o_ref, *, tk):
    tq = q_ref.shape[0]
    heads = [slice(j * LANES, (j + 1) * LANES) for j in range(q_ref.shape[1] // LANES)]
    lane = lax.broadcasted_iota(jnp.int32, (tq, LANES), 1)
    qs = []
    for cols in heads:
        q = q_ref[:, cols].astype(F32)
        bound = jnp.sqrt(jnp.sum(q * q, axis=-1, keepdims=True)) * kmax_ref[:, cols]
        qs.append(jnp.where(lane == SHIFT_LANE, -bound, q).astype(BF16))

    v_rows = -(-(MLA_V + 1) // 16) * 16

    def add_keys(accs, k_ref, vt_ref, rows, chunk):
        out = []
        for q, cols, acc in zip(qs, heads, accs):
            p_t = jnp.exp2(_dot_nt(k_ref[rows, cols], q)).astype(BF16)
            v_t = vt_ref[chunk, cols.start:cols.start + v_rows, :]
            out.append(acc + jnp.dot(v_t, p_t, preferred_element_type=F32))
        return tuple(out)

    accs = add_keys(tuple(jnp.zeros((v_rows, tq), F32) for _ in heads), kc_ref, vtc_ref, slice(None), 0)
    accs = lax.fori_loop(0, kl_ref.shape[0] // tk,
                         lambda c, a: add_keys(a, kl_ref, vtl_ref, pl.ds(pl.multiple_of(c * tk, tk), tk), c), accs)
    pad = jnp.zeros((LANES - v_rows, tq), F32)
    for cols, acc_t in zip(heads, accs):
        out_t = jnp.concatenate([acc_t * (1.0 / acc_t[MLA_V:MLA_V + 1, :]), pad], axis=0)
        o_ref[:, cols] = out_t.T.astype(BF16)


def _attn_call(kernel, name, q, keys_values, extra=()):
    bsz, lq, hs = q.shape
    tq = _tile(lq, ATTN_TQ)
    group = ATTN_HEADS * LANES
    blk = lambda rows, imap: pl.BlockSpec((None, rows, group), imap)
    whole = lambda b, h, i: (b, 0, h)
    return pl.pallas_call(
        kernel,
        grid=(bsz, hs // group, lq // tq),
        in_specs=([blk(tq, lambda b, h, i: (b, i, h))] + [blk(a.shape[1], whole) for a in keys_values]
                  + [blk(1, whole) for _ in extra]),
        out_specs=blk(tq, lambda b, h, i: (b, i, h)),
        out_shape=jax.ShapeDtypeStruct((bsz, lq, hs), BF16),
        compiler_params=_params("parallel", "parallel", "parallel"),
        name=name,
    )(q, *keys_values, *extra)


def _attention_online(q, k, v):
    lk = k.shape[1]
    tk = next(t for t in (ATTN_TK, 512, 256, lk) if lk % t == 0)
    return _attn_call(functools.partial(_attn_kernel, tk=tk), "attention", q, (k, v))


def _attention_latent(q, k_c, v_c, k_l, v_l, q_norm_bound, k_norm_bound):
    tk = next(t for t in (ATTN_TK, 512, 256, k_l.shape[1]) if k_l.shape[1] % t == 0)
    k_max_slots = jnp.full((q.shape[0], 1, q.shape[-1]), k_norm_bound, F32)
    cat = lambda a, b: jnp.concatenate([a, b], axis=1)

    def bounded(q, k_c, v_c, k_l, v_l, k_max_slots):
        bsz, lq, hs = q.shape
        tq = _tile(lq, ATTN_TQ)
        group = ATTN_HEADS * LANES
        chunks_t = lambda v, t: v.reshape(bsz, v.shape[1] // t, t, hs).transpose(0, 1, 3, 2)
        vt_c, vt_l = chunks_t(v_c, v_c.shape[1]), chunks_t(v_l, tk)
        blk = lambda rows, imap: pl.BlockSpec((None, rows, group), imap)
        blk_t = lambda a: pl.BlockSpec((None, a.shape[1], group, a.shape[3]), lambda b, h, i: (b, 0, h, 0),
                                       pipeline_mode=pl.Buffered(1))
        whole = lambda b, h, i: (b, 0, h)
        keys = lambda a: pl.BlockSpec((None, a.shape[1], group), whole, pipeline_mode=pl.Buffered(1))
        return pl.pallas_call(
            functools.partial(_attn_bounded_kernel, tk=tk),
            grid=(bsz, hs // group, lq // tq),
            in_specs=[blk(tq, lambda b, h, i: (b, i, h)), keys(k_c), blk_t(vt_c), keys(k_l), blk_t(vt_l),
                      blk(1, whole)],
            out_specs=blk(tq, lambda b, h, i: (b, i, h)),
            out_shape=jax.ShapeDtypeStruct((bsz, lq, hs), BF16),
            compiler_params=_params("parallel", "parallel", "parallel"),
            name="attention_bounded",
        )(q, k_c, vt_c, k_l, vt_l, k_max_slots)

    def online(q, k_c, v_c, k_l, v_l, _):
        return _attention_online(q, cat(k_c, k_l), cat(v_c, v_l))

    return lax.cond(q_norm_bound * k_norm_bound <= ATTN_BOUND_LIMIT, bounded, online,
                    q, k_c, v_c, k_l, v_l, k_max_slots)


def _conv_kernel(h_ref, hp_ref, hn_ref, w_ref, b_ref, g_ref, be_ref, o_ref, sh_ref, *, rows):
    i = pl.program_id(1)
    tm = h_ref.shape[0]
    sh_ref[0, 0:CONV_HALO, :] = jnp.where(i > 0, hp_ref[...], 0.0)
    sh_ref[0, CONV_HALO:CONV_HALO + tm, :] = h_ref[...]
    sh_ref[0, CONV_HALO + tm:, :] = jnp.where(i < pl.num_programs(1) - 1, hn_ref[...], 0.0)
    first = CONV_HALO - CONV_K // 2
    used = tm + 2 * CONV_HALO - SUBLANES
    for s in range(1, SUBLANES):
        sh_ref[s, 0:used, :] = sh_ref[0, s:s + used, :]
    for r0 in range(0, tm, rows):
        acc = jnp.zeros((rows, CONV_CH), F32)
        for j in range(CONV_K):
            shift, base = (first + j) % SUBLANES, (first + j) // SUBLANES * SUBLANES
            acc += sh_ref[shift, r0 + base:r0 + base + rows, :] * w_ref[j:j + 1, :]
        acc += b_ref[...]
        xc = acc - jnp.mean(acc, axis=-1, keepdims=True)
        y = xc * lax.rsqrt(jnp.mean(xc * xc, axis=-1, keepdims=True) + NORM_EPS)
        y = y * g_ref[...] + be_ref[...]
        o_ref[r0:r0 + rows, :] = (y * _sigmoid(y)).astype(BF16)


def _conformer_conv(hg, p):
    bsz, length, ch = hg.shape
    tm = _tile(length, 256)
    nh = tm // CONV_HALO
    last = length // CONV_HALO - 1
    return pl.pallas_call(
        functools.partial(_conv_kernel, rows=32),
        grid=(bsz, length // tm),
        in_specs=[pl.BlockSpec((None, tm, ch), lambda b, i: (b, i, 0)),
                  pl.BlockSpec((None, CONV_HALO, ch), lambda b, i: (b, jnp.maximum(i * nh - 1, 0), 0)),
                  pl.BlockSpec((None, CONV_HALO, ch), lambda b, i: (b, jnp.minimum((i + 1) * nh, last), 0)),
                  _const_spec((CONV_K, ch)), _const_spec((1, ch)), _const_spec((1, ch)), _const_spec((1, ch))],
        out_specs=pl.BlockSpec((None, tm, ch), lambda b, i: (b, i, 0)),
        out_shape=jax.ShapeDtypeStruct((bsz, length, ch), BF16),
        scratch_shapes=[pltpu.VMEM((SUBLANES, tm + 2 * CONV_HALO, ch), F32)],
        compiler_params=_params("parallel", "parallel"),
        name="conformer_conv",
    )(hg, hg, hg, p["dw_w"], p["dw_b"], p["cn_g"], p["cn_b"])


def _mix_delta(a_ref, c_ref, wa_ref, wc_ref):
    return (jnp.dot(a_ref[...], wa_ref[...], preferred_element_type=F32)
            + jnp.dot(c_ref[...], wc_ref[...], preferred_element_type=F32))


def _rwkv_in_kernel(x_ref, xp_ref, xn_ref, m_ref, xmix_ref, wr_ref, wk_ref, wv_ref, w0_ref, w1_ref, w2_ref,
                    a0_ref, a1_ref, a2_ref, g1_ref, g2_ref, kk_ref, ka_ref, rk_ref,
                    r_out, v_out, kkn_out, g_out, bonus_out, lw_out, kd_out, b_out):
    i = pl.program_id(1)
    tm = x_ref.shape[0]
    h = _modulate(x_ref[...], m_ref, 3)
    h_before = jnp.where(i > 0, _modulate(xp_ref[7:8, :], m_ref, 3), 0.0)
    h_after = jnp.where(i < pl.num_programs(1) - 1, _modulate(xn_ref[0:1, :], m_ref, 3), 0.0)
    row = lax.broadcasted_iota(jnp.int32, (tm, 1), 0)
    h_prev = jnp.where(row == 0, h_before, pltpu.roll(h, 1, 0))
    h_next = jnp.where(row == tm - 1, h_after, pltpu.roll(h, tm - 1, 0))
    xx = 0.5 * (h_prev + h_next) - h
    xr, xw, xk, xv, xa, xg = (h + xx * xmix_ref[j:j + 1, :] for j in range(6))
    r = _dot(xr, wr_ref[...])
    k = _dot(xk, wk_ref[...])
    v = _dot(xv, wv_ref[...])
    kk = k * kk_ref[...]
    kk = kk * (1.0 / jnp.maximum(jnp.sqrt(_head_sums(kk * kk)), 1e-12))
    g_out[...] = _dot(_sigmoid(_dot(xg, g1_ref[...])), g2_ref[...]).astype(BF16)
    k_sum = None
    for d in range(2):
        w_pre = w0_ref[d:d + 1, :] + _dot(jnp.tanh(_dot(xw, w1_ref[d])), w2_ref[d])
        lw_out[d] = -DECAY_SCALE * _sigmoid(w_pre)
        a = _sigmoid(a0_ref[d:d + 1, :] + _dot(_dot(xa, a1_ref[d]), a2_ref[d]))
        k_d = k * (1.0 + (a - 1.0) * ka_ref[...])
        kd_out[d] = k_d.astype(BF16)
        b_out[d] = (kk * a).astype(BF16)
        k_sum = k_d if k_sum is None else k_sum + k_d
    r_out[...] = r.astype(BF16)
    v_out[...] = v.astype(BF16)
    kkn_out[...] = kk.astype(BF16)
    bonus_out[...] = (_head_sums(r * (0.5 * k_sum) * rk_ref[...]) * v).astype(BF16)


def _rwkv_in(x, mods, p):
    bsz, length, d = x.shape
    tm = _tile(length, 256)
    nb = tm // 8
    last = length // 8 - 1
    tok = pl.BlockSpec((None, tm, d), lambda b, i: (b, i, 0))
    tok2 = pl.BlockSpec((None, 2, tm, d), lambda b, i: (b, 0, i, 0))
    consts = [p[n] for n in ("x_mix", "w_r", "w_k", "w_v", "w0", "w1", "w2", "a0", "a1", "a2", "g1", "g2",
                             "k_k", "k_a", "r_k")]
    one = jax.ShapeDtypeStruct((bsz, length, d), BF16)
    two = jax.ShapeDtypeStruct((bsz, 2, length, d), BF16)
    log_decay = jax.ShapeDtypeStruct((bsz, 2, length, d), F32)
    return pl.pallas_call(
        _rwkv_in_kernel,
        grid=(bsz, length // tm),
        in_specs=([tok,
                   pl.BlockSpec((None, 8, d), lambda b, i: (b, jnp.maximum(i * nb - 1, 0), 0)),
                   pl.BlockSpec((None, 8, d), lambda b, i: (b, jnp.minimum((i + 1) * nb, last), 0)),
                   _mod_spec(mods)] + [_const_spec(c.shape) for c in consts]),
        out_specs=[tok, tok, tok, tok, tok, tok2, tok2, tok2],
        out_shape=[one, one, one, one, one, log_decay, two, two],
        compiler_params=_params("parallel", "parallel"),
        name="rwkv_in",
    )(x, x, x, mods, *consts)


def _scan_kernel(r_ref, v_ref, kk_ref, lw_ref, kd_ref, b_ref, s0_ref, y_ref, st_ref, state_ref):
    d = pl.program_id(1)
    c = pl.program_id(2)
    n_rows, cs = lw_ref.shape[0], lw_ref.shape[1]
    n_pairs = state_ref.shape[1]

    @pl.when(c == 0)
    def _():
        state_ref[...] = s0_ref[...]

    rev = d == 1
    sign = 1 - 2 * d
    ti = lax.broadcasted_iota(jnp.int32, (cs, cs), 0)
    si = lax.broadcasted_iota(jnp.int32, (cs, cs), 1)
    incl = jnp.where((ti - si) * sign >= 0, 1.0, 0.0).astype(BF16)

    def scaled_streams(i):
        lw = lw_ref[i]
        l1 = lw.astype(BF16)
        rem = lw - l1.astype(F32)
        l2 = rem.astype(BF16)
        l3 = (rem - l2.astype(F32)).astype(BF16)
        cum = (jnp.dot(incl, l1, preferred_element_type=F32) + jnp.dot(incl, l2, preferred_element_type=F32)
               + jnp.dot(incl, l3, preferred_element_type=F32))
        total = jnp.where(rev, cum[0:1, :], cum[cs - 1:cs, :])
        p_inv = jnp.exp(-cum)
        to_end = jnp.exp(total - cum)
        kd = kd_ref[i]
        bv = b_ref[i]
        return dict(r_t=r_ref[i] * jnp.exp(cum), a_t=-kk_ref[i] * jnp.exp(cum - lw), k_t=kd * p_inv,
                    b_t=bv * p_inv, k_e=kd * to_end, b_e=bv * to_end, v=v_ref[i], p_tot=jnp.exp(total))

    rows = [scaled_streams(i) for i in range(n_rows)]

    assert cs & (cs - 1) == 0
    t2 = lax.broadcasted_iota(jnp.int32, (2 * cs, 2 * cs), 0) & (cs - 1)
    s2 = lax.broadcasted_iota(jnp.int32, (2 * cs, 2 * cs), 1) & (cs - 1)
    strict2 = (t2 - s2) * sign > 0
    incl2 = (t2 - s2) * sign >= 0
    even = lax.broadcasted_iota(jnp.int32, (cs, LANES), 1) < RWKV_HEAD
    vi = lax.broadcasted_iota(jnp.int32, (LANES, LANES), 0) < RWKV_HEAD
    ki = lax.broadcasted_iota(jnp.int32, (LANES, LANES), 1) < RWKV_HEAD
    same_head = vi == ki

    def stack(t):
        return jnp.concatenate([jnp.where(even, t, 0.0), jnp.where(even, 0.0, t)], axis=0)

    def twice(t):
        return jnp.concatenate([t, t], axis=0)

    bf = lambda t: t.astype(BF16)
    mm = lambda a, b: jnp.dot(a, b, preferred_element_type=F32)
    chains = [(i, p, slice(p * LANES, (p + 1) * LANES)) for i in range(n_rows) for p in range(n_pairs)]
    idx = range(len(chains))
    col = lambda name: [rows[i][name][:, sl] for i, _, sl in chains]
    a_t, r_t, b_t, k_t, b_e, k_e, v, p_tot = (col(n) for n in ("a_t", "r_t", "b_t", "k_t", "b_e", "k_e", "v", "p_tot"))
    eye = jnp.where(lax.broadcasted_iota(jnp.int32, (2 * cs, 2 * cs), 0)
                    == lax.broadcasted_iota(jnp.int32, (2 * cs, 2 * cs), 1), 1.0, 0.0)

    cross = [_dot_nt(jnp.concatenate([stack(a_t[n]), stack(r_t[n])], axis=0),
                     jnp.concatenate([stack(b_t[n]), stack(k_t[n])], axis=0)) for n in idx]
    m_ab = [jnp.where(strict2, x[:2 * cs, :2 * cs], 0.0) for x in cross]
    m_ak = [bf(jnp.where(strict2, x[:2 * cs, 2 * cs:], 0.0)) for x in cross]
    m_rb = [bf(jnp.where(incl2, x[2 * cs:, :2 * cs], 0.0)) for x in cross]
    m_rk = [bf(jnp.where(incl2, x[2 * cs:, 2 * cs:], 0.0)) for x in cross]
    v2 = [bf(twice(t)) for t in v]
    x_v = [mm(m_ak[n], v2[n]) for n in idx]
    y_v = [mm(m_rk[n], v2[n]) for n in idx]
    upd_v = [_dot_tn(v2[n], stack(k_e[n])) for n in idx]
    pw = [bf(m) for m in m_ab]
    t_inv = [eye + m for m in m_ab]
    for _ in range(max(1, (cs - 1).bit_length()) - 1):
        pw = [bf(mm(w, w)) for w in pw]
        t_inv = [t_inv[n] + mm(pw[n], bf(t_inv[n])) for n in idx]
    t_inv = [bf(t) for t in t_inv]

    states = [state_ref[i, p] for i, p, _ in chains]
    ar = [_dot_nt(jnp.concatenate([a_t[n], r_t[n]], axis=0), states[n]) for n in idx]
    u = [bf(mm(t_inv[n], bf(twice(ar[n][:cs]) + x_v[n]))) for n in idx]
    for n, (i, p, sl) in enumerate(chains):
        y2 = twice(ar[n][cs:]) + y_v[n] + mm(m_rb[n], u[n])
        y_ref[i, :, sl] = jnp.where(even, y2[:cs], y2[cs:])
        upd = upd_v[n] + _dot_tn(u[n], stack(b_e[n]))
        state_ref[i, p] = states[n] * p_tot[n] + jnp.where(same_head, upd, 0.0)

    @pl.when(c == pl.num_programs(2) - 1)
    def _():
        st_ref[...] = state_ref[...]


def _wkv_scan(r, v, kk, lw, kd, bv, s0):
    bsz, length, d = r.shape
    cs = SCAN_CHUNK
    nc = length // cs
    n_pairs = d // LANES
    nr = SCAN_ROWS if bsz % SCAN_ROWS == 0 else 1
    chunk = lambda d_, c: c + d_ * (nc - 1 - 2 * c)
    shared = pl.BlockSpec((nr, cs, d), lambda b, d_, c: (b, chunk(d_, c), 0))
    per_dir = pl.BlockSpec((nr, None, cs, d), lambda b, d_, c: (b, d_, chunk(d_, c), 0))
    st_spec = pl.BlockSpec((nr, None, n_pairs, LANES, LANES), lambda b, d_, c: (b, d_, 0, 0, 0))
    return pl.pallas_call(
        _scan_kernel,
        grid=(bsz // nr, 2, nc),
        in_specs=[shared, shared, shared, per_dir, per_dir, per_dir, st_spec],
        out_specs=[per_dir, st_spec],
        out_shape=[jax.ShapeDtypeStruct((bsz, 2, length, d), F32),
                   jax.ShapeDtypeStruct(s0.shape, F32)],
        scratch_shapes=[pltpu.VMEM((nr, n_pairs, LANES, LANES), F32)],
        compiler_params=_params("parallel", "parallel", "arbitrary"),
        name="wkv_scan",
    )(r, v, kk, lw, kd, bv, s0)


def _rwkv_delta(y_ref, bonus_ref, g_ref, lng_ref, lnb_ref, wo_ref):
    y = y_ref[0] + y_ref[1]
    inv_n = 1.0 / RWKV_HEAD
    yc = y - _head_sums(y) * inv_n
    rstd = lax.rsqrt(_head_sums(yc * yc) * inv_n + GN_EPS)
    yn = yc * rstd * lng_ref[...] + lnb_ref[...]
    return _dot((yn + bonus_ref[...]) * g_ref[...], wo_ref[...])


def _head_slots(w, per_head, offset=0):
    k = w.shape[0]
    w = w.reshape(k, MLA_HEADS, per_head)
    w = jnp.pad(w, ((0, 0), (0, 0), (offset, LANES - per_head - offset)))
    return w.reshape(k, MLA_HEADS * LANES)


def _mla_params(w_in, q_norm, w_qb, kv_norm, w_kvb, qn_q, qn_k, dw_w, dw_b, cn_g, cn_b, w_out):
    d = w_in.shape[0]
    o_kr = MLA_Q_RANK + MLA_KV_RANK
    mla_in = o_kr + MLA_ROPE
    w_in_p = jnp.concatenate([w_in[:, :mla_in], jnp.zeros((d, LANES - MLA_ROPE), F32), w_in[:, mla_in:]], axis=1)
    kvb = w_kvb.reshape(MLA_KV_RANK, MLA_HEADS, MLA_NOPE + MLA_V)
    w_k = _head_slots(kvb[:, :, :MLA_NOPE].reshape(MLA_KV_RANK, -1), MLA_NOPE)
    w_v = _head_slots(kvb[:, :, MLA_NOPE:].reshape(MLA_KV_RANK, -1), MLA_V)
    v_one = jnp.tile(jnp.arange(LANES) == MLA_V, MLA_HEADS).astype(F32)[None, :]
    k_one = jnp.tile(jnp.arange(LANES) == SHIFT_LANE, MLA_HEADS).astype(BF16)[None, :]
    w_out_att = jnp.pad(w_out[:MLA_HEADS * MLA_V].reshape(MLA_HEADS, MLA_V, d), ((0, 0), (0, LANES - MLA_V), (0, 0)))
    place = jnp.pad(jnp.eye(MLA_ROPE, dtype=F32), ((0, LANES - MLA_ROPE), (MLA_NOPE, LANES - QK_DIM)))
    w_kr = jnp.tile(place, (1, MLA_HEADS))
    pad_gain = lambda g: jnp.tile(jnp.pad(g, (0, LANES - QK_DIM)), MLA_HEADS)[None, :]
    norm_bound = lambda g: math.sqrt(QK_DIM) * BOUND_SLACK * jnp.max(jnp.abs(g))
    mix = MLA_HEADS * MLA_V
    return dict(
        w_in=w_in_p.astype(BF16), q_norm=q_norm[None, :], w_q=_head_slots(w_qb, QK_DIM).astype(BF16),
        kv_norm=kv_norm[None, :], w_k=w_k.astype(BF16), w_kr=w_kr.astype(BF16), w_v=w_v.astype(BF16),
        gain_q=pad_gain(qn_q) * (ATTN_SCALE * LOG2_E), gain_k=pad_gain(qn_k), v_one=v_one, k_one=k_one,
        q_norm_bound=norm_bound(qn_q) * (ATTN_SCALE * LOG2_E), k_norm_bound=norm_bound(qn_k),
        dw_w=dw_w[:, 0, :], dw_b=dw_b[None, :], cn_g=cn_g[None, :], cn_b=cn_b[None, :],
        w_out_att=w_out_att.reshape(MLA_HEADS * LANES, d).astype(BF16), w_out_conv=w_out[mix:].astype(BF16))


def _rope_tables(length, with_rope):
    n_freq = MLA_ROPE // 4
    if not with_rope:
        return jnp.ones((length, LANES), F32), jnp.zeros((length, LANES), F32), jnp.zeros((length, LANES), F32)
    t = jnp.arange(length)
    inv = ROPE_THETA ** (-jnp.arange(n_freq, dtype=F32) / n_freq)
    ang_r = (t // GRID_W).astype(F32)[:, None] * inv
    ang_c = (t % GRID_W).astype(F32)[:, None] * inv
    zeros = jnp.zeros((length, n_freq), F32)

    def slot(nope_val, r_first, r_second, c_first, c_second):
        nope = jnp.full((length, MLA_NOPE), nope_val, F32)
        pad = jnp.full((length, LANES - QK_DIM), nope_val, F32)
        return jnp.concatenate([nope, r_first, r_second, c_first, c_second, pad], axis=1)

    cos_r, sin_r, cos_c, sin_c = jnp.cos(ang_r), jnp.sin(ang_r), jnp.cos(ang_c), jnp.sin(ang_c)
    return (slot(1.0, cos_r, cos_r, cos_c, cos_c),
            slot(0.0, -sin_r, zeros, -sin_c, zeros),
            slot(0.0, zeros, sin_r, zeros, sin_c))


def _rwkv_params(x_mix, w_r, w_k, w_v, w0, w1, w2, a0, a1, a2, g1, g2, k_k, k_a, r_k, ln_g, ln_b, w_o):
    d = w_r.shape[0]
    bf = lambda w: w.astype(BF16)
    return dict(x_mix=x_mix, w_r=bf(w_r), w_k=bf(w_k), w_v=bf(w_v), w0=w0, w1=bf(w1), w2=bf(w2), a0=a0,
                a1=bf(a1), a2=bf(a2), g1=bf(g1), g2=bf(g2), k_k=k_k[None, :], k_a=k_a[None, :],
                r_k=r_k.reshape(1, d), ln_g=ln_g[None, :], ln_b=ln_b[None, :], w_o=bf(w_o))


def _mla_conv_layer(x, ctx, m_l, m_c, p):
    length, lc = x.shape[1], ctx.shape[1]
    q_l, k_l, v_l, hg_l = _mla_in(x, m_l, p, _rope_tables(length, True))
    q_c, k_c, v_c, hg_c = _mla_in(ctx, m_c, p, _rope_tables(lc, False))
    att_l = _attention_latent(q_l, k_c, v_c, k_l, v_l, p["q_norm_bound"], p["k_norm_bound"])
    att_c = _attention_online(q_c, k_c, v_c)
    consts = (p["w_out_att"], p["w_out_conv"])
    return (dict(delta=_mix_delta, tokens=(att_l, _conformer_conv(hg_l, p)), consts=consts),
            dict(delta=_mix_delta, tokens=(att_c, _conformer_conv(hg_c, p)), consts=consts))


def _rwkv_layer(x, ctx, m_l, m_c, p):
    bsz, _, d = x.shape
    r_c, v_c, kk_c, _, _, lw_c, kd_c, b_c = _rwkv_in(ctx, m_c, p)
    r_l, v_l, kk_l, g_l, bonus_l, lw_l, kd_l, b_l = _rwkv_in(x, m_l, p)
    s0 = jnp.zeros((bsz, 2, d // LANES, LANES, LANES), F32)
    _, s_ctx = _wkv_scan(r_c, v_c, kk_c, lw_c, kd_c, b_c, s0)
    y_l, _ = _wkv_scan(r_l, v_l, kk_l, lw_l, kd_l, b_l, s_ctx)
    return dict(delta=_rwkv_delta, tokens=(y_l, bonus_l, g_l), consts=(p["ln_g"], p["ln_b"], p["w_o"]))


def kernel(x, c, ctx, c_ctx, ada_w, ada_b, ffn_w1, ffn_w3, ffn_w2, mla_w_in, mla_q_norm, mla_w_qb, mla_kv_norm, mla_w_kvb, mla_qk_norm_q, mla_qk_norm_k, conv_dw_w, conv_dw_b, conv_norm_g, conv_norm_b, mix_w_out, rwkv_x_mix, rwkv_w_r, rwkv_w_k, rwkv_w_v, rwkv_w0, rwkv_w1, rwkv_w2, rwkv_a0, rwkv_a1, rwkv_a2, rwkv_g1, rwkv_g2, rwkv_k_k, rwkv_k_a, rwkv_r_k, rwkv_ln_g, rwkv_ln_b, rwkv_w_o):
    bsz = x.shape[0]
    depth = ada_w.shape[0]
    assert depth == 2, "layer 0 is the MLA/conv layer, layer 1 the (last) RWKV layer"
    cond = jnp.concatenate([c, c_ctx[None, :], jnp.zeros((16 - bsz - 1, c.shape[1]), F32)], axis=0)
    w1, w3, w2 = ffn_w1.astype(BF16), ffn_w3.astype(BF16), ffn_w2.astype(BF16)
    mla_p = _mla_params(mla_w_in[0], mla_q_norm[0], mla_w_qb[0], mla_kv_norm[0], mla_w_kvb[0], mla_qk_norm_q[0],
                        mla_qk_norm_k[0], conv_dw_w[0], conv_dw_b[0], conv_norm_g[0], conv_norm_b[0], mix_w_out[0])
    rwkv_p = _rwkv_params(rwkv_x_mix[0], rwkv_w_r[0], rwkv_w_k[0], rwkv_w_v[0], rwkv_w0[0], rwkv_w1[0],
                          rwkv_w2[0], rwkv_a0[0], rwkv_a1[0], rwkv_a2[0], rwkv_g1[0], rwkv_g2[0], rwkv_k_k[0],
                          rwkv_k_a[0], rwkv_r_k[0], rwkv_ln_g[0], rwkv_ln_b[0], rwkv_w_o[0])
    for i in range(depth):
        mods = _ada_mods(cond, ada_w[i], ada_b[i])
        m_l, m_c = mods[:bsz], mods[bsz:bsz + 1]
        x = _half_ffn(x, m_l, 0, w1[i, 0], w3[i, 0], w2[i, 0])
        ctx = _half_ffn(ctx, m_c, 0, w1[i, 0], w3[i, 0], w2[i, 0])
        if i == 0:
            mix_l, mix_c = _mla_conv_layer(x, ctx, m_l, m_c, mla_p)
            ctx = _half_ffn(ctx, m_c, 6, w1[i, 1], w3[i, 1], w2[i, 1], **mix_c)
        else:
            mix_l = _rwkv_layer(x, ctx, m_l, m_c, rwkv_p)
        x = _half_ffn(x, m_l, 6, w1[i, 1], w3[i, 1], w2[i, 1], **mix_l)
    return x
```

```python
import functools
import math

import jax
import jax.numpy as jnp
from jax import lax
from jax.experimental import pallas as pl
from jax.experimental.pallas import tpu as pltpu

F32 = jnp.float32
BF16 = jnp.bfloat16

LANES = 128
SUBLANES = 8
MXU_TILE = 256
VMEM_LIMIT = 56 * 1024 * 1024

NORM_EPS = 1e-6
N_MOD = 9
GRID_W = 64
ROPE_THETA = 10000.0

MLA_HEADS = 8
MLA_NOPE = 64
MLA_ROPE = 32
MLA_V = 64
MLA_Q_RANK = 384
MLA_KV_RANK = 128
QK_DIM = MLA_NOPE + MLA_ROPE
ATTN_SCALE = QK_DIM ** -0.5
LOG2_E = math.log2(math.e)
ATTN_TQ = 1024
ATTN_TK = 1024
ATTN_HEADS = 4
SHIFT_LANE = QK_DIM
BOUND_SLACK = 1.02
ATTN_BOUND_LIMIT = 40.0
CONV_CH = 512
CONV_K = 31
CONV_HALO = 16

RWKV_HEAD = 64
GN_EPS = RWKV_HEAD * 1e-5
SCAN_CHUNK = 64
SCAN_ROWS = 2
DECAY_SCALE = math.exp(-0.5)


def _dot(a, b):
    return jnp.dot(a.astype(BF16), b.astype(BF16), preferred_element_type=F32)


def _dot_nt(a, b):
    return lax.dot_general(a.astype(BF16), b.astype(BF16), (((1,), (1,)), ((), ())),
                           preferred_element_type=F32)


def _dot_tn(a, b):
    return lax.dot_general(a.astype(BF16), b.astype(BF16), (((0,), (0,)), ((), ())),
                           preferred_element_type=F32)


def _head_sums(x):
    even = lax.broadcasted_iota(jnp.int32, (1, LANES), 1) < RWKV_HEAD
    out = []
    for j in range(0, x.shape[1], LANES):
        u = x[:, j:j + LANES]
        both = jnp.sum(u, axis=-1, keepdims=True)
        first = jnp.sum(jnp.where(even, u, 0.0), axis=-1, keepdims=True)
        out.append(jnp.where(even, first, both - first))
    return jnp.concatenate(out, axis=1)


def _rms(x):
    return x * lax.rsqrt(jnp.mean(x * x, axis=-1, keepdims=True) + NORM_EPS)


def _sigmoid(x):
    return 1.0 / (1.0 + jnp.exp(-x))


def _modulate(x, m_ref, first):
    return _rms(x) * (1.0 + m_ref[first + 1:first + 2, :]) + m_ref[first:first + 1, :]


def _const_spec(shape):
    zeros = (0,) * len(shape)
    return pl.BlockSpec(shape, lambda *_: zeros, pipeline_mode=pl.Buffered(1))


def _params(*sem):
    return pltpu.CompilerParams(dimension_semantics=sem, vmem_limit_bytes=VMEM_LIMIT)


def _mod_spec(mods):
    d = mods.shape[-1]
    if mods.shape[0] == 1:
        return pl.BlockSpec((None, N_MOD, d), lambda b, i: (0, 0, 0))
    return pl.BlockSpec((None, N_MOD, d), lambda b, i: (b, 0, 0))


def _tile(length, pref):
    t = min(pref, length)
    assert length % t == 0, (length, t)
    return t


def _ada_kernel(c_ref, w_ref, b_ref, o_ref):
    c = c_ref[...]
    s = c * _sigmoid(c)
    w = w_ref[...]
    s_hi = s.astype(BF16)
    s_lo = (s - s_hi.astype(F32)).astype(BF16)
    w_hi = w.astype(BF16)
    w_lo = (w - w_hi.astype(F32)).astype(BF16)
    acc = jnp.dot(s_hi, w_hi, preferred_element_type=F32)
    acc += jnp.dot(s_hi, w_lo, preferred_element_type=F32)
    acc += jnp.dot(s_lo, w_hi, preferred_element_type=F32)
    o_ref[...] = acc + b_ref[...]


def _ada_mods(cond, w, b):
    r, d = cond.shape
    n = w.shape[1]
    out = pl.pallas_call(
        _ada_kernel,
        grid=(n // d,),
        in_specs=[pl.BlockSpec((r, d), lambda j: (0, 0)),
                  pl.BlockSpec((d, d), lambda j: (0, j)),
                  pl.BlockSpec((1, d), lambda j: (0, j))],
        out_specs=pl.BlockSpec((r, d), lambda j: (0, j)),
        out_shape=jax.ShapeDtypeStruct((r, n), F32),
        compiler_params=_params("parallel"),
        name="ada_mods",
    )(cond, w, b.reshape(1, n))
    return out.reshape(r, n // d, d)


def _ffn_kernel(x_ref, m_ref, *refs, first, f_split, delta, n_delta):
    delta_refs, (w1_ref, w3_ref, w2_ref, o_ref) = refs[:n_delta], refs[n_delta:]
    x = x_ref[...]
    if delta is not None:
        x = x + m_ref[5:6, :] * delta(*delta_refs)
    h = _modulate(x, m_ref, first).astype(BF16)
    acc = None
    for cols in (slice(0, f_split), slice(f_split, w1_ref.shape[1])):
        a = jnp.dot(h, w1_ref[:, cols], preferred_element_type=F32)
        b = jnp.dot(h, w3_ref[:, cols], preferred_element_type=F32)
        g = (a * _sigmoid(a) * b).astype(BF16)
        part = jnp.dot(g, w2_ref[cols, :], preferred_element_type=F32)
        acc = part if acc is None else acc + part
    o_ref[...] = x + (0.5 * m_ref[first + 2:first + 3, :]) * acc


def _half_ffn(x, mods, first, w1, w3, w2, delta=None, tokens=(), consts=()):
    bsz, length, d = x.shape
    f = w1.shape[1]
    tm = _tile(length, 512)
    assert f % MXU_TILE == 0
    f_split = (f // MXU_TILE + 1) // 2 * MXU_TILE

    def tok(a):
        if a.ndim == 3:
            return pl.BlockSpec((None, tm, a.shape[-1]), lambda b, i: (b, i, 0))
        return pl.BlockSpec((None, a.shape[1], tm, a.shape[-1]), lambda b, i: (b, 0, i, 0))

    return pl.pallas_call(
        functools.partial(_ffn_kernel, first=first, f_split=f_split, delta=delta, n_delta=len(tokens) + len(consts)),
        grid=(bsz, length // tm),
        in_specs=([tok(x), _mod_spec(mods)] + [tok(a) for a in tokens] + [_const_spec(c.shape) for c in consts]
                  + [_const_spec((d, f)), _const_spec((d, f)), _const_spec((f, d))]),
        out_specs=tok(x),
        out_shape=jax.ShapeDtypeStruct(x.shape, F32),
        compiler_params=_params("parallel", "parallel"),
        name="half_ffn",
    )(x, mods, *tokens, *consts, w1, w3, w2)


def _head_rms(t, width):
    slots = [t[:, j:j + LANES] for j in range(0, t.shape[1], LANES)]
    return jnp.concatenate(
        [u * lax.rsqrt(jnp.sum(u * u, axis=-1, keepdims=True) * (1.0 / width) + NORM_EPS) for u in slots], axis=1)


def _rope(t, cos, sin_lo, sin_hi):
    n = t.shape[1]
    rep = n // cos.shape[1]
    cos, sin_lo, sin_hi = (jnp.tile(u, (1, rep)) for u in (cos, sin_lo, sin_hi))
    return t * cos + pltpu.roll(t, n - 8, 1) * sin_lo + pltpu.roll(t, 8, 1) * sin_hi


def _mla_in_kernel(x_ref, m_ref, w_in_ref, qn_ref, wq_ref, kvn_ref, wk_ref, wkr_ref, wv_ref,
                   gq_ref, gk_ref, vone_ref, kone_ref, cos_ref, sl_ref, sh_ref,
                   q_ref, k_ref, v_ref, hg_ref):
    h = _modulate(x_ref[...], m_ref, 3)
    z = _dot(h, w_in_ref[...])
    o_kv = MLA_Q_RANK
    o_kr = o_kv + MLA_KV_RANK
    o_a = o_kr + LANES
    o_g = o_a + CONV_CH
    c_q = _rms(z[:, :o_kv]) * qn_ref[...]
    c_kv = _rms(z[:, o_kv:o_kr]) * kvn_ref[...]
    hg_ref[...] = z[:, o_a:o_g] * _sigmoid(z[:, o_g:o_g + CONV_CH])
    cos, s_lo, s_hi = cos_ref[...], sl_ref[...], sh_ref[...]
    q = _head_rms(_dot(c_q, wq_ref[...]), QK_DIM) * gq_ref[...]
    q_ref[...] = _rope(q, cos, s_lo, s_hi).astype(BF16)
    k = _dot(c_kv, wk_ref[...]) + _dot(z[:, o_kr:o_a], wkr_ref[...])
    k = _head_rms(k, QK_DIM) * gk_ref[...]
    k_ref[...] = _rope(k, cos, s_lo, s_hi).astype(BF16) + kone_ref[...]
    v_ref[...] = (_dot(c_kv, wv_ref[...]) + vone_ref[...]).astype(BF16)


def _mla_in(x, mods, p, rope_tabs):
    bsz, length, d = x.shape
    tm = _tile(length, 256)
    hs = MLA_HEADS * LANES
    cos, s_lo, s_hi = rope_tabs
    tab_spec = pl.BlockSpec((tm, LANES), lambda b, i: (i, 0))
    tok = lambda w: pl.BlockSpec((None, tm, w), lambda b, i: (b, i, 0))
    consts = [p["w_in"], p["q_norm"], p["w_q"], p["kv_norm"], p["w_k"], p["w_kr"], p["w_v"],
              p["gain_q"], p["gain_k"], p["v_one"], p["k_one"]]
    return pl.pallas_call(
        _mla_in_kernel,
        grid=(bsz, length // tm),
        in_specs=([tok(d), _mod_spec(mods)] + [_const_spec(c.shape) for c in consts]
                  + [tab_spec, tab_spec, tab_spec]),
        out_specs=[tok(hs), tok(hs), tok(hs), tok(CONV_CH)],
        out_shape=[jax.ShapeDtypeStruct((bsz, length, hs), BF16),
                   jax.ShapeDtypeStruct((bsz, length, hs), BF16),
                   jax.ShapeDtypeStruct((bsz, length, hs), BF16),
                   jax.ShapeDtypeStruct((bsz, length, CONV_CH), F32)],
        compiler_params=_params("parallel", "parallel"),
        name="mla_in",
    )(x, mods, *consts, cos, s_lo, s_hi)


def _attn_kernel(q_ref, k_ref, v_ref, o_ref, *, tk):
    tq = q_ref.shape[0]
    n_chunks = k_ref.shape[0] // tk
    heads = [slice(j * LANES, (j + 1) * LANES) for j in range(q_ref.shape[1] // LANES)]
    qs = [q_ref[:, cols] for cols in heads]

    def rows(c):
        return pl.ds(pl.multiple_of(c * tk, tk), tk)

    def scores(c):
        return tuple(_dot_nt(q, k_ref[rows(c), cols]) for q, cols in zip(qs, heads))

    def consume(c, s_all, carry):
        out = []
        for s, cols, (m, acc) in zip(s_all, heads, carry):
            m_new = jnp.maximum(m, jnp.max(s, axis=-1, keepdims=True))
            p = jnp.exp2(s - m_new).astype(BF16)
            acc = jnp.exp2(m - m_new) * acc + jnp.dot(p, v_ref[rows(c), cols], preferred_element_type=F32)
            out.append((m_new, acc))
        return tuple(out)

    def body(c, carry):
        s_all, state = carry
        return scores(c + 1), consume(c, s_all, state)

    init = tuple((jnp.full((tq, 1), -jnp.inf, F32), jnp.zeros((tq, LANES), F32)) for _ in heads)
    s_last, state = lax.fori_loop(0, n_chunks - 1, body, (scores(0), init))
    for cols, (_, acc) in zip(heads, consume(n_chunks - 1, s_last, state)):
        o_ref[:, cols] = (acc * (1.0 / acc[:, MLA_V:MLA_V + 1])).astype(BF16)


def _attn_bounded_kernel(q_ref, kc_ref, vtc_ref, kl_ref, vtl_ref, kmax_ref, o_ref, *, tk):
    tq = q_ref.shape[0]
    heads = [slice(j * LANES, (j + 1) * LANES) for j in range(q_ref.shape[1] // LANES)]
    lane = lax.broadcasted_iota(jnp.int32, (tq, LANES), 1)
    qs = []
    for cols in heads:
        q = q_ref[:, cols].astype(F32)
        bound = jnp.sqrt(jnp.sum(q * q, axis=-1, keepdims=True)) * kmax_ref[:, cols]
        qs.append(jnp.where(lane == SHIFT_LANE, -bound, q).astype(BF16))

    v_rows = -(-(MLA_V + 1) // 16) * 16

    def add_keys(accs, k_ref, vt_ref, rows, chunk):
        out = []
        for q, cols, acc in zip(qs, heads, accs):
            p_t = jnp.exp2(_dot_nt(k_ref[rows, cols], q)).astype(BF16)
            v_t = vt_ref[chunk, cols.start:cols.start + v_rows, :]
            out.append(acc + jnp.dot(v_t, p_t, preferred_element_type=F32))
        return tuple(out)

    accs = add_keys(tuple(jnp.zeros((v_rows, tq), F32) for _ in heads), kc_ref, vtc_ref, slice(None), 0)
    accs = lax.fori_loop(0, kl_ref.shape[0] // tk,
                         lambda c, a: add_keys(a, kl_ref, vtl_ref, pl.ds(pl.multiple_of(c * tk, tk), tk), c), accs)
    pad = jnp.zeros((LANES - v_rows, tq), F32)
    for cols, acc_t in zip(heads, accs):
        out_t = jnp.concatenate([acc_t * (1.0 / acc_t[MLA_V:MLA_V + 1, :]), pad], axis=0)
        o_ref[:, cols] = out_t.T.astype(BF16)


def _attn_call(kernel, name, q, keys_values, extra=()):
    bsz, lq, hs = q.shape
    tq = _tile(lq, ATTN_TQ)
    group = ATTN_HEADS * LANES
    blk = lambda rows, imap: pl.BlockSpec((None, rows, group), imap)
    whole = lambda b, h, i: (b, 0, h)
    return pl.pallas_call(
        kernel,
        grid=(bsz, hs // group, lq // tq),
        in_specs=([blk(tq, lambda b, h, i: (b, i, h))] + [blk(a.shape[1], whole) for a in keys_values]
                  + [blk(1, whole) for _ in extra]),
        out_specs=blk(tq, lambda b, h, i: (b, i, h)),
        out_shape=jax.ShapeDtypeStruct((bsz, lq, hs), BF16),
        compiler_params=_params("parallel", "parallel", "parallel"),
        name=name,
    )(q, *keys_values, *extra)


def _attention_online(q, k, v):
    lk = k.shape[1]
    tk = next(t for t in (ATTN_TK, 512, 256, lk) if lk % t == 0)
    return _attn_call(functools.partial(_attn_kernel, tk=tk), "attention", q, (k, v))


def _attention_latent(q, k_c, v_c, k_l, v_l, q_norm_bound, k_norm_bound):
    tk = next(t for t in (ATTN_TK, 512, 256, k_l.shape[1]) if k_l.shape[1] % t == 0)
    k_max_slots = jnp.full((q.shape[0], 1, q.shape[-1]), k_norm_bound, F32)
    cat = lambda a, b: jnp.concatenate([a, b], axis=1)

    def bounded(q, k_c, v_c, k_l, v_l, k_max_slots):
        bsz, lq, hs = q.shape
        tq = _tile(lq, ATTN_TQ)
        group = ATTN_HEADS * LANES
        chunks_t = lambda v, t: v.reshape(bsz, v.shape[1] // t, t, hs).transpose(0, 1, 3, 2)
        vt_c, vt_l = chunks_t(v_c, v_c.shape[1]), chunks_t(v_l, tk)
        blk = lambda rows, imap: pl.BlockSpec((None, rows, group), imap)
        blk_t = lambda a: pl.BlockSpec((None, a.shape[1], group, a.shape[3]), lambda b, h, i: (b, 0, h, 0),
                                       pipeline_mode=pl.Buffered(1))
        whole = lambda b, h, i: (b, 0, h)
        keys = lambda a: pl.BlockSpec((None, a.shape[1], group), whole, pipeline_mode=pl.Buffered(1))
        return pl.pallas_call(
            functools.partial(_attn_bounded_kernel, tk=tk),
            grid=(bsz, hs // group, lq // tq),
            in_specs=[blk(tq, lambda b, h, i: (b, i, h)), keys(k_c), blk_t(vt_c), keys(k_l), blk_t(vt_l),
                      blk(1, whole)],
            out_specs=blk(tq, lambda b, h, i: (b, i, h)),
            out_shape=jax.ShapeDtypeStruct((bsz, lq, hs), BF16),
            compiler_params=_params("parallel", "parallel", "parallel"),
            name="attention_bounded",
        )(q, k_c, vt_c, k_l, vt_l, k_max_slots)

    def online(q, k_c, v_c, k_l, v_l, _):
        return _attention_online(q, cat(k_c, k_l), cat(v_c, v_l))

    return lax.cond(q_norm_bound * k_norm_bound <= ATTN_BOUND_LIMIT, bounded, online,
                    q, k_c, v_c, k_l, v_l, k_max_slots)


def _conv_kernel(h_ref, hp_ref, hn_ref, w_ref, b_ref, g_ref, be_ref, o_ref, sh_ref, *, rows):
    i = pl.program_id(1)
    tm = h_ref.shape[0]
    sh_ref[0, 0:CONV_HALO, :] = jnp.where(i > 0, hp_ref[...], 0.0)
    sh_ref[0, CONV_HALO:CONV_HALO + tm, :] = h_ref[...]
    sh_ref[0, CONV_HALO + tm:, :] = jnp.where(i < pl.num_programs(1) - 1, hn_ref[...], 0.0)
    first = CONV_HALO - CONV_K // 2
    used = tm + 2 * CONV_HALO - SUBLANES
    for s in range(1, SUBLANES):
        sh_ref[s, 0:used, :] = sh_ref[0, s:s + used, :]
    for r0 in range(0, tm, rows):
        acc = jnp.zeros((rows, CONV_CH), F32)
        for j in range(CONV_K):
            shift, base = (first + j) % SUBLANES, (first + j) // SUBLANES * SUBLANES
            acc += sh_ref[shift, r0 + base:r0 + base + rows, :] * w_ref[j:j + 1, :]
        acc += b_ref[...]
        xc = acc - jnp.mean(acc, axis=-1, keepdims=True)
        y = xc * lax.rsqrt(jnp.mean(xc * xc, axis=-1, keepdims=True) + NORM_EPS)
        y = y * g_ref[...] + be_ref[...]
        o_ref[r0:r0 + rows, :] = (y * _sigmoid(y)).astype(BF16)


def _conformer_conv(hg, p):
    bsz, length, ch = hg.shape
    tm = _tile(length, 512)
    nh = tm // CONV_HALO
    last = length // CONV_HALO - 1
    return pl.pallas_call(
        functools.partial(_conv_kernel, rows=32),
        grid=(bsz, length // tm),
        in_specs=[pl.BlockSpec((None, tm, ch), lambda b, i: (b, i, 0)),
                  pl.BlockSpec((None, CONV_HALO, ch), lambda b, i: (b, jnp.maximum(i * nh - 1, 0), 0)),
                  pl.BlockSpec((None, CONV_HALO, ch), lambda b, i: (b, jnp.minimum((i + 1) * nh, last), 0)),
                  _const_spec((CONV_K, ch)), _const_spec((1, ch)), _const_spec((1, ch)), _const_spec((1, ch))],
        out_specs=pl.BlockSpec((None, tm, ch), lambda b, i: (b, i, 0)),
        out_shape=jax.ShapeDtypeStruct((bsz, length, ch), BF16),
        scratch_shapes=[pltpu.VMEM((SUBLANES, tm + 2 * CONV_HALO, ch), F32)],
        compiler_params=_params("parallel", "parallel"),
        name="conformer_conv",
    )(hg, hg, hg, p["dw_w"], p["dw_b"], p["cn_g"], p["cn_b"])


def _mix_delta(a_ref, c_ref, wa_ref, wc_ref):
    return (jnp.dot(a_ref[...], wa_ref[...], preferred_element_type=F32)
            + jnp.dot(c_ref[...], wc_ref[...], preferred_element_type=F32))


def _rwkv_in_kernel(x_ref, xp_ref, xn_ref, m_ref, xmix_ref, wr_ref, wk_ref, wv_ref, w0_ref, w1_ref, w2_ref,
                    a0_ref, a1_ref, a2_ref, g1_ref, g2_ref, kk_ref, ka_ref, rk_ref,
                    r_out, v_out, kkn_out, g_out, bonus_out, lw_out, kd_out, b_out):
    i = pl.program_id(1)
    tm = x_ref.shape[0]
    h = _modulate(x_ref[...], m_ref, 3)
    h_before = jnp.where(i > 0, _modulate(xp_ref[7:8, :], m_ref, 3), 0.0)
    h_after = jnp.where(i < pl.num_programs(1) - 1, _modulate(xn_ref[0:1, :], m_ref, 3), 0.0)
    row = lax.broadcasted_iota(jnp.int32, (tm, 1), 0)
    h_prev = jnp.where(row == 0, h_before, pltpu.roll(h, 1, 0))
    h_next = jnp.where(row == tm - 1, h_after, pltpu.roll(h, tm - 1, 0))
    xx = 0.5 * (h_prev + h_next) - h
    xr, xw, xk, xv, xa, xg = (h + xx * xmix_ref[j:j + 1, :] for j in range(6))
    r = _dot(xr, wr_ref[...])
    k = _dot(xk, wk_ref[...])
    v = _dot(xv, wv_ref[...])
    kk = k * kk_ref[...]
    kk = kk * (1.0 / jnp.maximum(jnp.sqrt(_head_sums(kk * kk)), 1e-12))
    g_out[...] = _dot(_sigmoid(_dot(xg, g1_ref[...])), g2_ref[...]).astype(BF16)
    k_sum = None
    for d in range(2):
        w_pre = w0_ref[d:d + 1, :] + _dot(jnp.tanh(_dot(xw, w1_ref[d])), w2_ref[d])
        lw_out[d] = -DECAY_SCALE * _sigmoid(w_pre)
        a = _sigmoid(a0_ref[d:d + 1, :] + _dot(_dot(xa, a1_ref[d]), a2_ref[d]))
        k_d = k * (1.0 + (a - 1.0) * ka_ref[...])
        kd_out[d] = k_d.astype(BF16)
        b_out[d] = (kk * a).astype(BF16)
        k_sum = k_d if k_sum is None else k_sum + k_d
    r_out[...] = r.astype(BF16)
    v_out[...] = v.astype(BF16)
    kkn_out[...] = kk.astype(BF16)
    bonus_out[...] = (_head_sums(r * (0.5 * k_sum) * rk_ref[...]) * v).astype(BF16)


def _rwkv_in(x, mods, p):
    bsz, length, d = x.shape
    tm = _tile(length, 512)
    nb = tm // 8
    last = length // 8 - 1
    tok = pl.BlockSpec((None, tm, d), lambda b, i: (b, i, 0))
    tok2 = pl.BlockSpec((None, 2, tm, d), lambda b, i: (b, 0, i, 0))
    consts = [p[n] for n in ("x_mix", "w_r", "w_k", "w_v", "w0", "w1", "w2", "a0", "a1", "a2", "g1", "g2",
                             "k_k", "k_a", "r_k")]
    one = jax.ShapeDtypeStruct((bsz, length, d), BF16)
    two = jax.ShapeDtypeStruct((bsz, 2, length, d), BF16)
    log_decay = jax.ShapeDtypeStruct((bsz, 2, length, d), F32)
    return pl.pallas_call(
        _rwkv_in_kernel,
        grid=(bsz, length // tm),
        in_specs=([tok,
                   pl.BlockSpec((None, 8, d), lambda b, i: (b, jnp.maximum(i * nb - 1, 0), 0)),
                   pl.BlockSpec((None, 8, d), lambda b, i: (b, jnp.minimum((i + 1) * nb, last), 0)),
                   _mod_spec(mods)] + [_const_spec(c.shape) for c in consts]),
        out_specs=[tok, tok, tok, tok, tok, tok2, tok2, tok2],
        out_shape=[one, one, one, one, one, log_decay, two, two],
        compiler_params=_params("parallel", "parallel"),
        name="rwkv_in",
    )(x, x, x, mods, *consts)


def _scan_kernel(r_ref, v_ref, kk_ref, lw_ref, kd_ref, b_ref, s0_ref, y_ref, st_ref, state_ref):
    d = pl.program_id(1)
    c = pl.program_id(2)
    n_rows, cs = lw_ref.shape[0], lw_ref.shape[1]
    n_pairs = state_ref.shape[1]

    @pl.when(c == 0)
    def _():
        state_ref[...] = s0_ref[...]

    rev = d == 1
    sign = 1 - 2 * d
    ti = lax.broadcasted_iota(jnp.int32, (cs, cs), 0)
    si = lax.broadcasted_iota(jnp.int32, (cs, cs), 1)
    incl = jnp.where((ti - si) * sign >= 0, 1.0, 0.0).astype(BF16)

    def scaled_streams(i):
        lw = lw_ref[i]
        l1 = lw.astype(BF16)
        rem = lw - l1.astype(F32)
        l2 = rem.astype(BF16)
        l3 = (rem - l2.astype(F32)).astype(BF16)
        cum = (jnp.dot(incl, l1, preferred_element_type=F32) + jnp.dot(incl, l2, preferred_element_type=F32)
               + jnp.dot(incl, l3, preferred_element_type=F32))
        total = jnp.where(rev, cum[0:1, :], cum[cs - 1:cs, :])
        p_inv = jnp.exp(-cum)
        to_end = jnp.exp(total - cum)
        kd = kd_ref[i]
        bv = b_ref[i]
        return dict(r_t=r_ref[i] * jnp.exp(cum), a_t=-kk_ref[i] * jnp.exp(cum - lw), k_t=kd * p_inv,
                    b_t=bv * p_inv, k_e=kd * to_end, b_e=bv * to_end, v=v_ref[i], p_tot=jnp.exp(total))

    rows = [scaled_streams(i) for i in range(n_rows)]

    assert cs & (cs - 1) == 0
    t2 = lax.broadcasted_iota(jnp.int32, (2 * cs, 2 * cs), 0) & (cs - 1)
    s2 = lax.broadcasted_iota(jnp.int32, (2 * cs, 2 * cs), 1) & (cs - 1)
    strict2 = (t2 - s2) * sign > 0
    incl2 = (t2 - s2) * sign >= 0
    even = lax.broadcasted_iota(jnp.int32, (cs, LANES), 1) < RWKV_HEAD
    vi = lax.broadcasted_iota(jnp.int32, (LANES, LANES), 0) < RWKV_HEAD
    ki = lax.broadcasted_iota(jnp.int32, (LANES, LANES), 1) < RWKV_HEAD
    same_head = vi == ki

    def stack(t):
        return jnp.concatenate([jnp.where(even, t, 0.0), jnp.where(even, 0.0, t)], axis=0)

    def twice(t):
        return jnp.concatenate([t, t], axis=0)

    bf = lambda t: t.astype(BF16)
    mm = lambda a, b: jnp.dot(a, b, preferred_element_type=F32)
    chains = [(i, p, slice(p * LANES, (p + 1) * LANES)) for i in range(n_rows) for p in range(n_pairs)]
    idx = range(len(chains))
    col = lambda name: [rows[i][name][:, sl] for i, _, sl in chains]
    a_t, r_t, b_t, k_t, b_e, k_e, v, p_tot = (col(n) for n in ("a_t", "r_t", "b_t", "k_t", "b_e", "k_e", "v", "p_tot"))
    eye = jnp.where(lax.broadcasted_iota(jnp.int32, (2 * cs, 2 * cs), 0)
                    == lax.broadcasted_iota(jnp.int32, (2 * cs, 2 * cs), 1), 1.0, 0.0)

    cross = [_dot_nt(jnp.concatenate([stack(a_t[n]), stack(r_t[n])], axis=0),
                     jnp.concatenate([stack(b_t[n]), stack(k_t[n])], axis=0)) for n in idx]
    m_ab = [jnp.where(strict2, x[:2 * cs, :2 * cs], 0.0) for x in cross]
    m_ak = [bf(jnp.where(strict2, x[:2 * cs, 2 * cs:], 0.0)) for x in cross]
    m_rb = [bf(jnp.where(incl2, x[2 * cs:, :2 * cs], 0.0)) for x in cross]
    m_rk = [bf(jnp.where(incl2, x[2 * cs:, 2 * cs:], 0.0)) for x in cross]
    v2 = [bf(twice(t)) for t in v]
    x_v = [mm(m_ak[n], v2[n]) for n in idx]
    y_v = [mm(m_rk[n], v2[n]) for n in idx]
    upd_v = [_dot_tn(v2[n], stack(k_e[n])) for n in idx]
    pw = [bf(m) for m in m_ab]
    t_inv = [eye + m for m in m_ab]
    for _ in range(max(1, (cs - 1).bit_length()) - 1):
        pw = [bf(mm(w, w)) for w in pw]
        t_inv = [t_inv[n] + mm(pw[n], bf(t_inv[n])) for n in idx]
    t_inv = [bf(t) for t in t_inv]

    states = [state_ref[i, p] for i, p, _ in chains]
    ar = [_dot_nt(jnp.concatenate([a_t[n], r_t[n]], axis=0), states[n]) for n in idx]
    u = [bf(mm(t_inv[n], bf(twice(ar[n][:cs]) + x_v[n]))) for n in idx]
    for n, (i, p, sl) in enumerate(chains):
        y2 = twice(ar[n][cs:]) + y_v[n] + mm(m_rb[n], u[n])
        y_ref[i, :, sl] = jnp.where(even, y2[:cs], y2[cs:])
        upd = upd_v[n] + _dot_tn(u[n], stack(b_e[n]))
        state_ref[i, p] = states[n] * p_tot[n] + jnp.where(same_head, upd, 0.0)

    @pl.when(c == pl.num_programs(2) - 1)
    def _():
        st_ref[...] = state_ref[...]


def _wkv_scan(r, v, kk, lw, kd, bv, s0):
    bsz, length, d = r.shape
    cs = SCAN_CHUNK
    nc = length // cs
    n_pairs = d // LANES
    nr = SCAN_ROWS if bsz % SCAN_ROWS == 0 else 1
    chunk = lambda d_, c: c + d_ * (nc - 1 - 2 * c)
    shared = pl.BlockSpec((nr, cs, d), lambda b, d_, c: (b, chunk(d_, c), 0))
    per_dir = pl.BlockSpec((nr, None, cs, d), lambda b, d_, c: (b, d_, chunk(d_, c), 0))
    st_spec = pl.BlockSpec((nr, None, n_pairs, LANES, LANES), lambda b, d_, c: (b, d_, 0, 0, 0))
    return pl.pallas_call(
        _scan_kernel,
        grid=(bsz // nr, 2, nc),
        in_specs=[shared, shared, shared, per_dir, per_dir, per_dir, st_spec],
        out_specs=[per_dir, st_spec],
        out_shape=[jax.ShapeDtypeStruct((bsz, 2, length, d), F32),
                   jax.ShapeDtypeStruct(s0.shape, F32)],
        scratch_shapes=[pltpu.VMEM((nr, n_pairs, LANES, LANES), F32)],
        compiler_params=_params("parallel", "parallel", "arbitrary"),
        name="wkv_scan",
    )(r, v, kk, lw, kd, bv, s0)


def _rwkv_delta(y_ref, bonus_ref, g_ref, lng_ref, lnb_ref, wo_ref):
    y = y_ref[0] + y_ref[1]
    inv_n = 1.0 / RWKV_HEAD
    yc = y - _head_sums(y) * inv_n
    rstd = lax.rsqrt(_head_sums(yc * yc) * inv_n + GN_EPS)
    yn = yc * rstd * lng_ref[...] + lnb_ref[...]
    return _dot((yn + bonus_ref[...]) * g_ref[...], wo_ref[...])


def _head_slots(w, per_head, offset=0):
    k = w.shape[0]
    w = w.reshape(k, MLA_HEADS, per_head)
    w = jnp.pad(w, ((0, 0), (0, 0), (offset, LANES - per_head - offset)))
    return w.reshape(k, MLA_HEADS * LANES)


def _mla_params(w_in, q_norm, w_qb, kv_norm, w_kvb, qn_q, qn_k, dw_w, dw_b, cn_g, cn_b, w_out):
    d = w_in.shape[0]
    o_kr = MLA_Q_RANK + MLA_KV_RANK
    mla_in = o_kr + MLA_ROPE
    w_in_p = jnp.concatenate([w_in[:, :mla_in], jnp.zeros((d, LANES - MLA_ROPE), F32), w_in[:, mla_in:]], axis=1)
    kvb = w_kvb.reshape(MLA_KV_RANK, MLA_HEADS, MLA_NOPE + MLA_V)
    w_k = _head_slots(kvb[:, :, :MLA_NOPE].reshape(MLA_KV_RANK, -1), MLA_NOPE)
    w_v = _head_slots(kvb[:, :, MLA_NOPE:].reshape(MLA_KV_RANK, -1), MLA_V)
    v_one = jnp.tile(jnp.arange(LANES) == MLA_V, MLA_HEADS).astype(F32)[None, :]
    k_one = jnp.tile(jnp.arange(LANES) == SHIFT_LANE, MLA_HEADS).astype(BF16)[None, :]
    w_out_att = jnp.pad(w_out[:MLA_HEADS * MLA_V].reshape(MLA_HEADS, MLA_V, d), ((0, 0), (0, LANES - MLA_V), (0, 0)))
    place = jnp.pad(jnp.eye(MLA_ROPE, dtype=F32), ((0, LANES - MLA_ROPE), (MLA_NOPE, LANES - QK_DIM)))
    w_kr = jnp.tile(place, (1, MLA_HEADS))
    pad_gain = lambda g: jnp.tile(jnp.pad(g, (0, LANES - QK_DIM)), MLA_HEADS)[None, :]
    norm_bound = lambda g: math.sqrt(QK_DIM) * BOUND_SLACK * jnp.max(jnp.abs(g))
    mix = MLA_HEADS * MLA_V
    return dict(
        w_in=w_in_p.astype(BF16), q_norm=q_norm[None, :], w_q=_head_slots(w_qb, QK_DIM).astype(BF16),
        kv_norm=kv_norm[None, :], w_k=w_k.astype(BF16), w_kr=w_kr.astype(BF16), w_v=w_v.astype(BF16),
        gain_q=pad_gain(qn_q) * (ATTN_SCALE * LOG2_E), gain_k=pad_gain(qn_k), v_one=v_one, k_one=k_one,
        q_norm_bound=norm_bound(qn_q) * (ATTN_SCALE * LOG2_E), k_norm_bound=norm_bound(qn_k),
        dw_w=dw_w[:, 0, :], dw_b=dw_b[None, :], cn_g=cn_g[None, :], cn_b=cn_b[None, :],
        w_out_att=w_out_att.reshape(MLA_HEADS * LANES, d).astype(BF16), w_out_conv=w_out[mix:].astype(BF16))


def _rope_tables(length, with_rope):
    n_freq = MLA_ROPE // 4
    if not with_rope:
        return jnp.ones((length, LANES), F32), jnp.zeros((length, LANES), F32), jnp.zeros((length, LANES), F32)
    t = jnp.arange(length)
    inv = ROPE_THETA ** (-jnp.arange(n_freq, dtype=F32) / n_freq)
    ang_r = (t // GRID_W).astype(F32)[:, None] * inv
    ang_c = (t % GRID_W).astype(F32)[:, None] * inv
    zeros = jnp.zeros((length, n_freq), F32)

    def slot(nope_val, r_first, r_second, c_first, c_second):
        nope = jnp.full((length, MLA_NOPE), nope_val, F32)
        pad = jnp.full((length, LANES - QK_DIM), nope_val, F32)
        return jnp.concatenate([nope, r_first, r_second, c_first, c_second, pad], axis=1)

    cos_r, sin_r, cos_c, sin_c = jnp.cos(ang_r), jnp.sin(ang_r), jnp.cos(ang_c), jnp.sin(ang_c)
    return (slot(1.0, cos_r, cos_r, cos_c, cos_c),
            slot(0.0, -sin_r, zeros, -sin_c, zeros),
            slot(0.0, zeros, sin_r, zeros, sin_c))


def _rwkv_params(x_mix, w_r, w_k, w_v, w0, w1, w2, a0, a1, a2, g1, g2, k_k, k_a, r_k, ln_g, ln_b, w_o):
    d = w_r.shape[0]
    bf = lambda w: w.astype(BF16)
    return dict(x_mix=x_mix, w_r=bf(w_r), w_k=bf(w_k), w_v=bf(w_v), w0=w0, w1=bf(w1), w2=bf(w2), a0=a0,
                a1=bf(a1), a2=bf(a2), g1=bf(g1), g2=bf(g2), k_k=k_k[None, :], k_a=k_a[None, :],
                r_k=r_k.reshape(1, d), ln_g=ln_g[None, :], ln_b=ln_b[None, :], w_o=bf(w_o))


def _mla_conv_layer(x, ctx, m_l, m_c, p):
    length, lc = x.shape[1], ctx.shape[1]
    q_l, k_l, v_l, hg_l = _mla_in(x, m_l, p, _rope_tables(length, True))
    q_c, k_c, v_c, hg_c = _mla_in(ctx, m_c, p, _rope_tables(lc, False))
    att_l = _attention_latent(q_l, k_c, v_c, k_l, v_l, p["q_norm_bound"], p["k_norm_bound"])
    att_c = _attention_online(q_c, k_c, v_c)
    consts = (p["w_out_att"], p["w_out_conv"])
    return (dict(delta=_mix_delta, tokens=(att_l, _conformer_conv(hg_l, p)), consts=consts),
            dict(delta=_mix_delta, tokens=(att_c, _conformer_conv(hg_c, p)), consts=consts))


def _rwkv_layer(x, ctx, m_l, m_c, p):
    bsz, _, d = x.shape
    r_c, v_c, kk_c, _, _, lw_c, kd_c, b_c = _rwkv_in(ctx, m_c, p)
    r_l, v_l, kk_l, g_l, bonus_l, lw_l, kd_l, b_l = _rwkv_in(x, m_l, p)
    s0 = jnp.zeros((bsz, 2, d // LANES, LANES, LANES), F32)
    _, s_ctx = _wkv_scan(r_c, v_c, kk_c, lw_c, kd_c, b_c, s0)
    y_l, _ = _wkv_scan(r_l, v_l, kk_l, lw_l, kd_l, b_l, s_ctx)
    return dict(delta=_rwkv_delta, tokens=(y_l, bonus_l, g_l), consts=(p["ln_g"], p["ln_b"], p["w_o"]))


def kernel(x, c, ctx, c_ctx, ada_w, ada_b, ffn_w1, ffn_w3, ffn_w2, mla_w_in, mla_q_norm, mla_w_qb, mla_kv_norm, mla_w_kvb, mla_qk_norm_q, mla_qk_norm_k, conv_dw_w, conv_dw_b, conv_norm_g, conv_norm_b, mix_w_out, rwkv_x_mix, rwkv_w_r, rwkv_w_k, rwkv_w_v, rwkv_w0, rwkv_w1, rwkv_w2, rwkv_a0, rwkv_a1, rwkv_a2, rwkv_g1, rwkv_g2, rwkv_k_k, rwkv_k_a, rwkv_r_k, rwkv_ln_g, rwkv_ln_b, rwkv_w_o):
    bsz = x.shape[0]
    depth = ada_w.shape[0]
    assert depth == 2, "layer 0 is the MLA/conv layer, layer 1 the (last) RWKV layer"
    cond = jnp.concatenate([c, c_ctx[None, :], jnp.zeros((16 - bsz - 1, c.shape[1]), F32)], axis=0)
    w1, w3, w2 = ffn_w1.astype(BF16), ffn_w3.astype(BF16), ffn_w2.astype(BF16)
    mla_p = _mla_params(mla_w_in[0], mla_q_norm[0], mla_w_qb[0], mla_kv_norm[0], mla_w_kvb[0], mla_qk_norm_q[0],
                        mla_qk_norm_k[0], conv_dw_w[0], conv_dw_b[0], conv_norm_g[0], conv_norm_b[0], mix_w_out[0])
    rwkv_p = _rwkv_params(rwkv_x_mix[0], rwkv_w_r[0], rwkv_w_k[0], rwkv_w_v[0], rwkv_w0[0], rwkv_w1[0],
                          rwkv_w2[0], rwkv_a0[0], rwkv_a1[0], rwkv_a2[0], rwkv_g1[0], rwkv_g2[0], rwkv_k_k[0],
                          rwkv_k_a[0], rwkv_r_k[0], rwkv_ln_g[0], rwkv_ln_b[0], rwkv_w_o[0])
    for i in range(depth):
        mods = _ada_mods(cond, ada_w[i], ada_b[i])
        m_l, m_c = mods[:bsz], mods[bsz:bsz + 1]
        x = _half_ffn(x, m_l, 0, w1[i, 0], w3[i, 0], w2[i, 0])
        ctx = _half_ffn(ctx, m_c, 0, w1[i, 0], w3[i, 0], w2[i, 0])
        if i == 0:
            mix_l, mix_c = _mla_conv_layer(x, ctx, m_l, m_c, mla_p)
            ctx = _half_ffn(ctx, m_c, 6, w1[i, 1], w3[i, 1], w2[i, 1], **mix_c)
        else:
            mix_l = _rwkv_layer(x, ctx, m_l, m_c, rwkv_p)
        x = _half_ffn(x, m_l, 6, w1[i, 1], w3[i, 1], w2[i, 1], **mix_l)
    return x
```

```python
import functools
import math

import jax
import jax.numpy as jnp
from jax import lax
from jax.experimental import pallas as pl
from jax.experimental.pallas import tpu as pltpu

F32 = jnp.float32
BF16 = jnp.bfloat16

LANES = 128
SUBLANES = 8
MXU_TILE = 256
VMEM_LIMIT = 56 * 1024 * 1024

NORM_EPS = 1e-6
N_MOD = 9
GRID_W = 64
ROPE_THETA = 10000.0

MLA_HEADS = 8
MLA_NOPE = 64
MLA_ROPE = 32
MLA_V = 64
MLA_Q_RANK = 384
MLA_KV_RANK = 128
QK_DIM = MLA_NOPE + MLA_ROPE
ATTN_SCALE = QK_DIM ** -0.5
LOG2_E = math.log2(math.e)
ATTN_TQ = 1024
ATTN_TK = 4096
ATTN_HEADS = 4
SHIFT_LANE = QK_DIM
BOUND_SLACK = 1.02
ATTN_BOUND_LIMIT = 40.0
CONV_CH = 512
CONV_K = 31
CONV_HALO = 16

RWKV_HEAD = 64
GN_EPS = RWKV_HEAD * 1e-5
SCAN_CHUNK = 64
SCAN_ROWS = 2
DECAY_SCALE = math.exp(-0.5)


def _dot(a, b):
    return jnp.dot(a.astype(BF16), b.astype(BF16), preferred_element_type=F32)


def _dot_nt(a, b):
    return lax.dot_general(a.astype(BF16), b.astype(BF16), (((1,), (1,)), ((), ())),
                           preferred_element_type=F32)


def _dot_tn(a, b):
    return lax.dot_general(a.astype(BF16), b.astype(BF16), (((0,), (0,)), ((), ())),
                           preferred_element_type=F32)


def _head_sums(x):
    even = lax.broadcasted_iota(jnp.int32, (1, LANES), 1) < RWKV_HEAD
    out = []
    for j in range(0, x.shape[1], LANES):
        u = x[:, j:j + LANES]
        both = jnp.sum(u, axis=-1, keepdims=True)
        first = jnp.sum(jnp.where(even, u, 0.0), axis=-1, keepdims=True)
        out.append(jnp.where(even, first, both - first))
    return jnp.concatenate(out, axis=1)


def _rms(x):
    return x * lax.rsqrt(jnp.mean(x * x, axis=-1, keepdims=True) + NORM_EPS)


def _sigmoid(x):
    return 1.0 / (1.0 + jnp.exp(-x))


def _modulate(x, m_ref, first):
    return _rms(x) * (1.0 + m_ref[first + 1:first + 2, :]) + m_ref[first:first + 1, :]


def _const_spec(shape):
    zeros = (0,) * len(shape)
    return pl.BlockSpec(shape, lambda *_: zeros, pipeline_mode=pl.Buffered(1))


def _params(*sem):
    return pltpu.CompilerParams(dimension_semantics=sem, vmem_limit_bytes=VMEM_LIMIT)


def _mod_spec(mods):
    d = mods.shape[-1]
    if mods.shape[0] == 1:
        return pl.BlockSpec((None, N_MOD, d), lambda b, i: (0, 0, 0))
    return pl.BlockSpec((None, N_MOD, d), lambda b, i: (b, 0, 0))


def _tile(length, pref):
    t = min(pref, length)
    assert length % t == 0, (length, t)
    return t


def _ada_kernel(c_ref, w_ref, b_ref, o_ref):
    c = c_ref[...]
    s = c * _sigmoid(c)
    w = w_ref[...]
    s_hi = s.astype(BF16)
    s_lo = (s - s_hi.astype(F32)).astype(BF16)
    w_hi = w.astype(BF16)
    w_lo = (w - w_hi.astype(F32)).astype(BF16)
    acc = jnp.dot(s_hi, w_hi, preferred_element_type=F32)
    acc += jnp.dot(s_hi, w_lo, preferred_element_type=F32)
    acc += jnp.dot(s_lo, w_hi, preferred_element_type=F32)
    o_ref[...] = acc + b_ref[...]


def _ada_mods(cond, w, b):
    r, d = cond.shape
    n = w.shape[1]
    out = pl.pallas_call(
        _ada_kernel,
        grid=(n // d,),
        in_specs=[pl.BlockSpec((r, d), lambda j: (0, 0)),
                  pl.BlockSpec((d, d), lambda j: (0, j)),
                  pl.BlockSpec((1, d), lambda j: (0, j))],
        out_specs=pl.BlockSpec((r, d), lambda j: (0, j)),
        out_shape=jax.ShapeDtypeStruct((r, n), F32),
        compiler_params=_params("parallel"),
        name="ada_mods",
    )(cond, w, b.reshape(1, n))
    return out.reshape(r, n // d, d)


def _ffn_kernel(x_ref, m_ref, *refs, first, f_split, delta, n_delta):
    delta_refs, (w1_ref, w3_ref, w2_ref, o_ref) = refs[:n_delta], refs[n_delta:]
    x = x_ref[...]
    if delta is not None:
        x = x + m_ref[5:6, :] * delta(*delta_refs)
    h = _modulate(x, m_ref, first).astype(BF16)
    acc = None
    for cols in (slice(0, f_split), slice(f_split, w1_ref.shape[1])):
        a = jnp.dot(h, w1_ref[:, cols], preferred_element_type=F32)
        b = jnp.dot(h, w3_ref[:, cols], preferred_element_type=F32)
        g = (a * _sigmoid(a) * b).astype(BF16)
        part = jnp.dot(g, w2_ref[cols, :], preferred_element_type=F32)
        acc = part if acc is None else acc + part
    o_ref[...] = x + (0.5 * m_ref[first + 2:first + 3, :]) * acc


def _half_ffn(x, mods, first, w1, w3, w2, delta=None, tokens=(), consts=()):
    bsz, length, d = x.shape
    f = w1.shape[1]
    tm = _tile(length, 512)
    assert f % MXU_TILE == 0
    f_split = (f // MXU_TILE + 1) // 2 * MXU_TILE

    def tok(a):
        if a.ndim == 3:
            return pl.BlockSpec((None, tm, a.shape[-1]), lambda b, i: (b, i, 0))
        return pl.BlockSpec((None, a.shape[1], tm, a.shape[-1]), lambda b, i: (b, 0, i, 0))

    return pl.pallas_call(
        functools.partial(_ffn_kernel, first=first, f_split=f_split, delta=delta, n_delta=len(tokens) + len(consts)),
        grid=(bsz, length // tm),
        in_specs=([tok(x), _mod_spec(mods)] + [tok(a) for a in tokens] + [_const_spec(c.shape) for c in consts]
                  + [_const_spec((d, f)), _const_spec((d, f)), _const_spec((f, d))]),
        out_specs=tok(x),
        out_shape=jax.ShapeDtypeStruct(x.shape, F32),
        compiler_params=_params("parallel", "parallel"),
        name="half_ffn",
    )(x, mods, *tokens, *consts, w1, w3, w2)


def _head_rms(t, width):
    slots = [t[:, j:j + LANES] for j in range(0, t.shape[1], LANES)]
    return jnp.concatenate(
        [u * lax.rsqrt(jnp.sum(u * u, axis=-1, keepdims=True) * (1.0 / width) + NORM_EPS) for u in slots], axis=1)


def _rope(t, cos, sin_lo, sin_hi):
    n = t.shape[1]
    rep = n // cos.shape[1]
    cos, sin_lo, sin_hi = (jnp.tile(u, (1, rep)) for u in (cos, sin_lo, sin_hi))
    return t * cos + pltpu.roll(t, n - 8, 1) * sin_lo + pltpu.roll(t, 8, 1) * sin_hi


def _mla_in_kernel(x_ref, m_ref, w_in_ref, qn_ref, wq_ref, kvn_ref, wk_ref, wkr_ref, wv_ref,
                   gq_ref, gk_ref, vone_ref, kone_ref, cos_ref, sl_ref, sh_ref,
                   q_ref, k_ref, v_ref, hg_ref):
    h = _modulate(x_ref[...], m_ref, 3)
    z = _dot(h, w_in_ref[...])
    o_kv = MLA_Q_RANK
    o_kr = o_kv + MLA_KV_RANK
    o_a = o_kr + LANES
    o_g = o_a + CONV_CH
    c_q = _rms(z[:, :o_kv]) * qn_ref[...]
    c_kv = _rms(z[:, o_kv:o_kr]) * kvn_ref[...]
    hg_ref[...] = z[:, o_a:o_g] * _sigmoid(z[:, o_g:o_g + CONV_CH])
    cos, s_lo, s_hi = cos_ref[...], sl_ref[...], sh_ref[...]
    q = _head_rms(_dot(c_q, wq_ref[...]), QK_DIM) * gq_ref[...]
    q_ref[...] = _rope(q, cos, s_lo, s_hi).astype(BF16)
    k = _dot(c_kv, wk_ref[...]) + _dot(z[:, o_kr:o_a], wkr_ref[...])
    k = _head_rms(k, QK_DIM) * gk_ref[...]
    k_ref[...] = _rope(k, cos, s_lo, s_hi).astype(BF16) + kone_ref[...]
    v_ref[...] = (_dot(c_kv, wv_ref[...]) + vone_ref[...]).astype(BF16)


def _mla_in(x, mods, p, rope_tabs):
    bsz, length, d = x.shape
    tm = _tile(length, 256)
    hs = MLA_HEADS * LANES
    cos, s_lo, s_hi = rope_tabs
    tab_spec = pl.BlockSpec((tm, LANES), lambda b, i: (i, 0))
    tok = lambda w: pl.BlockSpec((None, tm, w), lambda b, i: (b, i, 0))
    consts = [p["w_in"], p["q_norm"], p["w_q"], p["kv_norm"], p["w_k"], p["w_kr"], p["w_v"],
              p["gain_q"], p["gain_k"], p["v_one"], p["k_one"]]
    return pl.pallas_call(
        _mla_in_kernel,
        grid=(bsz, length // tm),
        in_specs=([tok(d), _mod_spec(mods)] + [_const_spec(c.shape) for c in consts]
                  + [tab_spec, tab_spec, tab_spec]),
        out_specs=[tok(hs), tok(hs), tok(hs), tok(CONV_CH)],
        out_shape=[jax.ShapeDtypeStruct((bsz, length, hs), BF16),
                   jax.ShapeDtypeStruct((bsz, length, hs), BF16),
                   jax.ShapeDtypeStruct((bsz, length, hs), BF16),
                   jax.ShapeDtypeStruct((bsz, length, CONV_CH), F32)],
        compiler_params=_params("parallel", "parallel"),
        name="mla_in",
    )(x, mods, *consts, cos, s_lo, s_hi)


def _attn_kernel(q_ref, k_ref, v_ref, o_ref, *, tk):
    tq = q_ref.shape[0]
    n_chunks = k_ref.shape[0] // tk
    heads = [slice(j * LANES, (j + 1) * LANES) for j in range(q_ref.shape[1] // LANES)]
    qs = [q_ref[:, cols] for cols in heads]

    def rows(c):
        return pl.ds(pl.multiple_of(c * tk, tk), tk)

    def scores(c):
        return tuple(_dot_nt(q, k_ref[rows(c), cols]) for q, cols in zip(qs, heads))

    def consume(c, s_all, carry):
        out = []
        for s, cols, (m, acc) in zip(s_all, heads, carry):
            m_new = jnp.maximum(m, jnp.max(s, axis=-1, keepdims=True))
            p = jnp.exp2(s - m_new).astype(BF16)
            acc = jnp.exp2(m - m_new) * acc + jnp.dot(p, v_ref[rows(c), cols], preferred_element_type=F32)
            out.append((m_new, acc))
        return tuple(out)

    def body(c, carry):
        s_all, state = carry
        return scores(c + 1), consume(c, s_all, state)

    init = tuple((jnp.full((tq, 1), -jnp.inf, F32), jnp.zeros((tq, LANES), F32)) for _ in heads)
    s_last, state = lax.fori_loop(0, n_chunks - 1, body, (scores(0), init))
    for cols, (_, acc) in zip(heads, consume(n_chunks - 1, s_last, state)):
        o_ref[:, cols] = (acc * (1.0 / acc[:, MLA_V:MLA_V + 1])).astype(BF16)


def _attn_bounded_kernel(q_ref, kc_ref, vtc_ref, kl_ref, vtl_ref, kmax_ref, o_ref, *, tk):
    tq = q_ref.shape[0]
    heads = [slice(j * LANES, (j + 1) * LANES) for j in range(q_ref.shape[1] // LANES)]
    lane = lax.broadcasted_iota(jnp.int32, (tq, LANES), 1)
    qs = []
    for cols in heads:
        q = q_ref[:, cols].astype(F32)
        bound = jnp.sqrt(jnp.sum(q * q, axis=-1, keepdims=True)) * kmax_ref[:, cols]
        qs.append(jnp.where(lane == SHIFT_LANE, -bound, q).astype(BF16))

    v_rows = -(-(MLA_V + 1) // 16) * 16

    def add_keys(accs, k_ref, vt_ref, rows, chunk):
        out = []
        for q, cols, acc in zip(qs, heads, accs):
            p_t = jnp.exp2(_dot_nt(k_ref[rows, cols], q)).astype(BF16)
            v_t = vt_ref[chunk, cols.start:cols.start + v_rows, :]
            out.append(acc + jnp.dot(v_t, p_t, preferred_element_type=F32))
        return tuple(out)

    accs = add_keys(tuple(jnp.zeros((v_rows, tq), F32) for _ in heads), kc_ref, vtc_ref, slice(None), 0)
    accs = lax.fori_loop(0, kl_ref.shape[0] // tk,
                         lambda c, a: add_keys(a, kl_ref, vtl_ref, pl.ds(pl.multiple_of(c * tk, tk), tk), c), accs)
    pad = jnp.zeros((LANES - v_rows, tq), F32)
    for cols, acc_t in zip(heads, accs):
        out_t = jnp.concatenate([acc_t * (1.0 / acc_t[MLA_V:MLA_V + 1, :]), pad], axis=0)
        o_ref[:, cols] = out_t.T.astype(BF16)


def _attn_call(kernel, name, q, keys_values, extra=()):
    bsz, lq, hs = q.shape
    tq = _tile(lq, ATTN_TQ)
    group = ATTN_HEADS * LANES
    blk = lambda rows, imap: pl.BlockSpec((None, rows, group), imap)
    whole = lambda b, h, i: (b, 0, h)
    return pl.pallas_call(
        kernel,
        grid=(bsz, hs // group, lq // tq),
        in_specs=([blk(tq, lambda b, h, i: (b, i, h))] + [blk(a.shape[1], whole) for a in keys_values]
                  + [blk(1, whole) for _ in extra]),
        out_specs=blk(tq, lambda b, h, i: (b, i, h)),
        out_shape=jax.ShapeDtypeStruct((bsz, lq, hs), BF16),
        compiler_params=_params("parallel", "parallel", "parallel"),
        name=name,
    )(q, *keys_values, *extra)


def _attention_online(q, k, v):
    lk = k.shape[1]
    tk = next(t for t in (ATTN_TK, 512, 256, lk) if lk % t == 0)
    return _attn_call(functools.partial(_attn_kernel, tk=tk), "attention", q, (k, v))


def _attention_latent(q, k_c, v_c, k_l, v_l, q_norm_bound, k_norm_bound):
    tk = next(t for t in (ATTN_TK, 512, 256, k_l.shape[1]) if k_l.shape[1] % t == 0)
    k_max_slots = jnp.full((q.shape[0], 1, q.shape[-1]), k_norm_bound, F32)
    cat = lambda a, b: jnp.concatenate([a, b], axis=1)

    def bounded(q, k_c, v_c, k_l, v_l, k_max_slots):
        bsz, lq, hs = q.shape
        tq = _tile(lq, ATTN_TQ)
        group = ATTN_HEADS * LANES
        chunks_t = lambda v, t: v.reshape(bsz, v.shape[1] // t, t, hs).transpose(0, 1, 3, 2)
        vt_c, vt_l = chunks_t(v_c, v_c.shape[1]), chunks_t(v_l, tk)
        blk = lambda rows, imap: pl.BlockSpec((None, rows, group), imap)
        blk_t = lambda a: pl.BlockSpec((None, a.shape[1], group, a.shape[3]), lambda b, h, i: (b, 0, h, 0),
                                       pipeline_mode=pl.Buffered(1))
        whole = lambda b, h, i: (b, 0, h)
        keys = lambda a: pl.BlockSpec((None, a.shape[1], group), whole, pipeline_mode=pl.Buffered(1))
        return pl.pallas_call(
            functools.partial(_attn_bounded_kernel, tk=tk),
            grid=(bsz, hs // group, lq // tq),
            in_specs=[blk(tq, lambda b, h, i: (b, i, h)), keys(k_c), blk_t(vt_c), keys(k_l), blk_t(vt_l),
                      blk(1, whole)],
            out_specs=blk(tq, lambda b, h, i: (b, i, h)),
            out_shape=jax.ShapeDtypeStruct((bsz, lq, hs), BF16),
            compiler_params=_params("parallel", "parallel", "parallel"),
            name="attention_bounded",
        )(q, k_c, vt_c, k_l, vt_l, k_max_slots)

    def online(q, k_c, v_c, k_l, v_l, _):
        return _attention_online(q, cat(k_c, k_l), cat(v_c, v_l))

    return lax.cond(q_norm_bound * k_norm_bound <= ATTN_BOUND_LIMIT, bounded, online,
                    q, k_c, v_c, k_l, v_l, k_max_slots)


def _conv_kernel(h_ref, hp_ref, hn_ref, w_ref, b_ref, g_ref, be_ref, o_ref, sh_ref, *, rows):
    i = pl.program_id(1)
    tm = h_ref.shape[0]
    sh_ref[0, 0:CONV_HALO, :] = jnp.where(i > 0, hp_ref[...], 0.0)
    sh_ref[0, CONV_HALO:CONV_HALO + tm, :] = h_ref[...]
    sh_ref[0, CONV_HALO + tm:, :] = jnp.where(i < pl.num_programs(1) - 1, hn_ref[...], 0.0)
    first = CONV_HALO - CONV_K // 2
    used = tm + 2 * CONV_HALO - SUBLANES
    for s in range(1, SUBLANES):
        sh_ref[s, 0:used, :] = sh_ref[0, s:s + used, :]
    for r0 in range(0, tm, rows):
        acc = jnp.zeros((rows, CONV_CH), F32)
        for j in range(CONV_K):
            shift, base = (first + j) % SUBLANES, (first + j) // SUBLANES * SUBLANES
            acc += sh_ref[shift, r0 + base:r0 + base + rows, :] * w_ref[j:j + 1, :]
        acc += b_ref[...]
        xc = acc - jnp.mean(acc, axis=-1, keepdims=True)
        y = xc * lax.rsqrt(jnp.mean(xc * xc, axis=-1, keepdims=True) + NORM_EPS)
        y = y * g_ref[...] + be_ref[...]
        o_ref[r0:r0 + rows, :] = (y * _sigmoid(y)).astype(BF16)


def _conformer_conv(hg, p):
    bsz, length, ch = hg.shape
    tm = _tile(length, 512)
    nh = tm // CONV_HALO
    last = length // CONV_HALO - 1
    return pl.pallas_call(
        functools.partial(_conv_kernel, rows=64),
        grid=(bsz, length // tm),
        in_specs=[pl.BlockSpec((None, tm, ch), lambda b, i: (b, i, 0)),
                  pl.BlockSpec((None, CONV_HALO, ch), lambda b, i: (b, jnp.maximum(i * nh - 1, 0), 0)),
                  pl.BlockSpec((None, CONV_HALO, ch), lambda b, i: (b, jnp.minimum((i + 1) * nh, last), 0)),
                  _const_spec((CONV_K, ch)), _const_spec((1, ch)), _const_spec((1, ch)), _const_spec((1, ch))],
        out_specs=pl.BlockSpec((None, tm, ch), lambda b, i: (b, i, 0)),
        out_shape=jax.ShapeDtypeStruct((bsz, length, ch), BF16),
        scratch_shapes=[pltpu.VMEM((SUBLANES, tm + 2 * CONV_HALO, ch), F32)],
        compiler_params=_params("parallel", "parallel"),
        name="conformer_conv",
    )(hg, hg, hg, p["dw_w"], p["dw_b"], p["cn_g"], p["cn_b"])


def _mix_delta(a_ref, c_ref, wa_ref, wc_ref):
    return (jnp.dot(a_ref[...], wa_ref[...], preferred_element_type=F32)
            + jnp.dot(c_ref[...], wc_ref[...], preferred_element_type=F32))


def _rwkv_in_kernel(x_ref, xp_ref, xn_ref, m_ref, xmix_ref, wr_ref, wk_ref, wv_ref, w0_ref, w1_ref, w2_ref,
                    a0_ref, a1_ref, a2_ref, g1_ref, g2_ref, kk_ref, ka_ref, rk_ref,
                    r_out, v_out, kkn_out, g_out, bonus_out, lw_out, kd_out, b_out):
    i = pl.program_id(1)
    tm = x_ref.shape[0]
    h = _modulate(x_ref[...], m_ref, 3)
    h_before = jnp.where(i > 0, _modulate(xp_ref[7:8, :], m_ref, 3), 0.0)
    h_after = jnp.where(i < pl.num_programs(1) - 1, _modulate(xn_ref[0:1, :], m_ref, 3), 0.0)
    row = lax.broadcasted_iota(jnp.int32, (tm, 1), 0)
    h_prev = jnp.where(row == 0, h_before, pltpu.roll(h, 1, 0))
    h_next = jnp.where(row == tm - 1, h_after, pltpu.roll(h, tm - 1, 0))
    xx = 0.5 * (h_prev + h_next) - h
    xr, xw, xk, xv, xa, xg = (h + xx * xmix_ref[j:j + 1, :] for j in range(6))
    r = _dot(xr, wr_ref[...])
    k = _dot(xk, wk_ref[...])
    v = _dot(xv, wv_ref[...])
    kk = k * kk_ref[...]
    kk = kk * (1.0 / jnp.maximum(jnp.sqrt(_head_sums(kk * kk)), 1e-12))
    g_out[...] = _dot(_sigmoid(_dot(xg, g1_ref[...])), g2_ref[...]).astype(BF16)
    k_ka = k * ka_ref[...]
    k_sum = None
    for d in range(2):
        w_pre = w0_ref[d:d + 1, :] + _dot(jnp.tanh(_dot(xw, w1_ref[d])), w2_ref[d])
        lw_out[d] = -DECAY_SCALE * _sigmoid(w_pre)
        a = _sigmoid(a0_ref[d:d + 1, :] + _dot(_dot(xa, a1_ref[d]), a2_ref[d]))
        k_d = k + k_ka * (a - 1.0)
        kd_out[d] = k_d.astype(BF16)
        b_out[d] = (kk * a).astype(BF16)
        k_sum = k_d if k_sum is None else k_sum + k_d
    r_out[...] = r.astype(BF16)
    v_out[...] = v.astype(BF16)
    kkn_out[...] = kk.astype(BF16)
    bonus_out[...] = (_head_sums(r * (0.5 * k_sum) * rk_ref[...]) * v).astype(BF16)


def _rwkv_in(x, mods, p):
    bsz, length, d = x.shape
    tm = _tile(length, 512)
    nb = tm // 8
    last = length // 8 - 1
    tok = pl.BlockSpec((None, tm, d), lambda b, i: (b, i, 0))
    tok2 = pl.BlockSpec((None, 2, tm, d), lambda b, i: (b, 0, i, 0))
    consts = [p[n] for n in ("x_mix", "w_r", "w_k", "w_v", "w0", "w1", "w2", "a0", "a1", "a2", "g1", "g2",
                             "k_k", "k_a", "r_k")]
    one = jax.ShapeDtypeStruct((bsz, length, d), BF16)
    two = jax.ShapeDtypeStruct((bsz, 2, length, d), BF16)
    log_decay = jax.ShapeDtypeStruct((bsz, 2, length, d), F32)
    return pl.pallas_call(
        _rwkv_in_kernel,
        grid=(bsz, length // tm),
        in_specs=([tok,
                   pl.BlockSpec((None, 8, d), lambda b, i: (b, jnp.maximum(i * nb - 1, 0), 0)),
                   pl.BlockSpec((None, 8, d), lambda b, i: (b, jnp.minimum((i + 1) * nb, last), 0)),
                   _mod_spec(mods)] + [_const_spec(c.shape) for c in consts]),
        out_specs=[tok, tok, tok, tok, tok, tok2, tok2, tok2],
        out_shape=[one, one, one, one, one, log_decay, two, two],
        compiler_params=_params("parallel", "parallel"),
        name="rwkv_in",
    )(x, x, x, mods, *consts)


def _scan_kernel(r_ref, v_ref, kk_ref, lw_ref, kd_ref, b_ref, s0_ref, y_ref, st_ref, state_ref):
    d = pl.program_id(1)
    c = pl.program_id(2)
    n_rows, cs = lw_ref.shape[0], lw_ref.shape[1]
    n_pairs = state_ref.shape[1]

    @pl.when(c == 0)
    def _():
        state_ref[...] = s0_ref[...]

    rev = d == 1
    sign = 1 - 2 * d
    ti = lax.broadcasted_iota(jnp.int32, (cs, cs), 0)
    si = lax.broadcasted_iota(jnp.int32, (cs, cs), 1)
    incl = jnp.where((ti - si) * sign >= 0, 1.0, 0.0).astype(BF16)

    def scaled_streams(i):
        lw = lw_ref[i]
        l1 = lw.astype(BF16)
        rem = lw - l1.astype(F32)
        l2 = rem.astype(BF16)
        l3 = (rem - l2.astype(F32)).astype(BF16)
        cum = (jnp.dot(incl, l1, preferred_element_type=F32) + jnp.dot(incl, l2, preferred_element_type=F32)
               + jnp.dot(incl, l3, preferred_element_type=F32))
        total = jnp.where(rev, cum[0:1, :], cum[cs - 1:cs, :])
        p_inv = jnp.exp(-cum)
        to_end = jnp.exp(total - cum)
        kd = kd_ref[i]
        bv = b_ref[i]
        return dict(r_t=r_ref[i] * jnp.exp(cum), a_t=-kk_ref[i] * jnp.exp(cum - lw), k_t=kd * p_inv,
                    b_t=bv * p_inv, k_e=kd * to_end, b_e=bv * to_end, v=v_ref[i], p_tot=jnp.exp(total))

    rows = [scaled_streams(i) for i in range(n_rows)]

    assert cs & (cs - 1) == 0
    t2 = lax.broadcasted_iota(jnp.int32, (2 * cs, 2 * cs), 0) & (cs - 1)
    s2 = lax.broadcasted_iota(jnp.int32, (2 * cs, 2 * cs), 1) & (cs - 1)
    strict2 = (t2 - s2) * sign > 0
    incl2 = (t2 - s2) * sign >= 0
    even = lax.broadcasted_iota(jnp.int32, (cs, LANES), 1) < RWKV_HEAD
    vi = lax.broadcasted_iota(jnp.int32, (LANES, LANES), 0) < RWKV_HEAD
    ki = lax.broadcasted_iota(jnp.int32, (LANES, LANES), 1) < RWKV_HEAD
    same_head = vi == ki

    def stack(t):
        return jnp.concatenate([jnp.where(even, t, 0.0), jnp.where(even, 0.0, t)], axis=0)

    def twice(t):
        return jnp.concatenate([t, t], axis=0)

    bf = lambda t: t.astype(BF16)
    mm = lambda a, b: jnp.dot(a, b, preferred_element_type=F32)
    chains = [(i, p, slice(p * LANES, (p + 1) * LANES)) for i in range(n_rows) for p in range(n_pairs)]
    idx = range(len(chains))
    col = lambda name: [rows[i][name][:, sl] for i, _, sl in chains]
    a_t, r_t, b_t, k_t, b_e, k_e, v, p_tot = (col(n) for n in ("a_t", "r_t", "b_t", "k_t", "b_e", "k_e", "v", "p_tot"))
    eye = jnp.where(lax.broadcasted_iota(jnp.int32, (2 * cs, 2 * cs), 0)
                    == lax.broadcasted_iota(jnp.int32, (2 * cs, 2 * cs), 1), 1.0, 0.0)

    cross = [_dot_nt(jnp.concatenate([stack(a_t[n]), stack(r_t[n])], axis=0),
                     jnp.concatenate([stack(b_t[n]), stack(k_t[n])], axis=0)) for n in idx]
    m_ab = [jnp.where(strict2, x[:2 * cs, :2 * cs], 0.0) for x in cross]
    m_ak = [bf(jnp.where(strict2, x[:2 * cs, 2 * cs:], 0.0)) for x in cross]
    m_rb = [bf(jnp.where(incl2, x[2 * cs:, :2 * cs], 0.0)) for x in cross]
    m_rk = [bf(jnp.where(incl2, x[2 * cs:, 2 * cs:], 0.0)) for x in cross]
    v2 = [bf(twice(t)) for t in v]
    x_v = [mm(m_ak[n], v2[n]) for n in idx]
    y_v = [mm(m_rk[n], v2[n]) for n in idx]
    upd_v = [_dot_tn(v2[n], stack(k_e[n])) for n in idx]
    pw = [bf(m) for m in m_ab]
    t_inv = [eye + m for m in m_ab]
    for _ in range(max(1, (cs - 1).bit_length()) - 1):
        pw = [bf(mm(w, w)) for w in pw]
        t_inv = [t_inv[n] + mm(pw[n], bf(t_inv[n])) for n in idx]
    t_inv = [bf(t) for t in t_inv]

    states = [state_ref[i, p] for i, p, _ in chains]
    ar = [_dot_nt(jnp.concatenate([a_t[n], r_t[n]], axis=0), states[n]) for n in idx]
    u = [bf(mm(t_inv[n], bf(twice(ar[n][:cs]) + x_v[n]))) for n in idx]
    for n, (i, p, sl) in enumerate(chains):
        y2 = twice(ar[n][cs:]) + y_v[n] + mm(m_rb[n], u[n])
        y_ref[i, :, sl] = jnp.where(even, y2[:cs], y2[cs:])
        upd = upd_v[n] + _dot_tn(u[n], stack(b_e[n]))
        state_ref[i, p] = states[n] * p_tot[n] + jnp.where(same_head, upd, 0.0)

    @pl.when(c == pl.num_programs(2) - 1)
    def _():
        st_ref[...] = state_ref[...]


def _wkv_scan(r, v, kk, lw, kd, bv, s0):
    bsz, length, d = r.shape
    cs = SCAN_CHUNK
    nc = length // cs
    n_pairs = d // LANES
    nr = SCAN_ROWS if bsz % SCAN_ROWS == 0 else 1
    chunk = lambda d_, c: c + d_ * (nc - 1 - 2 * c)
    shared = pl.BlockSpec((nr, cs, d), lambda b, d_, c: (b, chunk(d_, c), 0))
    per_dir = pl.BlockSpec((nr, None, cs, d), lambda b, d_, c: (b, d_, chunk(d_, c), 0))
    st_spec = pl.BlockSpec((nr, None, n_pairs, LANES, LANES), lambda b, d_, c: (b, d_, 0, 0, 0))
    return pl.pallas_call(
        _scan_kernel,
        grid=(bsz // nr, 2, nc),
        in_specs=[shared, shared, shared, per_dir, per_dir, per_dir, st_spec],
        out_specs=[per_dir, st_spec],
        out_shape=[jax.ShapeDtypeStruct((bsz, 2, length, d), F32),
                   jax.ShapeDtypeStruct(s0.shape, F32)],
        scratch_shapes=[pltpu.VMEM((nr, n_pairs, LANES, LANES), F32)],
        compiler_params=_params("parallel", "parallel", "arbitrary"),
        name="wkv_scan",
    )(r, v, kk, lw, kd, bv, s0)


def _rwkv_delta(y_ref, bonus_ref, g_ref, lng_ref, lnb_ref, wo_ref):
    y = y_ref[0] + y_ref[1]
    inv_n = 1.0 / RWKV_HEAD
    yc = y - _head_sums(y) * inv_n
    rstd = lax.rsqrt(_head_sums(yc * yc) * inv_n + GN_EPS)
    yn = yc * rstd * lng_ref[...] + lnb_ref[...]
    return _dot((yn + bonus_ref[...]) * g_ref[...], wo_ref[...])


def _head_slots(w, per_head, offset=0):
    k = w.shape[0]
    w = w.reshape(k, MLA_HEADS, per_head)
    w = jnp.pad(w, ((0, 0), (0, 0), (offset, LANES - per_head - offset)))
    return w.reshape(k, MLA_HEADS * LANES)


def _mla_params(w_in, q_norm, w_qb, kv_norm, w_kvb, qn_q, qn_k, dw_w, dw_b, cn_g, cn_b, w_out):
    d = w_in.shape[0]
    o_kr = MLA_Q_RANK + MLA_KV_RANK
    mla_in = o_kr + MLA_ROPE
    w_in_p = jnp.concatenate([w_in[:, :mla_in], jnp.zeros((d, LANES - MLA_ROPE), F32), w_in[:, mla_in:]], axis=1)
    kvb = w_kvb.reshape(MLA_KV_RANK, MLA_HEADS, MLA_NOPE + MLA_V)
    w_k = _head_slots(kvb[:, :, :MLA_NOPE].reshape(MLA_KV_RANK, -1), MLA_NOPE)
    w_v = _head_slots(kvb[:, :, MLA_NOPE:].reshape(MLA_KV_RANK, -1), MLA_V)
    v_one = jnp.tile(jnp.arange(LANES) == MLA_V, MLA_HEADS).astype(F32)[None, :]
    k_one = jnp.tile(jnp.arange(LANES) == SHIFT_LANE, MLA_HEADS).astype(BF16)[None, :]
    w_out_att = jnp.pad(w_out[:MLA_HEADS * MLA_V].reshape(MLA_HEADS, MLA_V, d), ((0, 0), (0, LANES - MLA_V), (0, 0)))
    place = jnp.pad(jnp.eye(MLA_ROPE, dtype=F32), ((0, LANES - MLA_ROPE), (MLA_NOPE, LANES - QK_DIM)))
    w_kr = jnp.tile(place, (1, MLA_HEADS))
    pad_gain = lambda g: jnp.tile(jnp.pad(g, (0, LANES - QK_DIM)), MLA_HEADS)[None, :]
    norm_bound = lambda g: math.sqrt(QK_DIM) * BOUND_SLACK * jnp.max(jnp.abs(g))
    mix = MLA_HEADS * MLA_V
    return dict(
        w_in=w_in_p.astype(BF16), q_norm=q_norm[None, :], w_q=_head_slots(w_qb, QK_DIM).astype(BF16),
        kv_norm=kv_norm[None, :], w_k=w_k.astype(BF16), w_kr=w_kr.astype(BF16), w_v=w_v.astype(BF16),
        gain_q=pad_gain(qn_q) * (ATTN_SCALE * LOG2_E), gain_k=pad_gain(qn_k), v_one=v_one, k_one=k_one,
        q_norm_bound=norm_bound(qn_q) * (ATTN_SCALE * LOG2_E), k_norm_bound=norm_bound(qn_k),
        dw_w=dw_w[:, 0, :], dw_b=dw_b[None, :], cn_g=cn_g[None, :], cn_b=cn_b[None, :],
        w_out_att=w_out_att.reshape(MLA_HEADS * LANES, d).astype(BF16), w_out_conv=w_out[mix:].astype(BF16))


def _rope_tables(length, with_rope):
    n_freq = MLA_ROPE // 4
    if not with_rope:
        return jnp.ones((length, LANES), F32), jnp.zeros((length, LANES), F32), jnp.zeros((length, LANES), F32)
    t = jnp.arange(length)
    inv = ROPE_THETA ** (-jnp.arange(n_freq, dtype=F32) / n_freq)
    ang_r = (t // GRID_W).astype(F32)[:, None] * inv
    ang_c = (t % GRID_W).astype(F32)[:, None] * inv
    zeros = jnp.zeros((length, n_freq), F32)

    def slot(nope_val, r_first, r_second, c_first, c_second):
        nope = jnp.full((length, MLA_NOPE), nope_val, F32)
        pad = jnp.full((length, LANES - QK_DIM), nope_val, F32)
        return jnp.concatenate([nope, r_first, r_second, c_first, c_second, pad], axis=1)

    cos_r, sin_r, cos_c, sin_c = jnp.cos(ang_r), jnp.sin(ang_r), jnp.cos(ang_c), jnp.sin(ang_c)
    return (slot(1.0, cos_r, cos_r, cos_c, cos_c),
            slot(0.0, -sin_r, zeros, -sin_c, zeros),
            slot(0.0, zeros, sin_r, zeros, sin_c))


def _rwkv_params(x_mix, w_r, w_k, w_v, w0, w1, w2, a0, a1, a2, g1, g2, k_k, k_a, r_k, ln_g, ln_b, w_o):
    d = w_r.shape[0]
    bf = lambda w: w.astype(BF16)
    return dict(x_mix=x_mix, w_r=bf(w_r), w_k=bf(w_k), w_v=bf(w_v), w0=w0, w1=bf(w1), w2=bf(w2), a0=a0,
                a1=bf(a1), a2=bf(a2), g1=bf(g1), g2=bf(g2), k_k=k_k[None, :], k_a=k_a[None, :],
                r_k=r_k.reshape(1, d), ln_g=ln_g[None, :], ln_b=ln_b[None, :], w_o=bf(w_o))


def _mla_conv_layer(x, ctx, m_l, m_c, p):
    length, lc = x.shape[1], ctx.shape[1]
    q_l, k_l, v_l, hg_l = _mla_in(x, m_l, p, _rope_tables(length, True))
    q_c, k_c, v_c, hg_c = _mla_in(ctx, m_c, p, _rope_tables(lc, False))
    att_l = _attention_latent(q_l, k_c, v_c, k_l, v_l, p["q_norm_bound"], p["k_norm_bound"])
    att_c = _attention_online(q_c, k_c, v_c)
    consts = (p["w_out_att"], p["w_out_conv"])
    return (dict(delta=_mix_delta, tokens=(att_l, _conformer_conv(hg_l, p)), consts=consts),
            dict(delta=_mix_delta, tokens=(att_c, _conformer_conv(hg_c, p)), consts=consts))


def _rwkv_layer(x, ctx, m_l, m_c, p):
    bsz, _, d = x.shape
    r_c, v_c, kk_c, _, _, lw_c, kd_c, b_c = _rwkv_in(ctx, m_c, p)
    r_l, v_l, kk_l, g_l, bonus_l, lw_l, kd_l, b_l = _rwkv_in(x, m_l, p)
    s0 = jnp.zeros((bsz, 2, d // LANES, LANES, LANES), F32)
    _, s_ctx = _wkv_scan(r_c, v_c, kk_c, lw_c, kd_c, b_c, s0)
    y_l, _ = _wkv_scan(r_l, v_l, kk_l, lw_l, kd_l, b_l, s_ctx)
    return dict(delta=_rwkv_delta, tokens=(y_l, bonus_l, g_l), consts=(p["ln_g"], p["ln_b"], p["w_o"]))


def kernel(x, c, ctx, c_ctx, ada_w, ada_b, ffn_w1, ffn_w3, ffn_w2, mla_w_in, mla_q_norm, mla_w_qb, mla_kv_norm, mla_w_kvb, mla_qk_norm_q, mla_qk_norm_k, conv_dw_w, conv_dw_b, conv_norm_g, conv_norm_b, mix_w_out, rwkv_x_mix, rwkv_w_r, rwkv_w_k, rwkv_w_v, rwkv_w0, rwkv_w1, rwkv_w2, rwkv_a0, rwkv_a1, rwkv_a2, rwkv_g1, rwkv_g2, rwkv_k_k, rwkv_k_a, rwkv_r_k, rwkv_ln_g, rwkv_ln_b, rwkv_w_o):
    bsz = x.shape[0]
    depth = ada_w.shape[0]
    assert depth == 2, "layer 0 is the MLA/conv layer, layer 1 the (last) RWKV layer"
    cond = jnp.concatenate([c, c_ctx[None, :], jnp.zeros((16 - bsz - 1, c.shape[1]), F32)], axis=0)
    w1, w3, w2 = ffn_w1.astype(BF16), ffn_w3.astype(BF16), ffn_w2.astype(BF16)
    mla_p = _mla_params(mla_w_in[0], mla_q_norm[0], mla_w_qb[0], mla_kv_norm[0], mla_w_kvb[0], mla_qk_norm_q[0],
                        mla_qk_norm_k[0], conv_dw_w[0], conv_dw_b[0], conv_norm_g[0], conv_norm_b[0], mix_w_out[0])
    rwkv_p = _rwkv_params(rwkv_x_mix[0], rwkv_w_r[0], rwkv_w_k[0], rwkv_w_v[0], rwkv_w0[0], rwkv_w1[0],
                          rwkv_w2[0], rwkv_a0[0], rwkv_a1[0], rwkv_a2[0], rwkv_g1[0], rwkv_g2[0], rwkv_k_k[0],
                          rwkv_k_a[0], rwkv_r_k[0], rwkv_ln_g[0], rwkv_ln_b[0], rwkv_w_o[0])
    for i in range(depth):
        mods = _ada_mods(cond, ada_w[i], ada_b[i])
        m_l, m_c = mods[:bsz], mods[bsz:bsz + 1]
        x = _half_ffn(x, m_l, 0, w1[i, 0], w3[i, 0], w2[i, 0])
        ctx = _half_ffn(ctx, m_c, 0, w1[i, 0], w3[i, 0], w2[i, 0])
        if i == 0:
            mix_l, mix_c = _mla_conv_layer(x, ctx, m_l, m_c, mla_p)
            ctx = _half_ffn(ctx, m_c, 6, w1[i, 1], w3[i, 1], w2[i, 1], **mix_c)
        else:
            mix_l = _rwkv_layer(x, ctx, m_l, m_c, rwkv_p)
        x = _half_ffn(x, m_l, 6, w1[i, 1], w3[i, 1], w2[i, 1], **mix_l)
    return x
```

```python
import functools
import math

import jax
import jax.numpy as jnp
from jax import lax
from jax.experimental import pallas as pl
from jax.experimental.pallas import tpu as pltpu

F32 = jnp.float32
BF16 = jnp.bfloat16

LANES = 128
SUBLANES = 8
MXU_TILE = 256
VMEM_LIMIT = 56 * 1024 * 1024

NORM_EPS = 1e-6
N_MOD = 9
GRID_W = 64
ROPE_THETA = 10000.0

MLA_HEADS = 8
MLA_NOPE = 64
MLA_ROPE = 32
MLA_V = 64
MLA_Q_RANK = 384
MLA_KV_RANK = 128
QK_DIM = MLA_NOPE + MLA_ROPE
ATTN_SCALE = QK_DIM ** -0.5
LOG2_E = math.log2(math.e)
ATTN_TQ = 1024
ATTN_TK = 4096
ATTN_HEADS = 4
SHIFT_LANE = QK_DIM
BOUND_SLACK = 1.02
ATTN_BOUND_LIMIT = 40.0
CONV_CH = 512
CONV_K = 31
CONV_HALO = 16

RWKV_HEAD = 64
GN_EPS = RWKV_HEAD * 1e-5
SCAN_CHUNK = 64
SCAN_ROWS = 2
DECAY_SCALE = math.exp(-0.5)


def _dot(a, b):
    return jnp.dot(a.astype(BF16), b.astype(BF16), preferred_element_type=F32)


def _dot_nt(a, b):
    return lax.dot_general(a.astype(BF16), b.astype(BF16), (((1,), (1,)), ((), ())),
                           preferred_element_type=F32)


def _dot_tn(a, b):
    return lax.dot_general(a.astype(BF16), b.astype(BF16), (((0,), (0,)), ((), ())),
                           preferred_element_type=F32)


def _head_sums(x):
    even = lax.broadcasted_iota(jnp.int32, (1, LANES), 1) < RWKV_HEAD
    out = []
    for j in range(0, x.shape[1], LANES):
        u = x[:, j:j + LANES]
        both = jnp.sum(u, axis=-1, keepdims=True)
        first = jnp.sum(jnp.where(even, u, 0.0), axis=-1, keepdims=True)
        out.append(jnp.where(even, first, both - first))
    return jnp.concatenate(out, axis=1)


def _rms(x):
    return x * lax.rsqrt(jnp.mean(x * x, axis=-1, keepdims=True) + NORM_EPS)


def _sigmoid(x):
    return 1.0 / (1.0 + jnp.exp(-x))


def _modulate(x, m_ref, first):
    return _rms(x) * (1.0 + m_ref[first + 1:first + 2, :]) + m_ref[first:first + 1, :]


def _const_spec(shape):
    zeros = (0,) * len(shape)
    return pl.BlockSpec(shape, lambda *_: zeros, pipeline_mode=pl.Buffered(1))


def _params(*sem):
    return pltpu.CompilerParams(dimension_semantics=sem, vmem_limit_bytes=VMEM_LIMIT)


def _mod_spec(mods):
    d = mods.shape[-1]
    if mods.shape[0] == 1:
        return pl.BlockSpec((None, N_MOD, d), lambda b, i: (0, 0, 0))
    return pl.BlockSpec((None, N_MOD, d), lambda b, i: (b, 0, 0))


def _tile(length, pref):
    t = min(pref, length)
    assert length % t == 0, (length, t)
    return t


def _ada_kernel(c_ref, w_ref, b_ref, o_ref):
    c = c_ref[...]
    s = c * _sigmoid(c)
    w = w_ref[...]
    s_hi = s.astype(BF16)
    s_lo = (s - s_hi.astype(F32)).astype(BF16)
    w_hi = w.astype(BF16)
    w_lo = (w - w_hi.astype(F32)).astype(BF16)
    acc = jnp.dot(s_hi, w_hi, preferred_element_type=F32)
    acc += jnp.dot(s_hi, w_lo, preferred_element_type=F32)
    acc += jnp.dot(s_lo, w_hi, preferred_element_type=F32)
    o_ref[...] = acc + b_ref[...]


def _ada_mods(cond, w, b):
    r, d = cond.shape
    n = w.shape[1]
    out = pl.pallas_call(
        _ada_kernel,
        grid=(n // d,),
        in_specs=[pl.BlockSpec((r, d), lambda j: (0, 0)),
                  pl.BlockSpec((d, d), lambda j: (0, j)),
                  pl.BlockSpec((1, d), lambda j: (0, j))],
        out_specs=pl.BlockSpec((r, d), lambda j: (0, j)),
        out_shape=jax.ShapeDtypeStruct((r, n), F32),
        compiler_params=_params("parallel"),
        name="ada_mods",
    )(cond, w, b.reshape(1, n))
    return out.reshape(r, n // d, d)


def _ffn_kernel(x_ref, m_ref, *refs, first, f_split, delta, n_delta):
    delta_refs, (w1_ref, w3_ref, w2_ref, o_ref) = refs[:n_delta], refs[n_delta:]
    x = x_ref[...]
    if delta is not None:
        x = x + m_ref[5:6, :] * delta(*delta_refs)
    h = _modulate(x, m_ref, first).astype(BF16)
    acc = None
    for cols in (slice(0, f_split), slice(f_split, w1_ref.shape[1])):
        a = jnp.dot(h, w1_ref[:, cols], preferred_element_type=F32)
        b = jnp.dot(h, w3_ref[:, cols], preferred_element_type=F32)
        g = (a * _sigmoid(a) * b).astype(BF16)
        part = jnp.dot(g, w2_ref[cols, :], preferred_element_type=F32)
        acc = part if acc is None else acc + part
    o_ref[...] = x + (0.5 * m_ref[first + 2:first + 3, :]) * acc


def _half_ffn(x, mods, first, w1, w3, w2, delta=None, tokens=(), consts=()):
    bsz, length, d = x.shape
    f = w1.shape[1]
    tm = _tile(length, 512)
    assert f % MXU_TILE == 0
    f_split = (f // MXU_TILE + 1) // 2 * MXU_TILE

    def tok(a):
        if a.ndim == 3:
            return pl.BlockSpec((None, tm, a.shape[-1]), lambda b, i: (b, i, 0))
        return pl.BlockSpec((None, a.shape[1], tm, a.shape[-1]), lambda b, i: (b, 0, i, 0))

    return pl.pallas_call(
        functools.partial(_ffn_kernel, first=first, f_split=f_split, delta=delta, n_delta=len(tokens) + len(consts)),
        grid=(bsz, length // tm),
        in_specs=([tok(x), _mod_spec(mods)] + [tok(a) for a in tokens] + [_const_spec(c.shape) for c in consts]
                  + [_const_spec((d, f)), _const_spec((d, f)), _const_spec((f, d))]),
        out_specs=tok(x),
        out_shape=jax.ShapeDtypeStruct(x.shape, F32),
        compiler_params=_params("parallel", "parallel"),
        name="half_ffn",
    )(x, mods, *tokens, *consts, w1, w3, w2)


def _head_rms(t, width):
    slots = [t[:, j:j + LANES] for j in range(0, t.shape[1], LANES)]
    return jnp.concatenate(
        [u * lax.rsqrt(jnp.sum(u * u, axis=-1, keepdims=True) * (1.0 / width) + NORM_EPS) for u in slots], axis=1)


def _rope(t, cos, sin_lo, sin_hi):
    n = t.shape[1]
    rep = n // cos.shape[1]
    cos, sin_lo, sin_hi = (jnp.tile(u, (1, rep)) for u in (cos, sin_lo, sin_hi))
    return t * cos + pltpu.roll(t, n - 8, 1) * sin_lo + pltpu.roll(t, 8, 1) * sin_hi


def _mla_in_kernel(x_ref, m_ref, w_in_ref, qn_ref, wq_ref, kvn_ref, wk_ref, wkr_ref, wv_ref,
                   gq_ref, gk_ref, vone_ref, kone_ref, cos_ref, sl_ref, sh_ref,
                   q_ref, k_ref, v_ref, hg_ref):
    h = _modulate(x_ref[...], m_ref, 3)
    z = _dot(h, w_in_ref[...])
    o_kv = MLA_Q_RANK
    o_kr = o_kv + MLA_KV_RANK
    o_a = o_kr + LANES
    o_g = o_a + CONV_CH
    c_q = _rms(z[:, :o_kv]) * qn_ref[...]
    c_kv = _rms(z[:, o_kv:o_kr]) * kvn_ref[...]
    hg_ref[...] = z[:, o_a:o_g] * _sigmoid(z[:, o_g:o_g + CONV_CH])
    cos, s_lo, s_hi = cos_ref[...], sl_ref[...], sh_ref[...]
    q = _head_rms(_dot(c_q, wq_ref[...]), QK_DIM) * gq_ref[...]
    q_ref[...] = _rope(q, cos, s_lo, s_hi).astype(BF16)
    k = _dot(c_kv, wk_ref[...]) + _dot(z[:, o_kr:o_a], wkr_ref[...])
    k = _head_rms(k, QK_DIM) * gk_ref[...]
    k_ref[...] = _rope(k, cos, s_lo, s_hi).astype(BF16) + kone_ref[...]
    v_ref[...] = (_dot(c_kv, wv_ref[...]) + vone_ref[...]).astype(BF16)


def _mla_in(x, mods, p, rope_tabs):
    bsz, length, d = x.shape
    tm = _tile(length, 256)
    hs = MLA_HEADS * LANES
    cos, s_lo, s_hi = rope_tabs
    tab_spec = pl.BlockSpec((tm, LANES), lambda b, i: (i, 0))
    tok = lambda w: pl.BlockSpec((None, tm, w), lambda b, i: (b, i, 0))
    consts = [p["w_in"], p["q_norm"], p["w_q"], p["kv_norm"], p["w_k"], p["w_kr"], p["w_v"],
              p["gain_q"], p["gain_k"], p["v_one"], p["k_one"]]
    return pl.pallas_call(
        _mla_in_kernel,
        grid=(bsz, length // tm),
        in_specs=([tok(d), _mod_spec(mods)] + [_const_spec(c.shape) for c in consts]
                  + [tab_spec, tab_spec, tab_spec]),
        out_specs=[tok(hs), tok(hs), tok(hs), tok(CONV_CH)],
        out_shape=[jax.ShapeDtypeStruct((bsz, length, hs), BF16),
                   jax.ShapeDtypeStruct((bsz, length, hs), BF16),
                   jax.ShapeDtypeStruct((bsz, length, hs), BF16),
                   jax.ShapeDtypeStruct((bsz, length, CONV_CH), F32)],
        compiler_params=_params("parallel", "parallel"),
        name="mla_in",
    )(x, mods, *consts, cos, s_lo, s_hi)


def _attn_kernel(q_ref, k_ref, v_ref, o_ref, *, tk):
    tq = q_ref.shape[0]
    n_chunks = k_ref.shape[0] // tk
    heads = [slice(j * LANES, (j + 1) * LANES) for j in range(q_ref.shape[1] // LANES)]
    qs = [q_ref[:, cols] for cols in heads]

    def rows(c):
        return pl.ds(pl.multiple_of(c * tk, tk), tk)

    def scores(c):
        return tuple(_dot_nt(q, k_ref[rows(c), cols]) for q, cols in zip(qs, heads))

    def consume(c, s_all, carry):
        out = []
        for s, cols, (m, acc) in zip(s_all, heads, carry):
            m_new = jnp.maximum(m, jnp.max(s, axis=-1, keepdims=True))
            p = jnp.exp2(s - m_new).astype(BF16)
            acc = jnp.exp2(m - m_new) * acc + jnp.dot(p, v_ref[rows(c), cols], preferred_element_type=F32)
            out.append((m_new, acc))
        return tuple(out)

    def body(c, carry):
        s_all, state = carry
        return scores(c + 1), consume(c, s_all, state)

    init = tuple((jnp.full((tq, 1), -jnp.inf, F32), jnp.zeros((tq, LANES), F32)) for _ in heads)
    s_last, state = lax.fori_loop(0, n_chunks - 1, body, (scores(0), init))
    for cols, (_, acc) in zip(heads, consume(n_chunks - 1, s_last, state)):
        o_ref[:, cols] = (acc * (1.0 / acc[:, MLA_V:MLA_V + 1])).astype(BF16)


def _attn_bounded_kernel(q_ref, kc_ref, vtc_ref, kl_ref, vtl_ref, kmax_ref, o_ref, *, tk):
    tq = q_ref.shape[0]
    heads = [slice(j * LANES, (j + 1) * LANES) for j in range(q_ref.shape[1] // LANES)]
    lane = lax.broadcasted_iota(jnp.int32, (tq, LANES), 1)
    qs = []
    for cols in heads:
        q = q_ref[:, cols].astype(F32)
        bound = jnp.sqrt(jnp.sum(q * q, axis=-1, keepdims=True)) * kmax_ref[:, cols]
        qs.append(jnp.where(lane == SHIFT_LANE, -bound, q).astype(BF16))

    v_rows = -(-(MLA_V + 1) // 16) * 16

    def add_keys(accs, k_ref, vt_ref, rows, chunk):
        out = []
        for q, cols, acc in zip(qs, heads, accs):
            p_t = jnp.exp2(_dot_nt(k_ref[rows, cols], q)).astype(BF16)
            v_t = vt_ref[chunk, cols.start:cols.start + v_rows, :]
            out.append(acc + jnp.dot(v_t, p_t, preferred_element_type=F32))
        return tuple(out)

    accs = add_keys(tuple(jnp.zeros((v_rows, tq), F32) for _ in heads), kc_ref, vtc_ref, slice(None), 0)
    accs = lax.fori_loop(0, kl_ref.shape[0] // tk,
                         lambda c, a: add_keys(a, kl_ref, vtl_ref, pl.ds(pl.multiple_of(c * tk, tk), tk), c), accs)
    pad = jnp.zeros((LANES - v_rows, tq), F32)
    for cols, acc_t in zip(heads, accs):
        out_t = jnp.concatenate([acc_t * (1.0 / acc_t[MLA_V:MLA_V + 1, :]), pad], axis=0)
        o_ref[:, cols] = out_t.T.astype(BF16)


def _attn_call(kernel, name, q, keys_values, extra=()):
    bsz, lq, hs = q.shape
    tq = _tile(lq, ATTN_TQ)
    group = ATTN_HEADS * LANES
    blk = lambda rows, imap: pl.BlockSpec((None, rows, group), imap)
    whole = lambda b, h, i: (b, 0, h)
    return pl.pallas_call(
        kernel,
        grid=(bsz, hs // group, lq // tq),
        in_specs=([blk(tq, lambda b, h, i: (b, i, h))] + [blk(a.shape[1], whole) for a in keys_values]
                  + [blk(1, whole) for _ in extra]),
        out_specs=blk(tq, lambda b, h, i: (b, i, h)),
        out_shape=jax.ShapeDtypeStruct((bsz, lq, hs), BF16),
        compiler_params=_params("parallel", "parallel", "parallel"),
        name=name,
    )(q, *keys_values, *extra)


def _attention_online(q, k, v):
    lk = k.shape[1]
    tk = next(t for t in (ATTN_TK, 512, 256, lk) if lk % t == 0)
    return _attn_call(functools.partial(_attn_kernel, tk=tk), "attention", q, (k, v))


def _attention_latent(q, k_c, v_c, k_l, v_l, q_norm_bound, k_norm_bound):
    tk = next(t for t in (ATTN_TK, 512, 256, k_l.shape[1]) if k_l.shape[1] % t == 0)
    k_max_slots = jnp.full((q.shape[0], 1, q.shape[-1]), k_norm_bound, F32)
    cat = lambda a, b: jnp.concatenate([a, b], axis=1)

    def bounded(q, k_c, v_c, k_l, v_l, k_max_slots):
        bsz, lq, hs = q.shape
        tq = _tile(lq, ATTN_TQ)
        group = ATTN_HEADS * LANES
        chunks_t = lambda v, t: v.reshape(bsz, v.shape[1] // t, t, hs).transpose(0, 1, 3, 2)
        vt_c, vt_l = chunks_t(v_c, v_c.shape[1]), chunks_t(v_l, tk)
        blk = lambda rows, imap: pl.BlockSpec((None, rows, group), imap)
        blk_t = lambda a: pl.BlockSpec((None, a.shape[1], group, a.shape[3]), lambda b, h, i: (b, 0, h, 0),
                                       pipeline_mode=pl.Buffered(1))
        whole = lambda b, h, i: (b, 0, h)
        keys = lambda a: pl.BlockSpec((None, a.shape[1], group), whole, pipeline_mode=pl.Buffered(1))
        return pl.pallas_call(
            functools.partial(_attn_bounded_kernel, tk=tk),
            grid=(bsz, hs // group, lq // tq),
            in_specs=[blk(tq, lambda b, h, i: (b, i, h)), keys(k_c), blk_t(vt_c), keys(k_l), blk_t(vt_l),
                      blk(1, whole)],
            out_specs=blk(tq, lambda b, h, i: (b, i, h)),
            out_shape=jax.ShapeDtypeStruct((bsz, lq, hs), BF16),
            compiler_params=_params("parallel", "parallel", "parallel"),
            name="attention_bounded",
        )(q, k_c, vt_c, k_l, vt_l, k_max_slots)

    def online(q, k_c, v_c, k_l, v_l, _):
        return _attention_online(q, cat(k_c, k_l), cat(v_c, v_l))

    return lax.cond(q_norm_bound * k_norm_bound <= ATTN_BOUND_LIMIT, bounded, online,
                    q, k_c, v_c, k_l, v_l, k_max_slots)


def _conv_kernel(h_ref, hp_ref, hn_ref, w_ref, b_ref, g_ref, be_ref, o_ref, sh_ref, *, rows):
    i = pl.program_id(1)
    tm = h_ref.shape[0]
    sh_ref[0, 0:CONV_HALO, :] = jnp.where(i > 0, hp_ref[...], 0.0)
    sh_ref[0, CONV_HALO:CONV_HALO + tm, :] = h_ref[...]
    sh_ref[0, CONV_HALO + tm:, :] = jnp.where(i < pl.num_programs(1) - 1, hn_ref[...], 0.0)
    first = CONV_HALO - CONV_K // 2
    used = tm + 2 * CONV_HALO - SUBLANES
    for s in range(1, SUBLANES):
        sh_ref[s, 0:used, :] = sh_ref[0, s:s + used, :]
    for r0 in range(0, tm, rows):
        acc = jnp.zeros((rows, CONV_CH), F32)
        for j in range(CONV_K):
            shift, base = (first + j) % SUBLANES, (first + j) // SUBLANES * SUBLANES
            acc += sh_ref[shift, r0 + base:r0 + base + rows, :] * w_ref[j:j + 1, :]
        acc += b_ref[...]
        xc = acc - jnp.mean(acc, axis=-1, keepdims=True)
        y = xc * lax.rsqrt(jnp.mean(xc * xc, axis=-1, keepdims=True) + NORM_EPS)
        y = y * g_ref[...] + be_ref[...]
        o_ref[r0:r0 + rows, :] = (y * _sigmoid(y)).astype(BF16)


def _conformer_conv(hg, p):
    bsz, length, ch = hg.shape
    tm = _tile(length, 512)
    nh = tm // CONV_HALO
    last = length // CONV_HALO - 1
    return pl.pallas_call(
        functools.partial(_conv_kernel, rows=64),
        grid=(bsz, length // tm),
        in_specs=[pl.BlockSpec((None, tm, ch), lambda b, i: (b, i, 0)),
                  pl.BlockSpec((None, CONV_HALO, ch), lambda b, i: (b, jnp.maximum(i * nh - 1, 0), 0)),
                  pl.BlockSpec((None, CONV_HALO, ch), lambda b, i: (b, jnp.minimum((i + 1) * nh, last), 0)),
                  _const_spec((CONV_K, ch)), _const_spec((1, ch)), _const_spec((1, ch)), _const_spec((1, ch))],
        out_specs=pl.BlockSpec((None, tm, ch), lambda b, i: (b, i, 0)),
        out_shape=jax.ShapeDtypeStruct((bsz, length, ch), BF16),
        scratch_shapes=[pltpu.VMEM((SUBLANES, tm + 2 * CONV_HALO, ch), F32)],
        compiler_params=_params("parallel", "parallel"),
        name="conformer_conv",
    )(hg, hg, hg, p["dw_w"], p["dw_b"], p["cn_g"], p["cn_b"])


def _mix_delta(a_ref, c_ref, wa_ref, wc_ref):
    return (jnp.dot(a_ref[...], wa_ref[...], preferred_element_type=F32)
            + jnp.dot(c_ref[...], wc_ref[...], preferred_element_type=F32))


def _rwkv_in_kernel(x_ref, xp_ref, xn_ref, m_ref, xmix_ref, wr_ref, wk_ref, wv_ref, w0_ref, w1_ref, w2_ref,
                    a0_ref, a1_ref, a2_ref, g1_ref, g2_ref, kk_ref, ka_ref, rk_ref,
                    r_out, v_out, kkn_out, g_out, bonus_out, lw_out, kd_out, b_out):
    i = pl.program_id(1)
    tm = x_ref.shape[0]
    h = _modulate(x_ref[...], m_ref, 3)
    h_before = jnp.where(i > 0, _modulate(xp_ref[7:8, :], m_ref, 3), 0.0)
    h_after = jnp.where(i < pl.num_programs(1) - 1, _modulate(xn_ref[0:1, :], m_ref, 3), 0.0)
    row = lax.broadcasted_iota(jnp.int32, (tm, 1), 0)
    h_prev = jnp.where(row == 0, h_before, pltpu.roll(h, 1, 0))
    h_next = jnp.where(row == tm - 1, h_after, pltpu.roll(h, tm - 1, 0))
    xx = 0.5 * (h_prev + h_next) - h
    xr, xw, xk, xv, xa, xg = (h + xx * xmix_ref[j:j + 1, :] for j in range(6))
    r = _dot(xr, wr_ref[...])
    k = _dot(xk, wk_ref[...])
    v = _dot(xv, wv_ref[...])
    kk = k * kk_ref[...]
    kk = kk * (1.0 / jnp.maximum(jnp.sqrt(_head_sums(kk * kk)), 1e-12))
    g_out[...] = _dot(_sigmoid(_dot(xg, g1_ref[...])), g2_ref[...]).astype(BF16)
    k_ka = k * ka_ref[...]
    k_sum = None
    for d in range(2):
        w_pre = w0_ref[d:d + 1, :] + _dot(jnp.tanh(_dot(xw, w1_ref[d])), w2_ref[d])
        lw_out[d] = -DECAY_SCALE * _sigmoid(w_pre)
        a = _sigmoid(a0_ref[d:d + 1, :] + _dot(_dot(xa, a1_ref[d]), a2_ref[d]))
        k_d = k + k_ka * (a - 1.0)
        kd_out[d] = k_d.astype(BF16)
        b_out[d] = (kk * a).astype(BF16)
        k_sum = k_d if k_sum is None else k_sum + k_d
    r_out[...] = r.astype(BF16)
    v_out[...] = v.astype(BF16)
    kkn_out[...] = kk.astype(BF16)
    bonus_out[...] = (_head_sums(r * (0.5 * k_sum) * rk_ref[...]) * v).astype(BF16)


def _rwkv_in(x, mods, p):
    bsz, length, d = x.shape
    tm = _tile(length, 512)
    nb = tm // 8
    last = length // 8 - 1
    tok = pl.BlockSpec((None, tm, d), lambda b, i: (b, i, 0))
    tok2 = pl.BlockSpec((None, 2, tm, d), lambda b, i: (b, 0, i, 0))
    consts = [p[n] for n in ("x_mix", "w_r", "w_k", "w_v", "w0", "w1", "w2", "a0", "a1", "a2", "g1", "g2",
                             "k_k", "k_a", "r_k")]
    one = jax.ShapeDtypeStruct((bsz, length, d), BF16)
    two = jax.ShapeDtypeStruct((bsz, 2, length, d), BF16)
    log_decay = jax.ShapeDtypeStruct((bsz, 2, length, d), F32)
    return pl.pallas_call(
        _rwkv_in_kernel,
        grid=(bsz, length // tm),
        in_specs=([tok,
                   pl.BlockSpec((None, 8, d), lambda b, i: (b, jnp.maximum(i * nb - 1, 0), 0)),
                   pl.BlockSpec((None, 8, d), lambda b, i: (b, jnp.minimum((i + 1) * nb, last), 0)),
                   _mod_spec(mods)] + [_const_spec(c.shape) for c in consts]),
        out_specs=[tok, tok, tok, tok, tok, tok2, tok2, tok2],
        out_shape=[one, one, one, one, one, log_decay, two, two],
        compiler_params=_params("parallel", "parallel"),
        name="rwkv_in",
    )(x, x, x, mods, *consts)


def _scan_kernel(r_ref, v_ref, kk_ref, lw_ref, kd_ref, b_ref, s0_ref, y_ref, st_ref, state_ref, *, reverse):
    c = pl.program_id(1)
    n_rows, cs = lw_ref.shape[0], lw_ref.shape[1]
    n_pairs = state_ref.shape[1]

    @pl.when(c == 0)
    def _():
        state_ref[...] = s0_ref[...]

    sign = -1 if reverse else 1
    ti = lax.broadcasted_iota(jnp.int32, (cs, cs), 0)
    si = lax.broadcasted_iota(jnp.int32, (cs, cs), 1)
    incl = jnp.where((ti - si) * sign >= 0, 1.0, 0.0).astype(BF16)

    def scaled_streams(i):
        lw = lw_ref[i]
        l1 = lw.astype(BF16)
        rem = lw - l1.astype(F32)
        l2 = rem.astype(BF16)
        l3 = (rem - l2.astype(F32)).astype(BF16)
        cum = (jnp.dot(incl, l1, preferred_element_type=F32) + jnp.dot(incl, l2, preferred_element_type=F32)
               + jnp.dot(incl, l3, preferred_element_type=F32))
        total = cum[0:1, :] if reverse else cum[cs - 1:cs, :]
        p_inv = jnp.exp(-cum)
        to_end = jnp.exp(total - cum)
        kd = kd_ref[i]
        bv = b_ref[i]
        return dict(r_t=r_ref[i] * jnp.exp(cum), a_t=-kk_ref[i] * jnp.exp(cum - lw), k_t=kd * p_inv,
                    b_t=bv * p_inv, k_e=kd * to_end, b_e=bv * to_end, v=v_ref[i], p_tot=jnp.exp(total))

    rows = [scaled_streams(i) for i in range(n_rows)]

    assert cs & (cs - 1) == 0
    t2 = lax.broadcasted_iota(jnp.int32, (2 * cs, 2 * cs), 0) & (cs - 1)
    s2 = lax.broadcasted_iota(jnp.int32, (2 * cs, 2 * cs), 1) & (cs - 1)
    strict2 = (t2 - s2) * sign > 0
    incl2 = (t2 - s2) * sign >= 0
    even = lax.broadcasted_iota(jnp.int32, (cs, LANES), 1) < RWKV_HEAD
    vi = lax.broadcasted_iota(jnp.int32, (LANES, LANES), 0) < RWKV_HEAD
    ki = lax.broadcasted_iota(jnp.int32, (LANES, LANES), 1) < RWKV_HEAD
    same_head = vi == ki

    def stack(t):
        return jnp.concatenate([jnp.where(even, t, 0.0), jnp.where(even, 0.0, t)], axis=0)

    def twice(t):
        return jnp.concatenate([t, t], axis=0)

    bf = lambda t: t.astype(BF16)
    mm = lambda a, b: jnp.dot(a, b, preferred_element_type=F32)
    chains = [(i, p, slice(p * LANES, (p + 1) * LANES)) for i in range(n_rows) for p in range(n_pairs)]
    idx = range(len(chains))
    col = lambda name: [rows[i][name][:, sl] for i, _, sl in chains]
    a_t, r_t, b_t, k_t, b_e, k_e, v, p_tot = (col(n) for n in ("a_t", "r_t", "b_t", "k_t", "b_e", "k_e", "v", "p_tot"))
    eye = jnp.where(lax.broadcasted_iota(jnp.int32, (2 * cs, 2 * cs), 0)
                    == lax.broadcasted_iota(jnp.int32, (2 * cs, 2 * cs), 1), 1.0, 0.0)

    cross = [_dot_nt(jnp.concatenate([stack(a_t[n]), stack(r_t[n])], axis=0),
                     jnp.concatenate([stack(b_t[n]), stack(k_t[n])], axis=0)) for n in idx]
    m_ab = [jnp.where(strict2, x[:2 * cs, :2 * cs], 0.0) for x in cross]
    m_ak = [bf(jnp.where(strict2, x[:2 * cs, 2 * cs:], 0.0)) for x in cross]
    m_rb = [bf(jnp.where(incl2, x[2 * cs:, :2 * cs], 0.0)) for x in cross]
    m_rk = [bf(jnp.where(incl2, x[2 * cs:, 2 * cs:], 0.0)) for x in cross]
    v2 = [bf(twice(t)) for t in v]
    x_v = [mm(m_ak[n], v2[n]) for n in idx]
    y_v = [mm(m_rk[n], v2[n]) for n in idx]
    upd_v = [_dot_tn(v2[n], stack(k_e[n])) for n in idx]
    pw = [bf(m) for m in m_ab]
    t_inv = [eye + m for m in m_ab]
    for _ in range(max(1, (cs - 1).bit_length()) - 1):
        pw = [bf(mm(w, w)) for w in pw]
        t_inv = [t_inv[n] + mm(pw[n], bf(t_inv[n])) for n in idx]
    t_inv = [bf(t) for t in t_inv]

    states = [state_ref[i, p] for i, p, _ in chains]
    ar = [_dot_nt(jnp.concatenate([a_t[n], r_t[n]], axis=0), states[n]) for n in idx]
    u = [bf(mm(t_inv[n], bf(twice(ar[n][:cs]) + x_v[n]))) for n in idx]
    for n, (i, p, sl) in enumerate(chains):
        y2 = twice(ar[n][cs:]) + y_v[n] + mm(m_rb[n], u[n])
        y_ref[i, :, sl] = jnp.where(even, y2[:cs], y2[cs:])
        upd = upd_v[n] + _dot_tn(u[n], stack(b_e[n]))
        state_ref[i, p] = states[n] * p_tot[n] + jnp.where(same_head, upd, 0.0)

    @pl.when(c == pl.num_programs(1) - 1)
    def _():
        st_ref[...] = state_ref[...]


def _wkv_scan(r, v, kk, lw, kd, bv, s0):
    bsz, length, d = r.shape
    cs = SCAN_CHUNK
    nc = length // cs
    n_pairs = d // LANES
    nr = SCAN_ROWS if bsz % SCAN_ROWS == 0 else 1
    ys, states = [], []
    for rev in (0, 1):
        chunk = (lambda c: nc - 1 - c) if rev else (lambda c: c)
        shared = pl.BlockSpec((nr, cs, d), lambda b, c, chunk=chunk: (b, chunk(c), 0))
        per_dir = pl.BlockSpec((nr, None, cs, d), lambda b, c, chunk=chunk, rev=rev: (b, rev, chunk(c), 0))
        st_spec = pl.BlockSpec((nr, n_pairs, LANES, LANES), lambda b, c: (b, 0, 0, 0))
        y, st = pl.pallas_call(
            functools.partial(_scan_kernel, reverse=bool(rev)),
            grid=(bsz // nr, nc),
            in_specs=[shared, shared, shared, per_dir, per_dir, per_dir, st_spec],
            out_specs=[shared, st_spec],
            out_shape=[jax.ShapeDtypeStruct((bsz, length, d), F32), jax.ShapeDtypeStruct(s0[rev].shape, F32)],
            scratch_shapes=[pltpu.VMEM((nr, n_pairs, LANES, LANES), F32)],
            compiler_params=_params("parallel", "arbitrary"),
            name="wkv_scan",
        )(r, v, kk, lw, kd, bv, s0[rev])
        ys.append(y)
        states.append(st)
    return ys, states


def _rwkv_delta(yf_ref, yb_ref, bonus_ref, g_ref, lng_ref, lnb_ref, wo_ref):
    y = yf_ref[...] + yb_ref[...]
    inv_n = 1.0 / RWKV_HEAD
    yc = y - _head_sums(y) * inv_n
    rstd = lax.rsqrt(_head_sums(yc * yc) * inv_n + GN_EPS)
    yn = yc * rstd * lng_ref[...] + lnb_ref[...]
    return _dot((yn + bonus_ref[...]) * g_ref[...], wo_ref[...])


def _head_slots(w, per_head, offset=0):
    k = w.shape[0]
    w = w.reshape(k, MLA_HEADS, per_head)
    w = jnp.pad(w, ((0, 0), (0, 0), (offset, LANES - per_head - offset)))
    return w.reshape(k, MLA_HEADS * LANES)


def _mla_params(w_in, q_norm, w_qb, kv_norm, w_kvb, qn_q, qn_k, dw_w, dw_b, cn_g, cn_b, w_out):
    d = w_in.shape[0]
    o_kr = MLA_Q_RANK + MLA_KV_RANK
    mla_in = o_kr + MLA_ROPE
    w_in_p = jnp.concatenate([w_in[:, :mla_in], jnp.zeros((d, LANES - MLA_ROPE), F32), w_in[:, mla_in:]], axis=1)
    kvb = w_kvb.reshape(MLA_KV_RANK, MLA_HEADS, MLA_NOPE + MLA_V)
    w_k = _head_slots(kvb[:, :, :MLA_NOPE].reshape(MLA_KV_RANK, -1), MLA_NOPE)
    w_v = _head_slots(kvb[:, :, MLA_NOPE:].reshape(MLA_KV_RANK, -1), MLA_V)
    v_one = jnp.tile(jnp.arange(LANES) == MLA_V, MLA_HEADS).astype(F32)[None, :]
    k_one = jnp.tile(jnp.arange(LANES) == SHIFT_LANE, MLA_HEADS).astype(BF16)[None, :]
    w_out_att = jnp.pad(w_out[:MLA_HEADS * MLA_V].reshape(MLA_HEADS, MLA_V, d), ((0, 0), (0, LANES - MLA_V), (0, 0)))
    place = jnp.pad(jnp.eye(MLA_ROPE, dtype=F32), ((0, LANES - MLA_ROPE), (MLA_NOPE, LANES - QK_DIM)))
    w_kr = jnp.tile(place, (1, MLA_HEADS))
    pad_gain = lambda g: jnp.tile(jnp.pad(g, (0, LANES - QK_DIM)), MLA_HEADS)[None, :]
    norm_bound = lambda g: math.sqrt(QK_DIM) * BOUND_SLACK * jnp.max(jnp.abs(g))
    mix = MLA_HEADS * MLA_V
    return dict(
        w_in=w_in_p.astype(BF16), q_norm=q_norm[None, :], w_q=_head_slots(w_qb, QK_DIM).astype(BF16),
        kv_norm=kv_norm[None, :], w_k=w_k.astype(BF16), w_kr=w_kr.astype(BF16), w_v=w_v.astype(BF16),
        gain_q=pad_gain(qn_q) * (ATTN_SCALE * LOG2_E), gain_k=pad_gain(qn_k), v_one=v_one, k_one=k_one,
        q_norm_bound=norm_bound(qn_q) * (ATTN_SCALE * LOG2_E), k_norm_bound=norm_bound(qn_k),
        dw_w=dw_w[:, 0, :], dw_b=dw_b[None, :], cn_g=cn_g[None, :], cn_b=cn_b[None, :],
        w_out_att=w_out_att.reshape(MLA_HEADS * LANES, d).astype(BF16), w_out_conv=w_out[mix:].astype(BF16))


def _rope_tables(length, with_rope):
    n_freq = MLA_ROPE // 4
    if not with_rope:
        return jnp.ones((length, LANES), F32), jnp.zeros((length, LANES), F32), jnp.zeros((length, LANES), F32)
    t = jnp.arange(length)
    inv = ROPE_THETA ** (-jnp.arange(n_freq, dtype=F32) / n_freq)
    ang_r = (t // GRID_W).astype(F32)[:, None] * inv
    ang_c = (t % GRID_W).astype(F32)[:, None] * inv
    zeros = jnp.zeros((length, n_freq), F32)

    def slot(nope_val, r_first, r_second, c_first, c_second):
        nope = jnp.full((length, MLA_NOPE), nope_val, F32)
        pad = jnp.full((length, LANES - QK_DIM), nope_val, F32)
        return jnp.concatenate([nope, r_first, r_second, c_first, c_second, pad], axis=1)

    cos_r, sin_r, cos_c, sin_c = jnp.cos(ang_r), jnp.sin(ang_r), jnp.cos(ang_c), jnp.sin(ang_c)
    return (slot(1.0, cos_r, cos_r, cos_c, cos_c),
            slot(0.0, -sin_r, zeros, -sin_c, zeros),
            slot(0.0, zeros, sin_r, zeros, sin_c))


def _rwkv_params(x_mix, w_r, w_k, w_v, w0, w1, w2, a0, a1, a2, g1, g2, k_k, k_a, r_k, ln_g, ln_b, w_o):
    d = w_r.shape[0]
    bf = lambda w: w.astype(BF16)
    return dict(x_mix=x_mix, w_r=bf(w_r), w_k=bf(w_k), w_v=bf(w_v), w0=w0, w1=bf(w1), w2=bf(w2), a0=a0,
                a1=bf(a1), a2=bf(a2), g1=bf(g1), g2=bf(g2), k_k=k_k[None, :], k_a=k_a[None, :],
                r_k=r_k.reshape(1, d), ln_g=ln_g[None, :], ln_b=ln_b[None, :], w_o=bf(w_o))


def _mla_conv_layer(x, ctx, m_l, m_c, p):
    length, lc = x.shape[1], ctx.shape[1]
    q_l, k_l, v_l, hg_l = _mla_in(x, m_l, p, _rope_tables(length, True))
    q_c, k_c, v_c, hg_c = _mla_in(ctx, m_c, p, _rope_tables(lc, False))
    att_l = _attention_latent(q_l, k_c, v_c, k_l, v_l, p["q_norm_bound"], p["k_norm_bound"])
    att_c = _attention_online(q_c, k_c, v_c)
    consts = (p["w_out_att"], p["w_out_conv"])
    return (dict(delta=_mix_delta, tokens=(att_l, _conformer_conv(hg_l, p)), consts=consts),
            dict(delta=_mix_delta, tokens=(att_c, _conformer_conv(hg_c, p)), consts=consts))


def _rwkv_layer(x, ctx, m_l, m_c, p):
    bsz, _, d = x.shape
    r_c, v_c, kk_c, _, _, lw_c, kd_c, b_c = _rwkv_in(ctx, m_c, p)
    r_l, v_l, kk_l, g_l, bonus_l, lw_l, kd_l, b_l = _rwkv_in(x, m_l, p)
    s0 = [jnp.zeros((bsz, d // LANES, LANES, LANES), F32)] * 2
    _, s_ctx = _wkv_scan(r_c, v_c, kk_c, lw_c, kd_c, b_c, s0)
    y_l, _ = _wkv_scan(r_l, v_l, kk_l, lw_l, kd_l, b_l, s_ctx)
    return dict(delta=_rwkv_delta, tokens=(*y_l, bonus_l, g_l), consts=(p["ln_g"], p["ln_b"], p["w_o"]))


def kernel(x, c, ctx, c_ctx, ada_w, ada_b, ffn_w1, ffn_w3, ffn_w2, mla_w_in, mla_q_norm, mla_w_qb, mla_kv_norm, mla_w_kvb, mla_qk_norm_q, mla_qk_norm_k, conv_dw_w, conv_dw_b, conv_norm_g, conv_norm_b, mix_w_out, rwkv_x_mix, rwkv_w_r, rwkv_w_k, rwkv_w_v, rwkv_w0, rwkv_w1, rwkv_w2, rwkv_a0, rwkv_a1, rwkv_a2, rwkv_g1, rwkv_g2, rwkv_k_k, rwkv_k_a, rwkv_r_k, rwkv_ln_g, rwkv_ln_b, rwkv_w_o):
    bsz = x.shape[0]
    depth = ada_w.shape[0]
    assert depth == 2, "layer 0 is the MLA/conv layer, layer 1 the (last) RWKV layer"
    cond = jnp.concatenate([c, c_ctx[None, :], jnp.zeros((16 - bsz - 1, c.shape[1]), F32)], axis=0)
    w1, w3, w2 = ffn_w1.astype(BF16), ffn_w3.astype(BF16), ffn_w2.astype(BF16)
    mla_p = _mla_params(mla_w_in[0], mla_q_norm[0], mla_w_qb[0], mla_kv_norm[0], mla_w_kvb[0], mla_qk_norm_q[0],
                        mla_qk_norm_k[0], conv_dw_w[0], conv_dw_b[0], conv_norm_g[0], conv_norm_b[0], mix_w_out[0])
    rwkv_p = _rwkv_params(rwkv_x_mix[0], rwkv_w_r[0], rwkv_w_k[0], rwkv_w_v[0], rwkv_w0[0], rwkv_w1[0],
                          rwkv_w2[0], rwkv_a0[0], rwkv_a1[0], rwkv_a2[0], rwkv_g1[0], rwkv_g2[0], rwkv_k_k[0],
                          rwkv_k_a[0], rwkv_r_k[0], rwkv_ln_g[0], rwkv_ln_b[0], rwkv_w_o[0])
    for i in range(depth):
        mods = _ada_mods(cond, ada_w[i], ada_b[i])
        m_l, m_c = mods[:bsz], mods[bsz:bsz + 1]
        x = _half_ffn(x, m_l, 0, w1[i, 0], w3[i, 0], w2[i, 0])
        ctx = _half_ffn(ctx, m_c, 0, w1[i, 0], w3[i, 0], w2[i, 0])
        if i == 0:
            mix_l, mix_c = _mla_conv_layer(x, ctx, m_l, m_c, mla_p)
            ctx = _half_ffn(ctx, m_c, 6, w1[i, 1], w3[i, 1], w2[i, 1], **mix_c)
        else:
            mix_l = _rwkv_layer(x, ctx, m_l, m_c, rwkv_p)
        x = _half_ffn(x, m_l, 6, w1[i, 1], w3[i, 1], w2[i, 1], **mix_l)
    return x
```
